```python
import jax, jax.numpy as jnp
from jax import lax
import numpy as np

D_MODEL = 1024
BATCH = 2
SEQ = 8192
DEPTH = 1

HEAD_DIM = 64
ROPE_DIM = HEAD_DIM // 4
ROPE_THETA = 500000.0
NORM_EPS = 1e-6
D_FF = 2816

NSA_HEADS = 8
NSA_KV_GROUPS = 2
NSA_HPG = NSA_HEADS // NSA_KV_GROUPS
CMP_BLOCK = 32
CMP_STRIDE = 16
CMP_HIDDEN = 2 * HEAD_DIM
SLC_BLOCK = 64
SLC_TOPN = 16
WIN = 512
Q_BLOCK = 128
FORCE = 1e9

DIL_PAIRS = ((128, 1), (512, 4), (2048, 16))
DIL_GROUPS = len(DIL_PAIRS)
DIL_HEADS = 4
DIL_BLOCK = 128

NSA_Q = NSA_HEADS * HEAD_DIM
NSA_KV = NSA_KV_GROUPS * HEAD_DIM
DIL_W = DIL_HEADS * HEAD_DIM
IN_SIZES = (NSA_Q, 6 * NSA_KV, 3 * NSA_HEADS, 3 * DIL_GROUPS * DIL_W, 2 * D_MODEL)
N_IN = sum(IN_SIZES)
NEG = -1e30

kernel_name = "hybrid_nsa_dilated_macaron_block"


def rmsnorm(x, g):
    x32 = x.astype(jnp.float32)
    y = x32 * lax.rsqrt(jnp.mean(x32 * x32, axis=-1, keepdims=True) + NORM_EPS) * g.astype(jnp.float32)
    return y.astype(x.dtype)


def swiglu(h, w_in, w_out):
    u, v = jnp.split(h @ w_in, 2, axis=-1)
    return (jax.nn.silu(u) * v) @ w_out


def rope_tables(positions):
    inv = jnp.power(ROPE_THETA, -jnp.arange(0, ROPE_DIM, 2, dtype=jnp.float32) / ROPE_DIM)
    ang = positions.astype(jnp.float32)[..., None] * inv
    return jnp.cos(ang), jnp.sin(ang)


def rope_partial(x, cos, sin):
    shp = cos.shape[:2] + (1,) * (x.ndim - 3) + cos.shape[-1:]
    c = cos.reshape(shp).astype(x.dtype)
    s = sin.reshape(shp).astype(x.dtype)
    half = ROPE_DIM // 2
    x1, x2, xp = x[..., :half], x[..., half:ROPE_DIM], x[..., ROPE_DIM:]
    return jnp.concatenate([x1 * c - x2 * s, x2 * c + x1 * s, xp], axis=-1)


def masked_softmax_stats(s, mask):
    s = s.astype(jnp.float32)
    m = jnp.max(jnp.where(mask, s, NEG), axis=-1, keepdims=True)
    e = jnp.exp(jnp.where(mask, s - m, NEG))
    d = jnp.sum(e, axis=-1, keepdims=True)
    return e / jnp.where(d > 0, d, 1.0), m, d


def compress(tok, pos, w1, w2):
    B, T, G, dh = tok.shape
    seg = tok.reshape(B, T // CMP_STRIDE, CMP_STRIDE, G, dh)
    blocks = jnp.concatenate([seg[:, :-1], seg[:, 1:]], axis=2) + pos[None, None, :, None, :]
    n_cmp = blocks.shape[1]
    flat = blocks.transpose(0, 1, 3, 2, 4).reshape(B, n_cmp, G, CMP_BLOCK * dh)
    return jax.nn.silu(flat @ w1) @ w2


def slc_map(n_cmp, n_slc):
    i = np.arange(n_cmp)[:, None]
    j = np.arange(n_slc)[None, :]
    lo = np.maximum(i * CMP_STRIDE, j * SLC_BLOCK)
    hi = np.minimum(i * CMP_STRIDE + CMP_BLOCK, (j + 1) * SLC_BLOCK)
    return (np.maximum(hi - lo, 0) / CMP_BLOCK).astype(np.float32)


def nsa_attention(q, q_rope, kc, vc, ks, vs, kw, vw, gates):
    B, T, G, Hg, dh = q.shape
    n_cmp = kc.shape[1]
    n_slc = T // SLC_BLOCK
    n_sel = min(SLC_TOPN, n_slc)
    scale = HEAD_DIM ** -0.5
    cmap = jnp.asarray(slc_map(n_cmp, n_slc))
    cmp_end = jnp.arange(n_cmp) * CMP_STRIDE + CMP_BLOCK - 1
    blk = jnp.arange(n_slc)
    ks_blk = ks.reshape(B, n_slc, SLC_BLOCK, G, dh).transpose(0, 3, 1, 2, 4)
    vs_blk = vs.reshape(B, n_slc, SLC_BLOCK, G, dh).transpose(0, 3, 1, 2, 4)
    kw_p = jnp.pad(kw, ((0, 0), (WIN, 0), (0, 0), (0, 0)))
    vw_p = jnp.pad(vw, ((0, 0), (WIN, 0), (0, 0), (0, 0)))
    bi = jnp.arange(B)[:, None, None, None]
    gi = jnp.arange(G)[None, :, None, None]

    def block(c):
        t0 = c * Q_BLOCK
        t = t0 + jnp.arange(Q_BLOCK)
        qc = lax.dynamic_slice_in_dim(q, t0, Q_BLOCK, axis=1)
        qr = lax.dynamic_slice_in_dim(q_rope, t0, Q_BLOCK, axis=1)
        gc = lax.dynamic_slice_in_dim(gates, t0, Q_BLOCK, axis=1)
        s = jnp.einsum('bqghd,bngd->bghqn', qc, kc) * scale
        p_c, _, _ = masked_softmax_stats(s, cmp_end[None, :] <= t[:, None])
        o_c = jnp.einsum('bghqn,bngd->bqghd', p_c.astype(vc.dtype), vc)
        imp = jnp.einsum('bghqn,nj->bgqj', p_c, cmap)
        cur = (t // SLC_BLOCK)[:, None]
        valid = blk[None, :] <= cur
        forced = (blk[None, :] == 0) | (blk[None, :] == cur) | (blk[None, :] == cur - 1)
        imp = jnp.where(forced, FORCE, jnp.where(valid, imp, -FORCE))
        top_val, idx = lax.top_k(imp, n_sel)
        sel_ok = top_val > -0.5 * FORCE
        kg = ks_blk[bi, gi, idx]
        vg = vs_blk[bi, gi, idx]
        s = jnp.einsum('bqghd,bgqnkd->bghqnk', qr, kg) * scale
        kpos = idx[..., None] * SLC_BLOCK + jnp.arange(SLC_BLOCK)
        mask_s = sel_ok[..., None] & (kpos <= t[None, None, :, None, None])
        p_s, _, _ = masked_softmax_stats(s.reshape(B, G, Hg, Q_BLOCK, n_sel * SLC_BLOCK),
                                         mask_s.reshape(B, G, 1, Q_BLOCK, n_sel * SLC_BLOCK))
        p_s = p_s.reshape(B, G, Hg, Q_BLOCK, n_sel, SLC_BLOCK)
        o_s = jnp.einsum('bghqnk,bgqnkd->bqghd', p_s.astype(vg.dtype), vg)
        kwc = lax.dynamic_slice_in_dim(kw_p, t0, Q_BLOCK + WIN, axis=1)
        vwc = lax.dynamic_slice_in_dim(vw_p, t0, Q_BLOCK + WIN, axis=1)
        s = jnp.einsum('bqghd,bkgd->bghqk', qr, kwc) * scale
        kp = t0 - WIN + jnp.arange(Q_BLOCK + WIN)
        diff = t[:, None] - kp[None, :]
        p_w, _, _ = masked_softmax_stats(s, (diff >= 0) & (diff < WIN) & (kp[None, :] >= 0))
        o_w = jnp.einsum('bghqk,bkgd->bqghd', p_w.astype(vwc.dtype), vwc)
        return gc[..., 0:1] * o_c + gc[..., 1:2] * o_s + gc[..., 2:3] * o_w

    out = lax.map(block, jnp.arange(T // Q_BLOCK))
    return out.transpose(1, 0, 2, 3, 4, 5).reshape(B, T, G * Hg * dh)


def dilated_group(q, k, v, dilation, span):
    B, T, H, dh = q.shape
    L = T // dilation
    Lp = -(-L // DIL_BLOCK) * DIL_BLOCK
    nb = Lp // DIL_BLOCK
    scale = HEAD_DIM ** -0.5

    def to_sub(a):
        a = a.reshape(B, L, dilation, H, dh).transpose(0, 2, 3, 1, 4)
        return jnp.pad(a, ((0, 0), (0, 0), (0, 0), (0, Lp - L), (0, 0)))

    def with_prev(a):
        a = jnp.pad(a, ((0, 0), (0, 0), (0, 0), (DIL_BLOCK, 0), (0, 0)))
        a = a.reshape(B, dilation, H, nb + 1, DIL_BLOCK, dh)
        return jnp.concatenate([a[:, :, :, :-1], a[:, :, :, 1:]], axis=4)

    qb = to_sub(q).reshape(B, dilation, H, nb, DIL_BLOCK, dh)
    kb = with_prev(to_sub(k))
    vb = with_prev(to_sub(v))
    s = jnp.einsum('brhnqd,brhnkd->brhnqk', qb, kb) * scale
    qi = jnp.arange(nb)[:, None] * DIL_BLOCK + jnp.arange(DIL_BLOCK)[None, :]
    kj = jnp.arange(nb)[:, None] * DIL_BLOCK - DIL_BLOCK + jnp.arange(2 * DIL_BLOCK)[None, :]
    diff = qi[:, :, None] - kj[:, None, :]
    mask = (diff >= 0) & (diff <= span) & (kj[:, None, :] >= 0)
    p, m, d = masked_softmax_stats(s, mask)
    o = jnp.einsum('brhnqk,brhnkd->brhnqd', p.astype(vb.dtype), vb)
    o = o.reshape(B, dilation, H, Lp, dh)[:, :, :, :L].transpose(0, 3, 1, 2, 4).reshape(B, T, H, dh)
    lse = (m + jnp.log(d))[..., 0].reshape(B, dilation, H, Lp)[..., :L]
    lse = lse.transpose(0, 3, 1, 2).reshape(B, T, H)
    return o, lse


def hybrid_mixer(h, cos, sin, w_in, cmp_pos, cmp_w1, cmp_w2, w_branch_nsa, w_branch_dil, w_out):
    B, T, _ = h.shape
    G, Hg, dh = NSA_KV_GROUPS, NSA_HPG, HEAD_DIM
    cuts = np.cumsum(IN_SIZES)[:-1].tolist()
    q_n, kv_n, g_n, qkv_d, g_m = jnp.split(h @ w_in, cuts, axis=-1)
    q_n = q_n.reshape(B, T, NSA_HEADS, dh)
    q_r = rope_partial(q_n, cos, sin).reshape(B, T, G, Hg, dh)
    q_n = q_n.reshape(B, T, G, Hg, dh)
    kv_n = kv_n.reshape(B, T, 6, G, dh)
    kc = compress(kv_n[:, :, 0], cmp_pos[0], cmp_w1[0], cmp_w2[0])
    vc = compress(kv_n[:, :, 1], cmp_pos[1], cmp_w1[1], cmp_w2[1])
    ks = rope_partial(kv_n[:, :, 2], cos, sin)
    kw = rope_partial(kv_n[:, :, 4], cos, sin)
    gates = jax.nn.sigmoid(g_n.astype(jnp.float32)).astype(h.dtype).reshape(B, T, G, Hg, 3)
    y_nsa = nsa_attention(q_n, q_r, kc, vc, ks, kv_n[:, :, 3], kw, kv_n[:, :, 5], gates)
    qkv_d = qkv_d.reshape(B, T, 3, DIL_GROUPS, DIL_HEADS, dh)
    q_d = rope_partial(qkv_d[:, :, 0], cos, sin)
    k_d = rope_partial(qkv_d[:, :, 1], cos, sin)
    outs, lses = [], []
    for gidx, (window, dil) in enumerate(DIL_PAIRS):
        o, lse = dilated_group(q_d[:, :, gidx], k_d[:, :, gidx], qkv_d[:, :, 2, gidx], dil, window // dil)
        outs.append(o)
        lses.append(lse)
    alpha = jax.nn.softmax(jnp.stack(lses, axis=2), axis=2).astype(h.dtype)
    y_dil = jnp.sum(alpha[..., None] * jnp.stack(outs, axis=2), axis=2).reshape(B, T, DIL_W)
    g_m = jax.nn.sigmoid(g_m.astype(jnp.float32)).astype(h.dtype).reshape(B, T, 2, D_MODEL)
    merged = g_m[:, :, 0] * (y_nsa @ w_branch_nsa) + g_m[:, :, 1] * (y_dil @ w_branch_dil)
    return merged @ w_out


def setup_inputs(seed: int = 0) -> dict:
    key = jax.random.key(seed)
    ks = jax.random.split(key, 20)
    f32 = jnp.float32

    def w(k, shape, fan_in):
        return jax.random.normal(k, shape, f32) * fan_in ** -0.5

    def gain(k, shape):
        return 1.0 + 0.02 * jax.random.normal(k, shape, f32)

    x = jax.random.normal(ks[0], (BATCH, SEQ, D_MODEL), f32)
    offset = jax.random.randint(ks[1], (BATCH, 1), 0, 4096, dtype=jnp.int32)
    positions = (offset + jnp.arange(SEQ, dtype=jnp.int32)[None, :]).astype(jnp.int32)
    return {
        "x": x,
        "positions": positions,
        "norm_ffn1": gain(ks[2], (DEPTH, D_MODEL)),
        "ffn1_w_in": w(ks[3], (DEPTH, D_MODEL, 2 * D_FF), D_MODEL),
        "ffn1_w_out": w(ks[4], (DEPTH, D_FF, D_MODEL), D_FF),
        "norm_mix": gain(ks[5], (DEPTH, D_MODEL)),
        "w_in": w(ks[6], (DEPTH, D_MODEL, N_IN), D_MODEL),
        "cmp_pos": 0.02 * jax.random.normal(ks[7], (DEPTH, 2, CMP_BLOCK, HEAD_DIM), f32),
        "cmp_w1": w(ks[8], (DEPTH, 2, CMP_BLOCK * HEAD_DIM, CMP_HIDDEN), CMP_BLOCK * HEAD_DIM),
        "cmp_w2": w(ks[9], (DEPTH, 2, CMP_HIDDEN, HEAD_DIM), CMP_HIDDEN),
        "w_branch_nsa": w(ks[10], (DEPTH, NSA_Q, D_MODEL), NSA_Q),
        "w_branch_dil": w(ks[11], (DEPTH, DIL_W, D_MODEL), DIL_W),
        "w_out": w(ks[12], (DEPTH, D_MODEL, D_MODEL), D_MODEL),
        "norm_ffn2": gain(ks[13], (DEPTH, D_MODEL)),
        "ffn2_w_in": w(ks[14], (DEPTH, D_MODEL, 2 * D_FF), D_MODEL),
        "ffn2_w_out": w(ks[15], (DEPTH, D_FF, D_MODEL), D_FF),
        "norm_final": gain(ks[16], (D_MODEL,)),
    }


def reference(x, positions, norm_ffn1, ffn1_w_in, ffn1_w_out, norm_mix, w_in, cmp_pos, cmp_w1, cmp_w2,
              w_branch_nsa, w_branch_dil, w_out, norm_ffn2, ffn2_w_in, ffn2_w_out, norm_final):
    cos, sin = rope_tables(positions)
    for l in range(DEPTH):
        x = x + 0.5 * swiglu(rmsnorm(x, norm_ffn1[l]), ffn1_w_in[l], ffn1_w_out[l])
        x = x + hybrid_mixer(rmsnorm(x, norm_mix[l]), cos, sin, w_in[l], cmp_pos[l], cmp_w1[l], cmp_w2[l],
                             w_branch_nsa[l], w_branch_dil[l], w_out[l])
        x = x + 0.5 * swiglu(rmsnorm(x, norm_ffn2[l]), ffn2_w_in[l], ffn2_w_out[l])
    return rmsnorm(x, norm_final)
```

```python
import functools

import numpy as np
import jax
import jax.numpy as jnp
from jax import lax
from jax.experimental import pallas as pl
from jax.experimental.pallas import tpu as pltpu

HEAD_DIM = 64
ROPE_DIM = HEAD_DIM // 4
ROPE_THETA = 500000.0
NORM_EPS = 1e-6

NSA_HEADS = 8
NSA_KV_GROUPS = 2
NSA_HPG = NSA_HEADS // NSA_KV_GROUPS
CMP_BLOCK = 32
CMP_STRIDE = 16
SLC_BLOCK = 64
SLC_TOPN = 16
WIN = 512
Q_BLOCK = 128
FORCE = 1e9
NEG = -1e30

DIL_PAIRS = ((128, 1), (512, 4), (2048, 16))
DIL_GROUPS = len(DIL_PAIRS)
DIL_HEADS = 4
DIL_BLOCK = 128

NSA_Q = NSA_HEADS * HEAD_DIM
NSA_KV = NSA_KV_GROUPS * HEAD_DIM
DIL_W = DIL_HEADS * HEAD_DIM

LANES = 128
SEL_KEY_TILE = 512
VMEM_LIMIT = 56 * 1024 * 1024

F32 = jnp.float32
BF16 = jnp.bfloat16
SCALE = HEAD_DIM ** -0.5


def _dot(a, b):
    return jnp.dot(a, b, preferred_element_type=F32)


def _dot_nt(a, b):
    return lax.dot_general(a, b, (((1,), (1,)), ((), ())), preferred_element_type=F32)


def _rmsnorm(x, g):
    return x * lax.rsqrt(jnp.mean(x * x, axis=-1, keepdims=True) + NORM_EPS) * g


def _params(sem):
    return pltpu.CompilerParams(dimension_semantics=sem, vmem_limit_bytes=VMEM_LIMIT)


def _ffn_kernel(*refs, final):
    if final:
        x_ref, g_ref, wu_ref, wv_ref, wo_ref, gf_ref, o_ref, h_scr, acc_scr = refs
    else:
        x_ref, g_ref, wu_ref, wv_ref, wo_ref, o_ref, h_scr, acc_scr = refs
    j = pl.program_id(1)

    @pl.when(j == 0)
    def _():
        h_scr[...] = _rmsnorm(x_ref[...], g_ref[...]).astype(BF16)
        acc_scr[...] = jnp.zeros_like(acc_scr)

    h = h_scr[...]
    u = _dot(h, wu_ref[...])
    v = _dot(h, wv_ref[...])
    a = (u * jax.nn.sigmoid(u)) * v
    acc_scr[...] += _dot(a.astype(BF16), wo_ref[...])

    @pl.when(j == pl.num_programs(1) - 1)
    def _():
        y = x_ref[...] + 0.5 * acc_scr[...]
        if final:
            y = _rmsnorm(y, gf_ref[...])
        o_ref[...] = y


def _ffn(x2, gain, w_in, w_out, final_gain=None, tm=1024, tf=256):
    m, d = x2.shape
    dff = w_out.shape[0]
    nf = dff // tf
    final = final_gain is not None
    in_specs = [
        pl.BlockSpec((tm, d), lambda i, j: (i, 0)),
        pl.BlockSpec((1, d), lambda i, j: (0, 0)),
        pl.BlockSpec((d, tf), lambda i, j: (0, j)),
        pl.BlockSpec((d, tf), lambda i, j: (0, j + nf)),
        pl.BlockSpec((tf, d), lambda i, j: (j, 0)),
    ]
    args = [x2, gain.reshape(1, d), w_in, w_in, w_out]
    if final:
        in_specs.append(pl.BlockSpec((1, d), lambda i, j: (0, 0)))
        args.append(final_gain.reshape(1, d))
    return pl.pallas_call(
        functools.partial(_ffn_kernel, final=final),
        grid=(m // tm, nf),
        in_specs=in_specs,
        out_specs=pl.BlockSpec((tm, d), lambda i, j: (i, 0)),
        out_shape=jax.ShapeDtypeStruct((m, d), F32),
        scratch_shapes=[pltpu.VMEM((tm, d), BF16), pltpu.VMEM((tm, d), F32)],
        compiler_params=_params(("parallel", "arbitrary")),
        name="ffn_final" if final else "ffn",
    )(*args)


def _rope_coeffs(pos_col, inv_lane):
    ang = pos_col.astype(F32) * inv_lane
    c, s = jnp.cos(ang), jnp.sin(ang)
    d = lax.broadcasted_iota(jnp.int32, ang.shape, 1) & (HEAD_DIM - 1)
    half = ROPE_DIM // 2
    coef_c = jnp.where(d < ROPE_DIM, c, 1.0)
    coef_hi = jnp.where(d < half, -s, 0.0)
    coef_lo = jnp.where((d >= half) & (d < ROPE_DIM), s, 0.0)
    return coef_c, coef_hi, coef_lo


def _proj_kernel(*refs, rope):
    if rope:
        x_ref, g_ref, w_ref, pos_ref, inv_ref, o_ref = refs
    else:
        x_ref, g_ref, w_ref, o_ref = refs
    h = _rmsnorm(x_ref[...], g_ref[...]).astype(BF16)
    y = _dot(h, w_ref[...])
    if not rope:
        o_ref[...] = y.astype(o_ref.dtype)
        return
    coef_c, coef_hi, coef_lo = _rope_coeffs(pos_ref[...], inv_ref[...])
    half = ROPE_DIM // 2
    for t in range(y.shape[1] // LANES):
        yt = y[:, t * LANES:(t + 1) * LANES]
        hi = pltpu.roll(yt, LANES - half, axis=1)
        lo = pltpu.roll(yt, half, axis=1)
        o_ref[:, t * LANES:(t + 1) * LANES] = (yt * coef_c + hi * coef_hi + lo * coef_lo).astype(o_ref.dtype)


def _proj(x2, gain, w, out_dtype, pos=None, inv_lane=None, tm=512):
    m, d = x2.shape
    n = w.shape[1]
    rope = pos is not None
    in_specs = [
        pl.BlockSpec((tm, d), lambda i: (i, 0)),
        pl.BlockSpec((1, d), lambda i: (0, 0)),
        pl.BlockSpec((d, n), lambda i: (0, 0)),
    ]
    args = [x2, gain.reshape(1, d), w]
    if rope:
        in_specs += [pl.BlockSpec((tm, 1), lambda i: (i, 0)), pl.BlockSpec((1, LANES), lambda i: (0, 0))]
        args += [pos, inv_lane]
    return pl.pallas_call(
        functools.partial(_proj_kernel, rope=rope),
        grid=(m // tm,),
        in_specs=in_specs,
        out_specs=pl.BlockSpec((tm, n), lambda i: (i, 0)),
        out_shape=jax.ShapeDtypeStruct((m, n), out_dtype),
        compiler_params=_params(("parallel",)),
        name="proj_rope" if rope else ("proj_f32" if out_dtype == F32 else "proj_bf16"),
    )(*args)


def _compress_kernel(seg_ref, pos_ref, w1_ref, w2_ref, o_ref):
    seg = seg_ref[0, 0, 0]
    a = _dot((seg + pos_ref[0, 0]).astype(BF16), w1_ref[0, 0])
    b = _dot((seg + pos_ref[0, 1]).astype(BF16), w1_ref[0, 1])
    n_seg = seg.shape[0]
    pre = a + pltpu.roll(b, n_seg - 1, axis=0)
    hid = pre * jax.nn.sigmoid(pre)
    o_ref[0, 0, 0] = _dot(hid.astype(BF16), w2_ref[0])


def _compress(seg, pos, w1, w2):
    _, b, g, n_seg, k = seg.shape
    hid = w1.shape[-1]
    dh = w2.shape[-1]
    return pl.pallas_call(
        _compress_kernel,
        grid=(2, b, g),
        in_specs=[
            pl.BlockSpec((1, 1, 1, n_seg, k), lambda c, i, j: (c, i, j, 0, 0)),
            pl.BlockSpec((1, 2, 1, k), lambda c, i, j: (c, 0, 0, 0)),
            pl.BlockSpec((1, 2, k, hid), lambda c, i, j: (c, 0, 0, 0)),
            pl.BlockSpec((1, hid, dh), lambda c, i, j: (c, 0, 0)),
        ],
        out_specs=pl.BlockSpec((1, 1, 1, n_seg, dh), lambda c, i, j: (c, i, j, 0, 0)),
        out_shape=jax.ShapeDtypeStruct((2, b, g, n_seg, dh), F32),
        compiler_params=_params(("parallel", "parallel", "parallel")),
        name="compress",
    )(seg, pos, w1, w2)


def _stack_heads(q, g, lane_grp):
    parts = []
    for h in range(NSA_HPG):
        hg = g * NSA_HPG + h
        tile = q[:, (hg // 2) * LANES:(hg // 2 + 1) * LANES]
        if hg % 2 != g:
            tile = pltpu.roll(tile, HEAD_DIM, axis=1)
        parts.append(jnp.where(lane_grp == g, tile * SCALE, 0.0))
    return jnp.concatenate(parts, axis=0).astype(BF16)


def _nsa_kernel(qp_ref, qr_ref, gate_ref, kc_ref, vc_ref, ks_ref, vs_ref, kw_ref, vw_ref, cmap_ref, et_ref,
                o_ref, impt_scr, *, n_sel, win_keys):
    qb = pl.program_id(1)
    t0 = qb * Q_BLOCK
    nq = Q_BLOCK
    hg = NSA_HPG
    n_cmp_pad = kc_ref.shape[1]
    t_col = t0 + lax.broadcasted_iota(jnp.int32, (nq, 1), 0)
    lane = lax.broadcasted_iota(jnp.int32, (nq, LANES), 1)
    lane_grp = lane >> 6
    sub = lax.broadcasted_iota(jnp.int32, (LANES, nq), 0)

    q_plain = qp_ref[0]
    q_rope = qr_ref[0].astype(F32)
    gates = jax.nn.sigmoid(gate_ref[0])

    cmp_end = lax.broadcasted_iota(jnp.int32, (nq, n_cmp_pad), 1) * CMP_STRIDE + (CMP_BLOCK - 1)
    mask_c = (cmp_end <= t_col)[None]
    cur = t_col >> 6
    forced = (lane == 0) | (lane == cur) | (lane == cur - 1)
    valid = lane <= cur

    win_start = pl.multiple_of(jnp.maximum(t0 - WIN, 0), Q_BLOCK)
    kpos_w = win_start + lax.broadcasted_iota(jnp.int32, (nq, win_keys), 1)
    diff_w = t_col - kpos_w
    mask_w = ((diff_w >= 0) & (diff_w < WIN))[None]

    n_tiles = (t0 + nq + SEL_KEY_TILE - 1) // SEL_KEY_TILE

    def softmax_pv(s3, mask3, v):
        m = jnp.max(jnp.where(mask3, s3, NEG), axis=-1, keepdims=True)
        e = jnp.exp(jnp.where(mask3, s3 - m, NEG))
        d = jnp.sum(e, axis=-1, keepdims=True)
        p = e / jnp.where(d > 0, d, 1.0)
        return p, _dot(p.reshape(hg * nq, -1).astype(BF16), v)

    for g in range(NSA_KV_GROUPS):
        qs_plain = _stack_heads(q_plain, g, lane_grp)
        qs_rope = _stack_heads(q_rope, g, lane_grp)

        s_c = _dot_nt(qs_plain, kc_ref[0]).reshape(hg, nq, n_cmp_pad)
        p_c, o_c = softmax_pv(s_c, mask_c, vc_ref[0])
        imp = jnp.dot(jnp.sum(p_c, axis=0), cmap_ref[...], preferred_element_type=F32,
                      precision=lax.Precision.HIGHEST)
        imp = jnp.where(forced, FORCE, jnp.where(valid, imp, -FORCE))
        imp_t = imp.T
        impt_scr[...] = imp_t

        def rank_body(i, cnt):
            row = impt_scr[pl.ds(i, 1), :]
            ahead = (row > imp_t) | ((row == imp_t) & (sub > i))
            return cnt + ahead.astype(jnp.int32)

        cnt = lax.fori_loop(0, LANES, rank_body, jnp.zeros((LANES, nq), jnp.int32), unroll=8)
        sel_t = jnp.where((cnt < n_sel) & (imp_t > -0.5 * FORCE), 1.0, 0.0)
        sel = sel_t.T.astype(BF16)

        def sel_body(kt, carry):
            m, l, acc = carry
            k0 = pl.multiple_of(kt * SEL_KEY_TILE, SEL_KEY_TILE)
            k = ks_ref[0, pl.ds(k0, SEL_KEY_TILE), :]
            v = vs_ref[0, pl.ds(k0, SEL_KEY_TILE), :]
            expand = et_ref[pl.ds(k0, SEL_KEY_TILE), :]
            s3 = _dot_nt(qs_rope, k).reshape(hg, nq, SEL_KEY_TILE)
            picked = _dot_nt(sel, expand) > 0.5
            kpos = k0 + lax.broadcasted_iota(jnp.int32, (nq, SEL_KEY_TILE), 1)
            mask3 = (picked & (kpos <= t_col))[None]
            m_new = jnp.maximum(m, jnp.max(jnp.where(mask3, s3, NEG), axis=-1, keepdims=True))
            alpha = jnp.exp(m - m_new)
            e = jnp.exp(jnp.where(mask3, s3 - m_new, NEG))
            l_new = alpha * l + jnp.sum(e, axis=-1, keepdims=True)
            pv = _dot(e.reshape(hg * nq, SEL_KEY_TILE).astype(BF16), v)
            acc_new = alpha.reshape(hg * nq, 1) * acc + pv
            return m_new, l_new, acc_new

        m0 = jnp.full((hg, nq, 1), NEG, F32)
        l0 = jnp.zeros((hg, nq, 1), F32)
        acc0 = jnp.zeros((hg * nq, LANES), F32)
        _, l_s, acc_s = lax.fori_loop(0, n_tiles, sel_body, (m0, l0, acc0))
        l_s = l_s.reshape(hg * nq, 1)
        o_s = acc_s / jnp.where(l_s > 0, l_s, 1.0)

        kw = kw_ref[0, pl.ds(win_start, win_keys), :]
        vw = vw_ref[0, pl.ds(win_start, win_keys), :]
        s_w = _dot_nt(qs_rope, kw).reshape(hg, nq, win_keys)
        _, o_w = softmax_pv(s_w, mask_w, vw)

        for h in range(hg):
            hglob = g * hg + h
            col = g * hg * 3 + h * 3
            rows = slice(h * nq, (h + 1) * nq)
            mix = (gates[:, col:col + 1] * o_c[rows] + gates[:, col + 1:col + 2] * o_s[rows]
                   + gates[:, col + 2:col + 3] * o_w[rows])
            if hglob % 2 != g:
                mix = pltpu.roll(mix, HEAD_DIM, axis=1)
            if hglob % 2 == 0:
                even = mix
            else:
                tile = jnp.where(lane < HEAD_DIM, even, mix)
                c = hglob // 2
                o_ref[0, :, c * LANES:(c + 1) * LANES] = tile.astype(o_ref.dtype)


def _nsa(pa3, r3, v3, kc, vc, cmap, expand_t, *, qp_blk, gate_blk, ks_blk, kw_blk, vs_blk, vw_blk):
    b, t, _ = pa3.shape
    n_cmp_pad = kc.shape[1]
    n_slc = t // SLC_BLOCK
    n_sel = min(SLC_TOPN, n_slc)
    win_keys = min(WIN + Q_BLOCK, t)
    full_t = lambda blk: pl.BlockSpec((1, t, LANES), lambda i, j, blk=blk: (i, 0, blk))
    return pl.pallas_call(
        functools.partial(_nsa_kernel, n_sel=n_sel, win_keys=win_keys),
        grid=(b, t // Q_BLOCK),
        in_specs=[
            pl.BlockSpec((1, Q_BLOCK, NSA_Q), lambda i, j: (i, j, qp_blk)),
            pl.BlockSpec((1, Q_BLOCK, NSA_Q), lambda i, j: (i, j, 0)),
            pl.BlockSpec((1, Q_BLOCK, LANES), lambda i, j: (i, j, gate_blk)),
            pl.BlockSpec((1, n_cmp_pad, LANES), lambda i, j: (i, 0, 0)),
            pl.BlockSpec((1, n_cmp_pad, LANES), lambda i, j: (i, 0, 0)),
            full_t(ks_blk), full_t(vs_blk), full_t(kw_blk), full_t(vw_blk),
            pl.BlockSpec((n_cmp_pad, LANES), lambda i, j: (0, 0)),
            pl.BlockSpec((t, LANES), lambda i, j: (0, 0)),
        ],
        out_specs=pl.BlockSpec((1, Q_BLOCK, NSA_Q), lambda i, j: (i, j, 0)),
        out_shape=jax.ShapeDtypeStruct((b, t, NSA_Q), BF16),
        scratch_shapes=[pltpu.VMEM((LANES, Q_BLOCK), F32)],
        compiler_params=_params(("parallel", "arbitrary")),
        name="nsa",
    )(pa3, r3, pa3, kc, vc, r3, v3, r3, v3, cmap, expand_t)


def _slc_map(n_cmp_pad, n_slc):
    i = np.arange(n_cmp_pad)[:, None]
    j = np.arange(LANES)[None, :]
    lo = np.maximum(i * CMP_STRIDE, j * SLC_BLOCK)
    hi = np.minimum(i * CMP_STRIDE + CMP_BLOCK, (j + 1) * SLC_BLOCK)
    m = (np.maximum(hi - lo, 0) / CMP_BLOCK).astype(np.float32)
    m[:, n_slc:] = 0.0
    m[n_cmp_pad - 1:, :] = 0.0
    return m


def _dil_kernel(q_ref, kp_ref, kc_ref, vp_ref, vc_ref, o_ref, lse_ref, *, span):
    ub = pl.program_id(2)
    nq = DIL_BLOCK
    lane = lax.broadcasted_iota(jnp.int32, (nq, LANES), 1)
    row = lax.broadcasted_iota(jnp.int32, (2 * nq, 2 * nq), 0) & (nq - 1)
    col = lax.broadcasted_iota(jnp.int32, (2 * nq, 2 * nq), 1)
    diff = row + nq - col
    mask = (diff >= 0) & (diff <= span) & ((col >= nq) | (ub > 0))
    for hp in range(DIL_HEADS // 2):
        cols = slice(hp * LANES, (hp + 1) * LANES)
        qt = q_ref[0, :, cols] * jnp.asarray(SCALE, BF16)
        zero = jnp.zeros_like(qt)
        qs = jnp.concatenate([jnp.where(lane < HEAD_DIM, qt, zero), jnp.where(lane >= HEAD_DIM, qt, zero)], axis=0)
        k = jnp.concatenate([kp_ref[0, :, cols], kc_ref[0, :, cols]], axis=0)
        v = jnp.concatenate([vp_ref[0, :, cols], vc_ref[0, :, cols]], axis=0)
        s = _dot_nt(qs, k)
        m = jnp.max(jnp.where(mask, s, NEG), axis=-1, keepdims=True)
        e = jnp.exp(jnp.where(mask, s - m, NEG))
        d = jnp.sum(e, axis=-1, keepdims=True)
        p = e / jnp.where(d > 0, d, 1.0)
        o = _dot(p.astype(BF16), v)
        lse = m + jnp.log(d)
        o_ref[0, :, cols] = jnp.where(lane < HEAD_DIM, o[:nq], o[nq:]).astype(o_ref.dtype)
        lse_ref[0, :, cols] = jnp.where(lane < HEAD_DIM, lse[:nq], lse[nq:])


def _dilated(r3, v3, gidx, dil, span, q_blk0, k_blk0, v_blk0):
    b, t, cr = r3.shape
    cv = v3.shape[-1]
    ln = t // dil
    rr = r3.reshape(b, ln, dil * cr)
    vv = v3.reshape(b, ln, dil * cv)
    nr, nv = cr // DIL_W, cv // DIL_W
    blk = (1, DIL_BLOCK, DIL_W)
    cur = lambda n, off: pl.BlockSpec(blk, lambda i, r, u: (i, u, r * n + off))
    prev = lambda n, off: pl.BlockSpec(blk, lambda i, r, u: (i, jnp.maximum(u - 1, 0), r * n + off))
    out_spec = pl.BlockSpec(blk, lambda i, r, u: (i, u, r))
    o, lse = pl.pallas_call(
        functools.partial(_dil_kernel, span=span),
        grid=(b, dil, ln // DIL_BLOCK),
        in_specs=[cur(nr, q_blk0 + gidx), prev(nr, k_blk0 + gidx), cur(nr, k_blk0 + gidx),
                  prev(nv, v_blk0 + gidx), cur(nv, v_blk0 + gidx)],
        out_specs=[out_spec, out_spec],
        out_shape=[jax.ShapeDtypeStruct((b, ln, dil * DIL_W), F32), jax.ShapeDtypeStruct((b, ln, dil * DIL_W), F32)],
        compiler_params=_params(("parallel", "parallel", "arbitrary")),
        name=f"dilated_{dil}",
    )(rr, rr, rr, vv, vv)
    return o.reshape(b * t, DIL_W), lse.reshape(b * t, DIL_W)


def _out_kernel(x_ref, yn_ref, o1_ref, o2_ref, o3_ref, l1_ref, l2_ref, l3_ref, g0_ref, g1_ref,
                wn_ref, wd_ref, wo_ref, o_ref):
    l1, l2, l3 = l1_ref[...], l2_ref[...], l3_ref[...]
    m = jnp.maximum(jnp.maximum(l1, l2), l3)
    a1, a2, a3 = jnp.exp(l1 - m), jnp.exp(l2 - m), jnp.exp(l3 - m)
    den = a1 + a2 + a3
    y_dil = (a1 / den) * o1_ref[...] + (a2 / den) * o2_ref[...] + (a3 / den) * o3_ref[...]
    merged = (jax.nn.sigmoid(g0_ref[...]) * _dot(yn_ref[...], wn_ref[...])
              + jax.nn.sigmoid(g1_ref[...]) * _dot(y_dil.astype(BF16), wd_ref[...]))
    o_ref[...] = x_ref[...] + _dot(merged.astype(BF16), wo_ref[...])


def _out_proj(x2, y_nsa, outs, lses, pa, w_bn, w_bd, w_o, tm=512):
    m, d = x2.shape
    row = lambda w: pl.BlockSpec((tm, w), lambda i: (i, 0))
    whole = lambda a: pl.BlockSpec(a.shape, lambda i: (0, 0))
    return pl.pallas_call(
        _out_kernel,
        grid=(m // tm,),
        in_specs=[row(d), row(NSA_Q)] + [row(DIL_W)] * 6
        + [pl.BlockSpec((tm, d), lambda i: (i, 0)), pl.BlockSpec((tm, d), lambda i: (i, 1))]
        + [whole(w_bn), whole(w_bd), whole(w_o)],
        out_specs=row(d),
        out_shape=jax.ShapeDtypeStruct((m, d), F32),
        compiler_params=_params(("parallel",)),
        name="out_proj",
    )(x2, y_nsa, *outs, *lses, pa, pa, w_bn, w_bd, w_o)


def _mixer(x2, pos_col, inv_lane, b, t, norm_mix, w_in, cmp_pos, cmp_w1, cmp_w2, w_branch_nsa, w_branch_dil, w_out):
    d = x2.shape[1]
    dh = HEAD_DIM
    o_q, o_kv, o_gn = 0, NSA_Q, NSA_Q + 6 * NSA_KV
    o_d = o_gn + 3 * NSA_HEADS
    o_gm = o_d + 3 * DIL_GROUPS * DIL_W
    kv = lambda i: w_in[:, o_kv + i * NSA_KV:o_kv + (i + 1) * NSA_KV]
    dd = lambda i: w_in[:, o_d + i * DIL_GROUPS * DIL_W:o_d + (i + 1) * DIL_GROUPS * DIL_W]
    w_q = w_in[:, o_q:o_q + NSA_Q]
    gn_pad = jnp.zeros((d, 2 * LANES - 3 * NSA_HEADS), w_in.dtype)
    w_a = jnp.concatenate([w_in[:, o_gm:o_gm + 2 * d], w_q, kv(0), kv(1), w_in[:, o_gn:o_gn + 3 * NSA_HEADS], gn_pad],
                          axis=1).astype(BF16)
    w_b = jnp.concatenate([w_q, kv(2), kv(4), dd(0), dd(1)], axis=1).astype(BF16)
    w_c = jnp.concatenate([kv(3), kv(5), dd(2)], axis=1).astype(BF16)

    pa = _proj(x2, norm_mix, w_a, F32)
    rb = _proj(x2, norm_mix, w_b, BF16, pos=pos_col, inv_lane=inv_lane)
    vc_ = _proj(x2, norm_mix, w_c, BF16)

    n_seg = t // CMP_STRIDE
    src0 = 2 * d + NSA_Q
    seg = pa[:, src0:src0 + 2 * NSA_KV].reshape(b, t, 2, NSA_KV_GROUPS, dh)
    seg = seg.transpose(2, 0, 3, 1, 4).reshape(2, b, NSA_KV_GROUPS, n_seg, CMP_STRIDE * dh)
    pos = cmp_pos.reshape(2, 2, 1, CMP_STRIDE * dh)
    w1 = cmp_w1.reshape(2, 2, CMP_STRIDE * dh, -1).astype(BF16)
    cmp = _compress(seg, pos, w1, cmp_w2.astype(BF16))
    cmp = cmp.transpose(0, 1, 3, 2, 4).reshape(2, b, n_seg, NSA_KV).astype(BF16)

    n_slc = t // SLC_BLOCK
    cmap = jnp.asarray(_slc_map(n_seg, n_slc))
    expand_t = jnp.asarray((np.arange(t)[:, None] // SLC_BLOCK == np.arange(LANES)[None, :]).astype(np.float32), BF16)
    pa3 = pa.reshape(b, t, -1)
    r3 = rb.reshape(b, t, -1)
    v3 = vc_.reshape(b, t, -1)
    y_nsa = _nsa(pa3, r3, v3, cmp[0], cmp[1], cmap, expand_t,
                 qp_blk=2 * d // NSA_Q, gate_blk=(2 * d + NSA_Q + 2 * NSA_KV) // LANES,
                 ks_blk=NSA_Q // LANES, kw_blk=NSA_Q // LANES + 1, vs_blk=0, vw_blk=1)

    outs, lses = [], []
    q_blk0 = (NSA_Q + 2 * NSA_KV) // DIL_W
    for gidx, (window, dil) in enumerate(DIL_PAIRS):
        o, lse = _dilated(r3, v3, gidx, dil, window // dil, q_blk0, q_blk0 + DIL_GROUPS, (2 * NSA_KV) // DIL_W)
        outs.append(o)
        lses.append(lse)

    return _out_proj(x2, y_nsa.reshape(b * t, NSA_Q), outs, lses, pa,
                     w_branch_nsa.astype(BF16), w_branch_dil.astype(BF16), w_out.astype(BF16))


def kernel(x, positions, norm_ffn1, ffn1_w_in, ffn1_w_out, norm_mix, w_in, cmp_pos, cmp_w1, cmp_w2,
           w_branch_nsa, w_branch_dil, w_out, norm_ffn2, ffn2_w_in, ffn2_w_out, norm_final):
    b, t, d = x.shape
    depth = norm_ffn1.shape[0]
    x2 = x.reshape(b * t, d)
    pos_col = positions.reshape(b * t, 1).astype(jnp.int32)
    inv = jnp.power(ROPE_THETA, -jnp.arange(0, ROPE_DIM, 2, dtype=F32) / ROPE_DIM)
    inv_head = jnp.concatenate([inv, inv, jnp.zeros((HEAD_DIM - ROPE_DIM,), F32)])
    inv_lane = jnp.tile(inv_head, LANES // HEAD_DIM).reshape(1, LANES)
    for l in range(depth):
        last = l == depth - 1
        x2 = _ffn(x2, norm_ffn1[l], ffn1_w_in[l].astype(BF16), ffn1_w_out[l].astype(BF16))
        x2 = _mixer(x2, pos_col, inv_lane, b, t, norm_mix[l], w_in[l], cmp_pos[l], cmp_w1[l], cmp_w2[l],
                    w_branch_nsa[l], w_branch_dil[l], w_out[l])
        x2 = _ffn(x2, norm_ffn2[l], ffn2_w_in[l].astype(BF16), ffn2_w_out[l].astype(BF16),
                  final_gain=norm_final if last else None)
    if depth == 0:
        raise ValueError("depth must be positive")
    return x2.reshape(b, t, d)
```

```python
import functools

import numpy as np
import jax
import jax.numpy as jnp
from jax import lax
from jax.experimental import pallas as pl
from jax.experimental.pallas import tpu as pltpu

HEAD_DIM = 64
ROPE_DIM = HEAD_DIM // 4
ROPE_THETA = 500000.0
NORM_EPS = 1e-6

NSA_HEADS = 8
NSA_KV_GROUPS = 2
NSA_HPG = NSA_HEADS // NSA_KV_GROUPS
CMP_BLOCK = 32
CMP_STRIDE = 16
SLC_BLOCK = 64
SLC_TOPN = 16
WIN = 512
Q_BLOCK = 128
FORCE = 1e9
NEG = -1e30

DIL_PAIRS = ((128, 1), (512, 4), (2048, 16))
DIL_GROUPS = len(DIL_PAIRS)
DIL_HEADS = 4
DIL_BLOCK = 128

NSA_Q = NSA_HEADS * HEAD_DIM
NSA_KV = NSA_KV_GROUPS * HEAD_DIM
DIL_W = DIL_HEADS * HEAD_DIM

LANES = 128
SEL_KEY_TILE = 512
VMEM_LIMIT = 56 * 1024 * 1024

F32 = jnp.float32
BF16 = jnp.bfloat16
SCALE = HEAD_DIM ** -0.5


def _dot(a, b):
    return jnp.dot(a, b, preferred_element_type=F32)


def _dot_nt(a, b):
    return lax.dot_general(a, b, (((1,), (1,)), ((), ())), preferred_element_type=F32)


def _rmsnorm(x, g):
    return x * lax.rsqrt(jnp.mean(x * x, axis=-1, keepdims=True) + NORM_EPS) * g


def _params(sem):
    return pltpu.CompilerParams(dimension_semantics=sem, vmem_limit_bytes=VMEM_LIMIT)


def _ffn_kernel(*refs, final):
    if final:
        x_ref, g_ref, wu_ref, wv_ref, wo_ref, gf_ref, o_ref, h_scr, acc_scr = refs
    else:
        x_ref, g_ref, wu_ref, wv_ref, wo_ref, o_ref, h_scr, acc_scr = refs
    j = pl.program_id(1)

    @pl.when(j == 0)
    def _():
        h_scr[...] = _rmsnorm(x_ref[...], g_ref[...]).astype(BF16)
        acc_scr[...] = jnp.zeros_like(acc_scr)

    h = h_scr[...]
    u = _dot(h, wu_ref[...])
    v = _dot(h, wv_ref[...])
    a = (u * jax.nn.sigmoid(u)) * v
    acc_scr[...] += _dot(a.astype(BF16), wo_ref[...])

    @pl.when(j == pl.num_programs(1) - 1)
    def _():
        y = x_ref[...] + 0.5 * acc_scr[...]
        if final:
            y = _rmsnorm(y, gf_ref[...])
        o_ref[...] = y


def _ffn(x2, gain, w_in, w_out, final_gain=None, tm=1024, tf=256):
    m, d = x2.shape
    dff = w_out.shape[0]
    nf = dff // tf
    final = final_gain is not None
    in_specs = [
        pl.BlockSpec((tm, d), lambda i, j: (i, 0)),
        pl.BlockSpec((1, d), lambda i, j: (0, 0)),
        pl.BlockSpec((d, tf), lambda i, j: (0, j)),
        pl.BlockSpec((d, tf), lambda i, j: (0, j + nf)),
        pl.BlockSpec((tf, d), lambda i, j: (j, 0)),
    ]
    args = [x2, gain.reshape(1, d), w_in, w_in, w_out]
    if final:
        in_specs.append(pl.BlockSpec((1, d), lambda i, j: (0, 0)))
        args.append(final_gain.reshape(1, d))
    return pl.pallas_call(
        functools.partial(_ffn_kernel, final=final),
        grid=(m // tm, nf),
        in_specs=in_specs,
        out_specs=pl.BlockSpec((tm, d), lambda i, j: (i, 0)),
        out_shape=jax.ShapeDtypeStruct((m, d), F32),
        scratch_shapes=[pltpu.VMEM((tm, d), BF16), pltpu.VMEM((tm, d), F32)],
        compiler_params=_params(("parallel", "arbitrary")),
        name="ffn_final" if final else "ffn",
    )(*args)


def _rope_coeffs(pos_col, inv_lane):
    ang = pos_col.astype(F32) * inv_lane
    c, s = jnp.cos(ang), jnp.sin(ang)
    d = lax.broadcasted_iota(jnp.int32, ang.shape, 1) & (HEAD_DIM - 1)
    half = ROPE_DIM // 2
    coef_c = jnp.where(d < ROPE_DIM, c, 1.0)
    coef_hi = jnp.where(d < half, -s, 0.0)
    coef_lo = jnp.where((d >= half) & (d < ROPE_DIM), s, 0.0)
    return coef_c, coef_hi, coef_lo


def _proj_kernel(*refs, rope):
    if rope:
        x_ref, g_ref, w_ref, pos_ref, inv_ref, o_ref = refs
    else:
        x_ref, g_ref, w_ref, o_ref = refs
    h = _rmsnorm(x_ref[...], g_ref[...]).astype(BF16)
    y = _dot(h, w_ref[...])
    if not rope:
        o_ref[...] = y.astype(o_ref.dtype)
        return
    coef_c, coef_hi, coef_lo = _rope_coeffs(pos_ref[...], inv_ref[...])
    half = ROPE_DIM // 2
    for t in range(y.shape[1] // LANES):
        yt = y[:, t * LANES:(t + 1) * LANES]
        hi = pltpu.roll(yt, LANES - half, axis=1)
        lo = pltpu.roll(yt, half, axis=1)
        o_ref[:, t * LANES:(t + 1) * LANES] = (yt * coef_c + hi * coef_hi + lo * coef_lo).astype(o_ref.dtype)


def _proj(x2, gain, w, out_dtype, pos=None, inv_lane=None, tm=512):
    m, d = x2.shape
    n = w.shape[1]
    rope = pos is not None
    in_specs = [
        pl.BlockSpec((tm, d), lambda i: (i, 0)),
        pl.BlockSpec((1, d), lambda i: (0, 0)),
        pl.BlockSpec((d, n), lambda i: (0, 0)),
    ]
    args = [x2, gain.reshape(1, d), w]
    if rope:
        in_specs += [pl.BlockSpec((tm, 1), lambda i: (i, 0)), pl.BlockSpec((1, LANES), lambda i: (0, 0))]
        args += [pos, inv_lane]
    return pl.pallas_call(
        functools.partial(_proj_kernel, rope=rope),
        grid=(m // tm,),
        in_specs=in_specs,
        out_specs=pl.BlockSpec((tm, n), lambda i: (i, 0)),
        out_shape=jax.ShapeDtypeStruct((m, n), out_dtype),
        compiler_params=_params(("parallel",)),
        name="proj_rope" if rope else ("proj_f32" if out_dtype == F32 else "proj_bf16"),
    )(*args)


def _compress_kernel(seg_ref, pos_ref, w1_ref, w2_ref, o_ref):
    seg = seg_ref[0, 0, 0]
    a = _dot((seg + pos_ref[0, 0]).astype(BF16), w1_ref[0, 0])
    b = _dot((seg + pos_ref[0, 1]).astype(BF16), w1_ref[0, 1])
    n_seg = seg.shape[0]
    pre = a + pltpu.roll(b, n_seg - 1, axis=0)
    hid = pre * jax.nn.sigmoid(pre)
    o_ref[0, 0, 0] = _dot(hid.astype(BF16), w2_ref[0])


def _compress(seg, pos, w1, w2):
    _, b, g, n_seg, k = seg.shape
    hid = w1.shape[-1]
    dh = w2.shape[-1]
    return pl.pallas_call(
        _compress_kernel,
        grid=(2, b, g),
        in_specs=[
            pl.BlockSpec((1, 1, 1, n_seg, k), lambda c, i, j: (c, i, j, 0, 0)),
            pl.BlockSpec((1, 2, 1, k), lambda c, i, j: (c, 0, 0, 0)),
            pl.BlockSpec((1, 2, k, hid), lambda c, i, j: (c, 0, 0, 0)),
            pl.BlockSpec((1, hid, dh), lambda c, i, j: (c, 0, 0)),
        ],
        out_specs=pl.BlockSpec((1, 1, 1, n_seg, dh), lambda c, i, j: (c, i, j, 0, 0)),
        out_shape=jax.ShapeDtypeStruct((2, b, g, n_seg, dh), F32),
        compiler_params=_params(("parallel", "parallel", "parallel")),
        name="compress",
    )(seg, pos, w1, w2)


MASK_BIAS = NEG
M_INIT = -1e29


def _stack_heads_t(q, g):
    zeros = jnp.zeros((HEAD_DIM, q.shape[0]), F32)
    parts = []
    for h in range(NSA_HPG):
        hg = g * NSA_HPG + h
        tile_t = (q[:, (hg // 2) * LANES:(hg // 2 + 1) * LANES] * SCALE).T
        feat = tile_t[(hg % 2) * HEAD_DIM:(hg % 2 + 1) * HEAD_DIM]
        parts.append(jnp.concatenate([feat, zeros] if g == 0 else [zeros, feat], axis=0))
    return jnp.concatenate(parts, axis=1).astype(BF16)


def _softmax_t(s, bias, hg):
    sb = s + jnp.concatenate([bias] * hg, axis=1)
    m = jnp.maximum(jnp.max(sb, axis=0, keepdims=True), M_INIT)
    e = jnp.exp(sb - m)
    d = jnp.sum(e, axis=0, keepdims=True)
    return e, 1.0 / jnp.where(d > 0, d, 1.0)


def _nsa_kernel(qp_ref, qr_ref, gate_ref, kc_ref, vct_ref, ks_ref, vst_ref, kw_ref, vwt_ref, cmapt_ref,
                o_ref, cnt_scr, bias_scr, *, n_sel, win_keys):
    qb = pl.program_id(1)
    t0 = qb * Q_BLOCK
    nq = Q_BLOCK
    hg = NSA_HPG
    tk = SEL_KEY_TILE
    n_cmp_pad = kc_ref.shape[1]
    t_row = t0 + lax.broadcasted_iota(jnp.int32, (1, nq), 1)
    blk = lax.broadcasted_iota(jnp.int32, (LANES, nq), 0)
    sub8 = lax.broadcasted_iota(jnp.int32, (8, nq), 0)

    q_plain = qp_ref[0]
    q_rope = qr_ref[0].astype(F32)
    gates_t = jax.nn.sigmoid(gate_ref[0]).T

    cmp_end = lax.broadcasted_iota(jnp.int32, (n_cmp_pad, nq), 0) * CMP_STRIDE + (CMP_BLOCK - 1)
    bias_c = jnp.where(cmp_end <= t_row, 0.0, MASK_BIAS)
    cur = t_row >> 6
    forced = (blk == 0) | (blk == cur) | (blk == cur - 1)
    valid = blk <= cur
    n_blocks = (qb + 1) * (Q_BLOCK // SLC_BLOCK)

    win_start = pl.multiple_of(jnp.maximum(t0 - WIN, 0), Q_BLOCK)
    diff_w = t_row - (win_start + lax.broadcasted_iota(jnp.int32, (win_keys, nq), 0))
    bias_w = jnp.where((diff_w >= 0) & (diff_w < WIN), 0.0, MASK_BIAS)

    n_tiles = (t0 + nq + tk - 1) // tk

    for g in range(NSA_KV_GROUPS):
        qt_plain = _stack_heads_t(q_plain, g)
        qt_rope = _stack_heads_t(q_rope, g)
        feat = slice(g * HEAD_DIM, (g + 1) * HEAD_DIM)

        e_c, r_c = _softmax_t(_dot(kc_ref[0], qt_plain), bias_c, hg)
        p_c = e_c * r_c
        o_c = _dot(vct_ref[0], p_c.astype(BF16))
        p_sum = p_c[:, 0:nq]
        for h in range(1, hg):
            p_sum = p_sum + p_c[:, h * nq:(h + 1) * nq]
        imp = jnp.dot(cmapt_ref[...], p_sum, preferred_element_type=F32, precision=lax.Precision.HIGHEST)
        imp = jnp.where(forced, FORCE, jnp.where(valid, imp, -FORCE))

        cnt_scr[...] = jnp.zeros_like(cnt_scr)
        for ig in range(LANES // 8):
            @pl.when(ig * 8 < n_blocks)
            def _(ig=ig):
                parts = []
                for jv in range(LANES // 8):
                    x = imp[jv * 8:(jv + 1) * 8]
                    c = jnp.zeros((8, nq), jnp.int32)
                    for r in range(8):
                        row = imp[ig * 8 + r:ig * 8 + r + 1]
                        if jv > ig:
                            ahead = row >= x
                        elif jv < ig:
                            ahead = row > x
                        else:
                            ahead = (row > x) | ((row == x) & (sub8 > r))
                        c = c + ahead.astype(jnp.int32)
                    parts.append(c)
                cnt_scr[...] += jnp.concatenate(parts, axis=0)
        picked = (cnt_scr[...] < n_sel) & (imp > -0.5 * FORCE)
        bias_scr[...] = jnp.where(picked, 0.0, MASK_BIAS)

        def sel_tile(kt, carry, causal):
            m, l, acc = carry
            k0 = pl.multiple_of(kt * tk, tk)
            b0 = kt * (tk // SLC_BLOCK)
            rows = [jnp.broadcast_to(bias_scr[pl.ds(b0 + i, 1), :], (SLC_BLOCK, nq)) for i in range(tk // SLC_BLOCK)]
            bias = jnp.concatenate(rows, axis=0)
            if causal:
                kpos = k0 + lax.broadcasted_iota(jnp.int32, (tk, nq), 0)
                bias = bias + jnp.where(kpos <= t_row, 0.0, MASK_BIAS)
            s = _dot(ks_ref[0, pl.ds(k0, tk), :], qt_rope) + jnp.concatenate([bias] * hg, axis=1)
            m_new = jnp.maximum(m, jnp.max(s, axis=0, keepdims=True))
            alpha = jnp.exp(m - m_new)
            e = jnp.exp(s - m_new)
            l_new = alpha * l + jnp.sum(e, axis=0, keepdims=True)
            acc_new = alpha * acc + _dot(vst_ref[0, :, pl.ds(k0, tk)], e.astype(BF16))
            return m_new, l_new, acc_new

        init = (jnp.full((1, hg * nq), M_INIT, F32), jnp.zeros((1, hg * nq), F32), jnp.zeros((LANES, hg * nq), F32))
        carry = lax.fori_loop(0, n_tiles - 1, functools.partial(sel_tile, causal=False), init)
        _, l_s, acc_s = sel_tile(n_tiles - 1, carry, True)
        o_s = acc_s * (1.0 / jnp.where(l_s > 0, l_s, 1.0))

        e_w, r_w = _softmax_t(_dot(kw_ref[0, pl.ds(win_start, win_keys), :], qt_rope), bias_w, hg)
        o_w = _dot(vwt_ref[0, :, pl.ds(win_start, win_keys)], e_w.astype(BF16)) * r_w

        mixes = []
        for h in range(hg):
            row = g * hg * 3 + h * 3
            cols = slice(h * nq, (h + 1) * nq)
            mixes.append(gates_t[row:row + 1] * o_c[feat, cols] + gates_t[row + 1:row + 2] * o_s[feat, cols]
                         + gates_t[row + 2:row + 3] * o_w[feat, cols])
        for hp in range(hg // 2):
            c = (g * hg) // 2 + hp
            pair = jnp.concatenate([mixes[2 * hp], mixes[2 * hp + 1]], axis=0)
            o_ref[0, :, c * LANES:(c + 1) * LANES] = pair.T.astype(o_ref.dtype)


def _nsa(pa3, r3, kc, vct, vst, vwt, cmapt, *, qp_blk, gate_blk, ks_blk, kw_blk):
    b, t, _ = pa3.shape
    n_cmp_pad = kc.shape[1]
    n_sel = min(SLC_TOPN, t // SLC_BLOCK)
    win_keys = min(WIN + Q_BLOCK, t)
    keys = lambda blk: pl.BlockSpec((1, t, LANES), lambda i, j, blk=blk: (i, 0, blk))
    vals_t = pl.BlockSpec((1, LANES, t), lambda i, j: (i, 0, 0))
    return pl.pallas_call(
        functools.partial(_nsa_kernel, n_sel=n_sel, win_keys=win_keys),
        grid=(b, t // Q_BLOCK),
        in_specs=[
            pl.BlockSpec((1, Q_BLOCK, NSA_Q), lambda i, j: (i, j, qp_blk)),
            pl.BlockSpec((1, Q_BLOCK, NSA_Q), lambda i, j: (i, j, 0)),
            pl.BlockSpec((1, Q_BLOCK, LANES), lambda i, j: (i, j, gate_blk)),
            pl.BlockSpec((1, n_cmp_pad, LANES), lambda i, j: (i, 0, 0)),
            pl.BlockSpec((1, LANES, n_cmp_pad), lambda i, j: (i, 0, 0)),
            keys(ks_blk), vals_t, keys(kw_blk), vals_t,
            pl.BlockSpec((LANES, n_cmp_pad), lambda i, j: (0, 0)),
        ],
        out_specs=pl.BlockSpec((1, Q_BLOCK, NSA_Q), lambda i, j: (i, j, 0)),
        out_shape=jax.ShapeDtypeStruct((b, t, NSA_Q), BF16),
        scratch_shapes=[pltpu.VMEM((LANES, Q_BLOCK), jnp.int32), pltpu.VMEM((LANES, Q_BLOCK), F32)],
        compiler_params=_params(("parallel", "arbitrary")),
        name="nsa",
    )(pa3, r3, pa3, kc, vct, r3, vst, r3, vwt, cmapt)


def _slc_map_t(n_cmp_pad, n_slc):
    i = np.arange(n_cmp_pad)[None, :]
    j = np.arange(LANES)[:, None]
    lo = np.maximum(i * CMP_STRIDE, j * SLC_BLOCK)
    hi = np.minimum(i * CMP_STRIDE + CMP_BLOCK, (j + 1) * SLC_BLOCK)
    m = (np.maximum(hi - lo, 0) / CMP_BLOCK).astype(np.float32)
    m[n_slc:, :] = 0.0
    m[:, n_cmp_pad - 1:] = 0.0
    return m


def _dil_kernel(q_ref, kp_ref, kc_ref, vp_ref, vc_ref, o_ref, lse_ref, *, span):
    ub = pl.program_id(2)
    nq = DIL_BLOCK
    lane = lax.broadcasted_iota(jnp.int32, (nq, LANES), 1)
    row = lax.broadcasted_iota(jnp.int32, (2 * nq, 2 * nq), 0) & (nq - 1)
    col = lax.broadcasted_iota(jnp.int32, (2 * nq, 2 * nq), 1)
    diff = row + nq - col
    mask = (diff >= 0) & (diff <= span) & ((col >= nq) | (ub > 0))
    for hp in range(DIL_HEADS // 2):
        cols = slice(hp * LANES, (hp + 1) * LANES)
        qt = q_ref[0, :, cols] * jnp.asarray(SCALE, BF16)
        zero = jnp.zeros_like(qt)
        qs = jnp.concatenate([jnp.where(lane < HEAD_DIM, qt, zero), jnp.where(lane >= HEAD_DIM, qt, zero)], axis=0)
        k = jnp.concatenate([kp_ref[0, :, cols], kc_ref[0, :, cols]], axis=0)
        v = jnp.concatenate([vp_ref[0, :, cols], vc_ref[0, :, cols]], axis=0)
        s = _dot_nt(qs, k)
        m = jnp.max(jnp.where(mask, s, NEG), axis=-1, keepdims=True)
        e = jnp.exp(jnp.where(mask, s - m, NEG))
        d = jnp.sum(e, axis=-1, keepdims=True)
        p = e / jnp.where(d > 0, d, 1.0)
        o = _dot(p.astype(BF16), v)
        lse = m + jnp.log(d)
        o_ref[0, :, cols] = jnp.where(lane < HEAD_DIM, o[:nq], o[nq:]).astype(o_ref.dtype)
        lse_ref[0, :, cols] = jnp.where(lane < HEAD_DIM, lse[:nq], lse[nq:])


def _dilated(r3, v3, gidx, dil, span, q_blk0, k_blk0, v_blk0):
    b, t, cr = r3.shape
    cv = v3.shape[-1]
    ln = t // dil
    rr = r3.reshape(b, ln, dil * cr)
    vv = v3.reshape(b, ln, dil * cv)
    nr, nv = cr // DIL_W, cv // DIL_W
    blk = (1, DIL_BLOCK, DIL_W)
    cur = lambda n, off: pl.BlockSpec(blk, lambda i, r, u: (i, u, r * n + off))
    prev = lambda n, off: pl.BlockSpec(blk, lambda i, r, u: (i, jnp.maximum(u - 1, 0), r * n + off))
    out_spec = pl.BlockSpec(blk, lambda i, r, u: (i, u, r))
    o, lse = pl.pallas_call(
        functools.partial(_dil_kernel, span=span),
        grid=(b, dil, ln // DIL_BLOCK),
        in_specs=[cur(nr, q_blk0 + gidx), prev(nr, k_blk0 + gidx), cur(nr, k_blk0 + gidx),
                  prev(nv, v_blk0 + gidx), cur(nv, v_blk0 + gidx)],
        out_specs=[out_spec, out_spec],
        out_shape=[jax.ShapeDtypeStruct((b, ln, dil * DIL_W), F32), jax.ShapeDtypeStruct((b, ln, dil * DIL_W), F32)],
        compiler_params=_params(("parallel", "parallel", "arbitrary")),
        name=f"dilated_{dil}",
    )(rr, rr, rr, vv, vv)
    return o.reshape(b * t, DIL_W), lse.reshape(b * t, DIL_W)


def _out_kernel(x_ref, yn_ref, o1_ref, o2_ref, o3_ref, l1_ref, l2_ref, l3_ref, g0_ref, g1_ref,
                wn_ref, wd_ref, wo_ref, o_ref):
    l1, l2, l3 = l1_ref[...], l2_ref[...], l3_ref[...]
    m = jnp.maximum(jnp.maximum(l1, l2), l3)
    a1, a2, a3 = jnp.exp(l1 - m), jnp.exp(l2 - m), jnp.exp(l3 - m)
    den = a1 + a2 + a3
    y_dil = (a1 / den) * o1_ref[...] + (a2 / den) * o2_ref[...] + (a3 / den) * o3_ref[...]
    merged = (jax.nn.sigmoid(g0_ref[...]) * _dot(yn_ref[...], wn_ref[...])
              + jax.nn.sigmoid(g1_ref[...]) * _dot(y_dil.astype(BF16), wd_ref[...]))
    o_ref[...] = x_ref[...] + _dot(merged.astype(BF16), wo_ref[...])


def _out_proj(x2, y_nsa, outs, lses, pa, w_bn, w_bd, w_o, tm=512):
    m, d = x2.shape
    row = lambda w: pl.BlockSpec((tm, w), lambda i: (i, 0))
    whole = lambda a: pl.BlockSpec(a.shape, lambda i: (0, 0))
    return pl.pallas_call(
        _out_kernel,
        grid=(m // tm,),
        in_specs=[row(d), row(NSA_Q)] + [row(DIL_W)] * 6
        + [pl.BlockSpec((tm, d), lambda i: (i, 0)), pl.BlockSpec((tm, d), lambda i: (i, 1))]
        + [whole(w_bn), whole(w_bd), whole(w_o)],
        out_specs=row(d),
        out_shape=jax.ShapeDtypeStruct((m, d), F32),
        compiler_params=_params(("parallel",)),
        name="out_proj",
    )(x2, y_nsa, *outs, *lses, pa, pa, w_bn, w_bd, w_o)


def _mixer(x2, pos_col, inv_lane, b, t, norm_mix, w_in, cmp_pos, cmp_w1, cmp_w2, w_branch_nsa, w_branch_dil, w_out):
    d = x2.shape[1]
    dh = HEAD_DIM
    o_q, o_kv, o_gn = 0, NSA_Q, NSA_Q + 6 * NSA_KV
    o_d = o_gn + 3 * NSA_HEADS
    o_gm = o_d + 3 * DIL_GROUPS * DIL_W
    kv = lambda i: w_in[:, o_kv + i * NSA_KV:o_kv + (i + 1) * NSA_KV]
    dd = lambda i: w_in[:, o_d + i * DIL_GROUPS * DIL_W:o_d + (i + 1) * DIL_GROUPS * DIL_W]
    w_q = w_in[:, o_q:o_q + NSA_Q]
    gn_pad = jnp.zeros((d, 2 * LANES - 3 * NSA_HEADS), w_in.dtype)
    w_a = jnp.concatenate([w_in[:, o_gm:o_gm + 2 * d], w_q, kv(0), kv(1), w_in[:, o_gn:o_gn + 3 * NSA_HEADS], gn_pad],
                          axis=1).astype(BF16)
    w_b = jnp.concatenate([w_q, kv(2), kv(4), dd(0), dd(1)], axis=1).astype(BF16)
    w_c = jnp.concatenate([kv(3), kv(5), dd(2)], axis=1).astype(BF16)

    pa = _proj(x2, norm_mix, w_a, F32)
    rb = _proj(x2, norm_mix, w_b, BF16, pos=pos_col, inv_lane=inv_lane)
    vc_ = _proj(x2, norm_mix, w_c, BF16)

    n_seg = t // CMP_STRIDE
    src0 = 2 * d + NSA_Q
    seg = pa[:, src0:src0 + 2 * NSA_KV].reshape(b, t, 2, NSA_KV_GROUPS, dh)
    seg = seg.transpose(2, 0, 3, 1, 4).reshape(2, b, NSA_KV_GROUPS, n_seg, CMP_STRIDE * dh)
    pos = cmp_pos.reshape(2, 2, 1, CMP_STRIDE * dh)
    w1 = cmp_w1.reshape(2, 2, CMP_STRIDE * dh, -1).astype(BF16)
    cmp = _compress(seg, pos, w1, cmp_w2.astype(BF16))
    cmp = cmp.transpose(0, 1, 3, 2, 4).reshape(2, b, n_seg, NSA_KV).astype(BF16)

    cmapt = jnp.asarray(_slc_map_t(n_seg, t // SLC_BLOCK))
    pa3 = pa.reshape(b, t, -1)
    r3 = rb.reshape(b, t, -1)
    v3 = vc_.reshape(b, t, -1)
    vst = v3[:, :, 0:NSA_KV].transpose(0, 2, 1)
    vwt = v3[:, :, NSA_KV:2 * NSA_KV].transpose(0, 2, 1)
    y_nsa = _nsa(pa3, r3, cmp[0], cmp[1].transpose(0, 2, 1), vst, vwt, cmapt,
                 qp_blk=2 * d // NSA_Q, gate_blk=(2 * d + NSA_Q + 2 * NSA_KV) // LANES,
                 ks_blk=NSA_Q // LANES, kw_blk=NSA_Q // LANES + 1)

    outs, lses = [], []
    q_blk0 = (NSA_Q + 2 * NSA_KV) // DIL_W
    for gidx, (window, dil) in enumerate(DIL_PAIRS):
        o, lse = _dilated(r3, v3, gidx, dil, window // dil, q_blk0, q_blk0 + DIL_GROUPS, (2 * NSA_KV) // DIL_W)
        outs.append(o)
        lses.append(lse)

    return _out_proj(x2, y_nsa.reshape(b * t, NSA_Q), outs, lses, pa,
                     w_branch_nsa.astype(BF16), w_branch_dil.astype(BF16), w_out.astype(BF16))


def kernel(x, positions, norm_ffn1, ffn1_w_in, ffn1_w_out, norm_mix, w_in, cmp_pos, cmp_w1, cmp_w2,
           w_branch_nsa, w_branch_dil, w_out, norm_ffn2, ffn2_w_in, ffn2_w_out, norm_final):
    b, t, d = x.shape
    depth = norm_ffn1.shape[0]
    x2 = x.reshape(b * t, d)
    pos_col = positions.reshape(b * t, 1).astype(jnp.int32)
    inv = jnp.power(ROPE_THETA, -jnp.arange(0, ROPE_DIM, 2, dtype=F32) / ROPE_DIM)
    inv_head = jnp.concatenate([inv, inv, jnp.zeros((HEAD_DIM - ROPE_DIM,), F32)])
    inv_lane = jnp.tile(inv_head, LANES // HEAD_DIM).reshape(1, LANES)
    for l in range(depth):
        last = l == depth - 1
        x2 = _ffn(x2, norm_ffn1[l], ffn1_w_in[l].astype(BF16), ffn1_w_out[l].astype(BF16))
        x2 = _mixer(x2, pos_col, inv_lane, b, t, norm_mix[l], w_in[l], cmp_pos[l], cmp_w1[l], cmp_w2[l],
                    w_branch_nsa[l], w_branch_dil[l], w_out[l])
        x2 = _ffn(x2, norm_ffn2[l], ffn2_w_in[l].astype(BF16), ffn2_w_out[l].astype(BF16),
                  final_gain=norm_final if last else None)
    if depth == 0:
        raise ValueError("depth must be positive")
    return x2.reshape(b, t, d)
```

```python
import functools

import numpy as np
import jax
import jax.numpy as jnp
from jax import lax
from jax.experimental import pallas as pl
from jax.experimental.pallas import tpu as pltpu

HEAD_DIM = 64
ROPE_DIM = HEAD_DIM // 4
ROPE_THETA = 500000.0
NORM_EPS = 1e-6

NSA_HEADS = 8
NSA_KV_GROUPS = 2
NSA_HPG = NSA_HEADS // NSA_KV_GROUPS
CMP_BLOCK = 32
CMP_STRIDE = 16
SLC_BLOCK = 64
SLC_TOPN = 16
WIN = 512
Q_BLOCK = 128
FORCE = 1e9
NEG = -1e30

DIL_PAIRS = ((128, 1), (512, 4), (2048, 16))
DIL_GROUPS = len(DIL_PAIRS)
DIL_HEADS = 4
DIL_BLOCK = 128

NSA_Q = NSA_HEADS * HEAD_DIM
NSA_KV = NSA_KV_GROUPS * HEAD_DIM
DIL_W = DIL_HEADS * HEAD_DIM

LANES = 128
SEL_KEY_TILE = 512
VMEM_LIMIT = 56 * 1024 * 1024

F32 = jnp.float32
BF16 = jnp.bfloat16
SCALE = HEAD_DIM ** -0.5


def _dot(a, b):
    return jnp.dot(a, b, preferred_element_type=F32)


def _dot_nt(a, b):
    return lax.dot_general(a, b, (((1,), (1,)), ((), ())), preferred_element_type=F32)


def _rmsnorm(x, g):
    return x * lax.rsqrt(jnp.mean(x * x, axis=-1, keepdims=True) + NORM_EPS) * g


def _params(sem):
    return pltpu.CompilerParams(dimension_semantics=sem, vmem_limit_bytes=VMEM_LIMIT)


def _ffn_kernel(*refs, final):
    if final:
        x_ref, g_ref, wu_ref, wv_ref, wo_ref, gf_ref, o_ref, h_scr, acc_scr = refs
    else:
        x_ref, g_ref, wu_ref, wv_ref, wo_ref, o_ref, h_scr, acc_scr = refs
    j = pl.program_id(1)

    @pl.when(j == 0)
    def _():
        h_scr[...] = _rmsnorm(x_ref[...], g_ref[...]).astype(BF16)
        acc_scr[...] = jnp.zeros_like(acc_scr)

    h = h_scr[...]
    u = _dot(h, wu_ref[...])
    v = _dot(h, wv_ref[...])
    a = (u * jax.nn.sigmoid(u)) * v
    acc_scr[...] += _dot(a.astype(BF16), wo_ref[...])

    @pl.when(j == pl.num_programs(1) - 1)
    def _():
        y = x_ref[...] + 0.5 * acc_scr[...]
        if final:
            y = _rmsnorm(y, gf_ref[...])
        o_ref[...] = y


def _ffn(x2, gain, w_in, w_out, final_gain=None, tm=1024, tf=256):
    m, d = x2.shape
    dff = w_out.shape[0]
    nf = dff // tf
    final = final_gain is not None
    in_specs = [
        pl.BlockSpec((tm, d), lambda i, j: (i, 0)),
        pl.BlockSpec((1, d), lambda i, j: (0, 0)),
        pl.BlockSpec((d, tf), lambda i, j: (0, j)),
        pl.BlockSpec((d, tf), lambda i, j: (0, j + nf)),
        pl.BlockSpec((tf, d), lambda i, j: (j, 0)),
    ]
    args = [x2, gain.reshape(1, d), w_in, w_in, w_out]
    if final:
        in_specs.append(pl.BlockSpec((1, d), lambda i, j: (0, 0)))
        args.append(final_gain.reshape(1, d))
    return pl.pallas_call(
        functools.partial(_ffn_kernel, final=final),
        grid=(m // tm, nf),
        in_specs=in_specs,
        out_specs=pl.BlockSpec((tm, d), lambda i, j: (i, 0)),
        out_shape=jax.ShapeDtypeStruct((m, d), F32),
        scratch_shapes=[pltpu.VMEM((tm, d), BF16), pltpu.VMEM((tm, d), F32)],
        compiler_params=_params(("parallel", "arbitrary")),
        name="ffn_final" if final else "ffn",
    )(*args)


def _rope_coeffs(pos_col, inv_lane):
    ang = pos_col.astype(F32) * inv_lane
    c, s = jnp.cos(ang), jnp.sin(ang)
    d = lax.broadcasted_iota(jnp.int32, ang.shape, 1) & (HEAD_DIM - 1)
    half = ROPE_DIM // 2
    coef_c = jnp.where(d < ROPE_DIM, c, 1.0)
    coef_hi = jnp.where(d < half, -s, 0.0)
    coef_lo = jnp.where((d >= half) & (d < ROPE_DIM), s, 0.0)
    return coef_c, coef_hi, coef_lo


def _proj_kernel(*refs, rope):
    if rope:
        x_ref, g_ref, w_ref, pos_ref, inv_ref, o_ref = refs
    else:
        x_ref, g_ref, w_ref, o_ref = refs
    h = _rmsnorm(x_ref[...], g_ref[...]).astype(BF16)
    y = _dot(h, w_ref[...])
    if not rope:
        o_ref[...] = y.astype(o_ref.dtype)
        return
    coef_c, coef_hi, coef_lo = _rope_coeffs(pos_ref[...], inv_ref[...])
    half = ROPE_DIM // 2
    for t in range(y.shape[1] // LANES):
        yt = y[:, t * LANES:(t + 1) * LANES]
        hi = pltpu.roll(yt, LANES - half, axis=1)
        lo = pltpu.roll(yt, half, axis=1)
        o_ref[:, t * LANES:(t + 1) * LANES] = (yt * coef_c + hi * coef_hi + lo * coef_lo).astype(o_ref.dtype)


def _proj(x2, gain, w, out_dtype, pos=None, inv_lane=None, tm=512):
    m, d = x2.shape
    n = w.shape[1]
    rope = pos is not None
    in_specs = [
        pl.BlockSpec((tm, d), lambda i: (i, 0)),
        pl.BlockSpec((1, d), lambda i: (0, 0)),
        pl.BlockSpec((d, n), lambda i: (0, 0)),
    ]
    args = [x2, gain.reshape(1, d), w]
    if rope:
        in_specs += [pl.BlockSpec((tm, 1), lambda i: (i, 0)), pl.BlockSpec((1, LANES), lambda i: (0, 0))]
        args += [pos, inv_lane]
    return pl.pallas_call(
        functools.partial(_proj_kernel, rope=rope),
        grid=(m // tm,),
        in_specs=in_specs,
        out_specs=pl.BlockSpec((tm, n), lambda i: (i, 0)),
        out_shape=jax.ShapeDtypeStruct((m, n), out_dtype),
        compiler_params=_params(("parallel",)),
        name="proj_rope" if rope else ("proj_f32" if out_dtype == F32 else "proj_bf16"),
    )(*args)


def _compress_kernel(seg_ref, pos_ref, w1_ref, w2_ref, o_ref):
    seg = seg_ref[0, 0, 0]
    a = _dot((seg + pos_ref[0, 0]).astype(BF16), w1_ref[0, 0])
    b = _dot((seg + pos_ref[0, 1]).astype(BF16), w1_ref[0, 1])
    n_seg = seg.shape[0]
    pre = a + pltpu.roll(b, n_seg - 1, axis=0)
    hid = pre * jax.nn.sigmoid(pre)
    o_ref[0, 0, 0] = _dot(hid.astype(BF16), w2_ref[0])


def _compress(seg, pos, w1, w2):
    _, b, g, n_seg, k = seg.shape
    hid = w1.shape[-1]
    dh = w2.shape[-1]
    return pl.pallas_call(
        _compress_kernel,
        grid=(2, b, g),
        in_specs=[
            pl.BlockSpec((1, 1, 1, n_seg, k), lambda c, i, j: (c, i, j, 0, 0)),
            pl.BlockSpec((1, 2, 1, k), lambda c, i, j: (c, 0, 0, 0)),
            pl.BlockSpec((1, 2, k, hid), lambda c, i, j: (c, 0, 0, 0)),
            pl.BlockSpec((1, hid, dh), lambda c, i, j: (c, 0, 0)),
        ],
        out_specs=pl.BlockSpec((1, 1, 1, n_seg, dh), lambda c, i, j: (c, i, j, 0, 0)),
        out_shape=jax.ShapeDtypeStruct((2, b, g, n_seg, dh), F32),
        compiler_params=_params(("parallel", "parallel", "parallel")),
        name="compress",
    )(seg, pos, w1, w2)


MASK_BIAS = NEG
M_INIT = -1e29


def _stack_heads_t(q, g):
    zeros = jnp.zeros((HEAD_DIM, q.shape[0]), F32)
    parts = []
    for h in range(NSA_HPG):
        hg = g * NSA_HPG + h
        tile_t = (q[:, (hg // 2) * LANES:(hg // 2 + 1) * LANES] * SCALE).T
        feat = tile_t[(hg % 2) * HEAD_DIM:(hg % 2 + 1) * HEAD_DIM]
        parts.append(jnp.concatenate([feat, zeros] if g == 0 else [zeros, feat], axis=0))
    return jnp.concatenate(parts, axis=1).astype(BF16)


def _softmax_t(s, bias, hg):
    sb = s + jnp.concatenate([bias] * hg, axis=1)
    m = jnp.maximum(jnp.max(sb, axis=0, keepdims=True), M_INIT)
    e = jnp.exp(sb - m)
    d = jnp.sum(e, axis=0, keepdims=True)
    return e, 1.0 / jnp.where(d > 0, d, 1.0)


def _nsa_kernel(qp_ref, qr_ref, gate_ref, kc_ref, vct_ref, ks_ref, vst_ref, kw_ref, vwt_ref, cmapt_ref,
                o_ref, cnt_scr, bias_scr, sa_scr, sb_scr, *, n_sel, win_keys):
    qb = pl.program_id(1)
    t0 = qb * Q_BLOCK
    nq = Q_BLOCK
    hg = NSA_HPG
    tk = SEL_KEY_TILE
    n_cmp_pad = kc_ref.shape[1]
    t_row = t0 + lax.broadcasted_iota(jnp.int32, (1, nq), 1)
    blk = lax.broadcasted_iota(jnp.int32, (LANES, nq), 0)
    sub8 = lax.broadcasted_iota(jnp.int32, (8, nq), 0)

    q_plain = qp_ref[0]
    q_rope = qr_ref[0].astype(F32)
    gates_t = jax.nn.sigmoid(gate_ref[0]).T

    cmp_end = lax.broadcasted_iota(jnp.int32, (n_cmp_pad, nq), 0) * CMP_STRIDE + (CMP_BLOCK - 1)
    bias_c = jnp.where(cmp_end <= t_row, 0.0, MASK_BIAS)
    cur = t_row >> 6
    forced = (blk == 0) | (blk == cur) | (blk == cur - 1)
    valid = blk <= cur
    n_blocks = (qb + 1) * (Q_BLOCK // SLC_BLOCK)

    win_start = pl.multiple_of(jnp.maximum(t0 - WIN, 0), Q_BLOCK)
    diff_w = t_row - (win_start + lax.broadcasted_iota(jnp.int32, (win_keys, nq), 0))
    bias_w = jnp.where((diff_w >= 0) & (diff_w < WIN), 0.0, MASK_BIAS)

    first_blk = qb * (Q_BLOCK // SLC_BLOCK)
    n_sweep = (t0 + tk - 1) // tk
    n_key_tiles = ks_ref.shape[1] // tk
    bias_own = jnp.where(lax.broadcasted_iota(jnp.int32, (nq, nq), 0) <= lax.broadcasted_iota(jnp.int32, (nq, nq), 1),
                         0.0, MASK_BIAS)

    for g in range(NSA_KV_GROUPS):
        qt_plain = _stack_heads_t(q_plain, g)
        qt_rope = _stack_heads_t(q_rope, g)
        feat = slice(g * HEAD_DIM, (g + 1) * HEAD_DIM)

        e_c, r_c = _softmax_t(_dot(kc_ref[0], qt_plain), bias_c, hg)
        p_c = e_c * r_c
        o_c = _dot(vct_ref[0], p_c.astype(BF16))
        p_sum = p_c[:, 0:nq]
        for h in range(1, hg):
            p_sum = p_sum + p_c[:, h * nq:(h + 1) * nq]
        imp = jnp.dot(cmapt_ref[...], p_sum, preferred_element_type=F32, precision=lax.Precision.HIGHEST)
        imp = jnp.where(forced, FORCE, jnp.where(valid, imp, -FORCE))

        cnt_scr[...] = jnp.zeros_like(cnt_scr)
        for ig in range(LANES // 8):
            @pl.when(ig * 8 < n_blocks)
            def _(ig=ig):
                parts = []
                for jv in range(LANES // 8):
                    x = imp[jv * 8:(jv + 1) * 8]
                    c = jnp.zeros((8, nq), jnp.int32)
                    for r in range(8):
                        row = imp[ig * 8 + r:ig * 8 + r + 1]
                        if jv > ig:
                            ahead = row >= x
                        elif jv < ig:
                            ahead = row > x
                        else:
                            ahead = (row > x) | ((row == x) & (sub8 > r))
                        c = c + ahead.astype(jnp.int32)
                    parts.append(c)
                cnt_scr[...] += jnp.concatenate(parts, axis=0)
        picked = (cnt_scr[...] < n_sel) & (imp > -0.5 * FORCE)
        bias_scr[...] = jnp.where(picked & (blk < first_blk), 0.0, MASK_BIAS)

        def absorb(s, bias, vt, carry):
            m, l, acc = carry
            s = s + jnp.concatenate([bias] * hg, axis=1)
            m_new = jnp.maximum(m, jnp.max(s, axis=0, keepdims=True))
            alpha = jnp.exp(m - m_new)
            e = jnp.exp(s - m_new)
            return m_new, alpha * l + jnp.sum(e, axis=0, keepdims=True), alpha * acc + _dot(vt, e.astype(BF16))

        def key_tile(kt):
            return pl.multiple_of(jnp.minimum(kt, n_key_tiles - 1) * tk, tk)

        def tile_bias(kt):
            b0 = kt * (tk // SLC_BLOCK)
            rows = [jnp.broadcast_to(bias_scr[pl.ds(b0 + i, 1), :], (SLC_BLOCK, nq)) for i in range(tk // SLC_BLOCK)]
            return jnp.concatenate(rows, axis=0)

        def scores_into(scr, kt):
            scr[...] = _dot(ks_ref[0, pl.ds(key_tile(kt), tk), :], qt_rope)

        def sweep_pair(i, carry):
            scores_into(sb_scr, 2 * i + 1)
            carry = absorb(sa_scr[...], tile_bias(2 * i), vst_ref[0, :, pl.ds(key_tile(2 * i), tk)], carry)
            scores_into(sa_scr, 2 * i + 2)
            return absorb(sb_scr[...], tile_bias(2 * i + 1), vst_ref[0, :, pl.ds(key_tile(2 * i + 1), tk)], carry)

        init = (jnp.full((1, hg * nq), M_INIT, F32), jnp.zeros((1, hg * nq), F32), jnp.zeros((LANES, hg * nq), F32))
        scores_into(sa_scr, 0)
        carry = lax.fori_loop(0, (n_sweep + 1) // 2, sweep_pair, init)
        own = pl.ds(pl.multiple_of(t0, nq), nq)
        _, l_s, acc_s = absorb(_dot(ks_ref[0, own, :], qt_rope), bias_own, vst_ref[0, :, own], carry)
        o_s = acc_s * (1.0 / jnp.where(l_s > 0, l_s, 1.0))

        e_w, r_w = _softmax_t(_dot(kw_ref[0, pl.ds(win_start, win_keys), :], qt_rope), bias_w, hg)
        o_w = _dot(vwt_ref[0, :, pl.ds(win_start, win_keys)], e_w.astype(BF16)) * r_w

        mixes = []
        for h in range(hg):
            row = g * hg * 3 + h * 3
            cols = slice(h * nq, (h + 1) * nq)
            mixes.append(gates_t[row:row + 1] * o_c[feat, cols] + gates_t[row + 1:row + 2] * o_s[feat, cols]
                         + gates_t[row + 2:row + 3] * o_w[feat, cols])
        for hp in range(hg // 2):
            c = (g * hg) // 2 + hp
            pair = jnp.concatenate([mixes[2 * hp], mixes[2 * hp + 1]], axis=0)
            o_ref[0, :, c * LANES:(c + 1) * LANES] = pair.T.astype(o_ref.dtype)


def _nsa(pa3, r3, kc, vct, vst, vwt, cmapt, *, qp_blk, gate_blk, ks_blk, kw_blk):
    b, t, _ = pa3.shape
    n_cmp_pad = kc.shape[1]
    n_sel = min(SLC_TOPN, t // SLC_BLOCK)
    win_keys = min(WIN + Q_BLOCK, t)
    keys = lambda blk: pl.BlockSpec((1, t, LANES), lambda i, j, blk=blk: (i, 0, blk))
    vals_t = pl.BlockSpec((1, LANES, t), lambda i, j: (i, 0, 0))
    return pl.pallas_call(
        functools.partial(_nsa_kernel, n_sel=n_sel, win_keys=win_keys),
        grid=(b, t // Q_BLOCK),
        in_specs=[
            pl.BlockSpec((1, Q_BLOCK, NSA_Q), lambda i, j: (i, j, qp_blk)),
            pl.BlockSpec((1, Q_BLOCK, NSA_Q), lambda i, j: (i, j, 0)),
            pl.BlockSpec((1, Q_BLOCK, LANES), lambda i, j: (i, j, gate_blk)),
            pl.BlockSpec((1, n_cmp_pad, LANES), lambda i, j: (i, 0, 0)),
            pl.BlockSpec((1, LANES, n_cmp_pad), lambda i, j: (i, 0, 0)),
            keys(ks_blk), vals_t, keys(kw_blk), vals_t,
            pl.BlockSpec((LANES, n_cmp_pad), lambda i, j: (0, 0)),
        ],
        out_specs=pl.BlockSpec((1, Q_BLOCK, NSA_Q), lambda i, j: (i, j, 0)),
        out_shape=jax.ShapeDtypeStruct((b, t, NSA_Q), BF16),
        scratch_shapes=[pltpu.VMEM((LANES, Q_BLOCK), jnp.int32), pltpu.VMEM((LANES, Q_BLOCK), F32),
                        pltpu.VMEM((SEL_KEY_TILE, NSA_HPG * Q_BLOCK), F32), pltpu.VMEM((SEL_KEY_TILE, NSA_HPG * Q_BLOCK), F32)],
        compiler_params=_params(("parallel", "arbitrary")),
        name="nsa",
    )(pa3, r3, pa3, kc, vct, r3, vst, r3, vwt, cmapt)


def _slc_map_t(n_cmp_pad, n_slc):
    i = np.arange(n_cmp_pad)[None, :]
    j = np.arange(LANES)[:, None]
    lo = np.maximum(i * CMP_STRIDE, j * SLC_BLOCK)
    hi = np.minimum(i * CMP_STRIDE + CMP_BLOCK, (j + 1) * SLC_BLOCK)
    m = (np.maximum(hi - lo, 0) / CMP_BLOCK).astype(np.float32)
    m[n_slc:, :] = 0.0
    m[:, n_cmp_pad - 1:] = 0.0
    return m


def _dil_kernel(q_ref, kp_ref, kc_ref, vp_ref, vc_ref, o_ref, lse_ref, *, span):
    ub = pl.program_id(2)
    nq = DIL_BLOCK
    lane = lax.broadcasted_iota(jnp.int32, (nq, LANES), 1)
    row = lax.broadcasted_iota(jnp.int32, (2 * nq, 2 * nq), 0) & (nq - 1)
    col = lax.broadcasted_iota(jnp.int32, (2 * nq, 2 * nq), 1)
    diff = row + nq - col
    mask = (diff >= 0) & (diff <= span) & ((col >= nq) | (ub > 0))
    for hp in range(DIL_HEADS // 2):
        cols = slice(hp * LANES, (hp + 1) * LANES)
        qt = q_ref[0, :, cols] * jnp.asarray(SCALE, BF16)
        zero = jnp.zeros_like(qt)
        qs = jnp.concatenate([jnp.where(lane < HEAD_DIM, qt, zero), jnp.where(lane >= HEAD_DIM, qt, zero)], axis=0)
        k = jnp.concatenate([kp_ref[0, :, cols], kc_ref[0, :, cols]], axis=0)
        v = jnp.concatenate([vp_ref[0, :, cols], vc_ref[0, :, cols]], axis=0)
        s = _dot_nt(qs, k)
        m = jnp.max(jnp.where(mask, s, NEG), axis=-1, keepdims=True)
        e = jnp.exp(jnp.where(mask, s - m, NEG))
        d = jnp.sum(e, axis=-1, keepdims=True)
        p = e / jnp.where(d > 0, d, 1.0)
        o = _dot(p.astype(BF16), v)
        lse = m + jnp.log(d)
        o_ref[0, :, cols] = jnp.where(lane < HEAD_DIM, o[:nq], o[nq:]).astype(o_ref.dtype)
        lse_ref[0, :, cols] = jnp.where(lane < HEAD_DIM, lse[:nq], lse[nq:])


def _dilated(r3, v3, gidx, dil, span, q_blk0, k_blk0, v_blk0):
    b, t, cr = r3.shape
    cv = v3.shape[-1]
    ln = t // dil
    rr = r3.reshape(b, ln, dil * cr)
    vv = v3.reshape(b, ln, dil * cv)
    nr, nv = cr // DIL_W, cv // DIL_W
    blk = (1, DIL_BLOCK, DIL_W)
    cur = lambda n, off: pl.BlockSpec(blk, lambda i, r, u: (i, u, r * n + off))
    prev = lambda n, off: pl.BlockSpec(blk, lambda i, r, u: (i, jnp.maximum(u - 1, 0), r * n + off))
    out_spec = pl.BlockSpec(blk, lambda i, r, u: (i, u, r))
    o, lse = pl.pallas_call(
        functools.partial(_dil_kernel, span=span),
        grid=(b, dil, ln // DIL_BLOCK),
        in_specs=[cur(nr, q_blk0 + gidx), prev(nr, k_blk0 + gidx), cur(nr, k_blk0 + gidx),
                  prev(nv, v_blk0 + gidx), cur(nv, v_blk0 + gidx)],
        out_specs=[out_spec, out_spec],
        out_shape=[jax.ShapeDtypeStruct((b, ln, dil * DIL_W), F32), jax.ShapeDtypeStruct((b, ln, dil * DIL_W), F32)],
        compiler_params=_params(("parallel", "parallel", "arbitrary")),
        name=f"dilated_{dil}",
    )(rr, rr, rr, vv, vv)
    return o.reshape(b * t, DIL_W), lse.reshape(b * t, DIL_W)


def _out_kernel(x_ref, yn_ref, o1_ref, o2_ref, o3_ref, l1_ref, l2_ref, l3_ref, g0_ref, g1_ref,
                wn_ref, wd_ref, wo_ref, o_ref):
    l1, l2, l3 = l1_ref[...], l2_ref[...], l3_ref[...]
    m = jnp.maximum(jnp.maximum(l1, l2), l3)
    a1, a2, a3 = jnp.exp(l1 - m), jnp.exp(l2 - m), jnp.exp(l3 - m)
    den = a1 + a2 + a3
    y_dil = (a1 / den) * o1_ref[...] + (a2 / den) * o2_ref[...] + (a3 / den) * o3_ref[...]
    merged = (jax.nn.sigmoid(g0_ref[...]) * _dot(yn_ref[...], wn_ref[...])
              + jax.nn.sigmoid(g1_ref[...]) * _dot(y_dil.astype(BF16), wd_ref[...]))
    o_ref[...] = x_ref[...] + _dot(merged.astype(BF16), wo_ref[...])


def _out_proj(x2, y_nsa, outs, lses, pa, w_bn, w_bd, w_o, tm=512):
    m, d = x2.shape
    row = lambda w: pl.BlockSpec((tm, w), lambda i: (i, 0))
    whole = lambda a: pl.BlockSpec(a.shape, lambda i: (0, 0))
    return pl.pallas_call(
        _out_kernel,
        grid=(m // tm,),
        in_specs=[row(d), row(NSA_Q)] + [row(DIL_W)] * 6
        + [pl.BlockSpec((tm, d), lambda i: (i, 0)), pl.BlockSpec((tm, d), lambda i: (i, 1))]
        + [whole(w_bn), whole(w_bd), whole(w_o)],
        out_specs=row(d),
        out_shape=jax.ShapeDtypeStruct((m, d), F32),
        compiler_params=_params(("parallel",)),
        name="out_proj",
    )(x2, y_nsa, *outs, *lses, pa, pa, w_bn, w_bd, w_o)


def _mixer(x2, pos_col, inv_lane, b, t, norm_mix, w_in, cmp_pos, cmp_w1, cmp_w2, w_branch_nsa, w_branch_dil, w_out):
    d = x2.shape[1]
    dh = HEAD_DIM
    o_q, o_kv, o_gn = 0, NSA_Q, NSA_Q + 6 * NSA_KV
    o_d = o_gn + 3 * NSA_HEADS
    o_gm = o_d + 3 * DIL_GROUPS * DIL_W
    kv = lambda i: w_in[:, o_kv + i * NSA_KV:o_kv + (i + 1) * NSA_KV]
    dd = lambda i: w_in[:, o_d + i * DIL_GROUPS * DIL_W:o_d + (i + 1) * DIL_GROUPS * DIL_W]
    w_q = w_in[:, o_q:o_q + NSA_Q]
    gn_pad = jnp.zeros((d, 2 * LANES - 3 * NSA_HEADS), w_in.dtype)
    w_a = jnp.concatenate([w_in[:, o_gm:o_gm + 2 * d], w_q, kv(0), kv(1), w_in[:, o_gn:o_gn + 3 * NSA_HEADS], gn_pad],
                          axis=1).astype(BF16)
    w_b = jnp.concatenate([w_q, kv(2), kv(4), dd(0), dd(1)], axis=1).astype(BF16)
    w_c = jnp.concatenate([kv(3), kv(5), dd(2)], axis=1).astype(BF16)

    pa = _proj(x2, norm_mix, w_a, F32)
    rb = _proj(x2, norm_mix, w_b, BF16, pos=pos_col, inv_lane=inv_lane)
    vc_ = _proj(x2, norm_mix, w_c, BF16)

    n_seg = t // CMP_STRIDE
    src0 = 2 * d + NSA_Q
    seg = pa[:, src0:src0 + 2 * NSA_KV].reshape(b, t, 2, NSA_KV_GROUPS, dh)
    seg = seg.transpose(2, 0, 3, 1, 4).reshape(2, b, NSA_KV_GROUPS, n_seg, CMP_STRIDE * dh)
    pos = cmp_pos.reshape(2, 2, 1, CMP_STRIDE * dh)
    w1 = cmp_w1.reshape(2, 2, CMP_STRIDE * dh, -1).astype(BF16)
    cmp = _compress(seg, pos, w1, cmp_w2.astype(BF16))
    cmp = cmp.transpose(0, 1, 3, 2, 4).reshape(2, b, n_seg, NSA_KV).astype(BF16)

    cmapt = jnp.asarray(_slc_map_t(n_seg, t // SLC_BLOCK))
    pa3 = pa.reshape(b, t, -1)
    r3 = rb.reshape(b, t, -1)
    v3 = vc_.reshape(b, t, -1)
    vst = v3[:, :, 0:NSA_KV].transpose(0, 2, 1)
    vwt = v3[:, :, NSA_KV:2 * NSA_KV].transpose(0, 2, 1)
    y_nsa = _nsa(pa3, r3, cmp[0], cmp[1].transpose(0, 2, 1), vst, vwt, cmapt,
                 qp_blk=2 * d // NSA_Q, gate_blk=(2 * d + NSA_Q + 2 * NSA_KV) // LANES,
                 ks_blk=NSA_Q // LANES, kw_blk=NSA_Q // LANES + 1)

    outs, lses = [], []
    q_blk0 = (NSA_Q + 2 * NSA_KV) // DIL_W
    for gidx, (window, dil) in enumerate(DIL_PAIRS):
        o, lse = _dilated(r3, v3, gidx, dil, window // dil, q_blk0, q_blk0 + DIL_GROUPS, (2 * NSA_KV) // DIL_W)
        outs.append(o)
        lses.append(lse)

    return _out_proj(x2, y_nsa.reshape(b * t, NSA_Q), outs, lses, pa,
                     w_branch_nsa.astype(BF16), w_branch_dil.astype(BF16), w_out.astype(BF16))


def kernel(x, positions, norm_ffn1, ffn1_w_in, ffn1_w_out, norm_mix, w_in, cmp_pos, cmp_w1, cmp_w2,
           w_branch_nsa, w_branch_dil, w_out, norm_ffn2, ffn2_w_in, ffn2_w_out, norm_final):
    b, t, d = x.shape
    depth = norm_ffn1.shape[0]
    x2 = x.reshape(b * t, d)
    pos_col = positions.reshape(b * t, 1).astype(jnp.int32)
    inv = jnp.power(ROPE_THETA, -jnp.arange(0, ROPE_DIM, 2, dtype=F32) / ROPE_DIM)
    inv_head = jnp.concatenate([inv, inv, jnp.zeros((HEAD_DIM - ROPE_DIM,), F32)])
    inv_lane = jnp.tile(inv_head, LANES // HEAD_DIM).reshape(1, LANES)
    for l in range(depth):
        last = l == depth - 1
        x2 = _ffn(x2, norm_ffn1[l], ffn1_w_in[l].astype(BF16), ffn1_w_out[l].astype(BF16))
        x2 = _mixer(x2, pos_col, inv_lane, b, t, norm_mix[l], w_in[l], cmp_pos[l], cmp_w1[l], cmp_w2[l],
                    w_branch_nsa[l], w_branch_dil[l], w_out[l])
        x2 = _ffn(x2, norm_ffn2[l], ffn2_w_in[l].astype(BF16), ffn2_w_out[l].astype(BF16),
                  final_gain=norm_final if last else None)
    if depth == 0:
        raise ValueError("depth must be positive")
    return x2.reshape(b, t, d)
```

```python
import functools

import numpy as np
import jax
import jax.numpy as jnp
from jax import lax
from jax.experimental import pallas as pl
from jax.experimental.pallas import tpu as pltpu

HEAD_DIM = 64
ROPE_DIM = HEAD_DIM // 4
ROPE_THETA = 500000.0
NORM_EPS = 1e-6

NSA_HEADS = 8
NSA_KV_GROUPS = 2
NSA_HPG = NSA_HEADS // NSA_KV_GROUPS
CMP_BLOCK = 32
CMP_STRIDE = 16
SLC_BLOCK = 64
SLC_TOPN = 16
WIN = 512
Q_BLOCK = 128
FORCE = 1e9
NEG = -1e30

DIL_PAIRS = ((128, 1), (512, 4), (2048, 16))
DIL_GROUPS = len(DIL_PAIRS)
DIL_HEADS = 4
DIL_BLOCK = 128

NSA_Q = NSA_HEADS * HEAD_DIM
NSA_KV = NSA_KV_GROUPS * HEAD_DIM
DIL_W = DIL_HEADS * HEAD_DIM

LANES = 128
SEL_KEY_TILE = 512
VMEM_LIMIT = 56 * 1024 * 1024

F32 = jnp.float32
BF16 = jnp.bfloat16
SCALE = HEAD_DIM ** -0.5
LOG2E = 1.4426950408889634


def _dot(a, b):
    return jnp.dot(a, b, preferred_element_type=F32)


def _dot_nt(a, b):
    return lax.dot_general(a, b, (((1,), (1,)), ((), ())), preferred_element_type=F32)


def _rmsnorm(x, g):
    return x * lax.rsqrt(jnp.mean(x * x, axis=-1, keepdims=True) + NORM_EPS) * g


def _params(sem):
    return pltpu.CompilerParams(dimension_semantics=sem, vmem_limit_bytes=VMEM_LIMIT)


def _ffn_kernel(*refs, final):
    if final:
        x_ref, g_ref, wu_ref, wv_ref, wo_ref, gf_ref, o_ref, h_scr, acc_scr = refs
    else:
        x_ref, g_ref, wu_ref, wv_ref, wo_ref, o_ref, h_scr, acc_scr = refs
    j = pl.program_id(1)

    @pl.when(j == 0)
    def _():
        h_scr[...] = _rmsnorm(x_ref[...], g_ref[...]).astype(BF16)
        acc_scr[...] = jnp.zeros_like(acc_scr)

    h = h_scr[...]
    u = _dot(h, wu_ref[...])
    v = _dot(h, wv_ref[...])
    a = (u * jax.nn.sigmoid(u)) * v
    acc_scr[...] += _dot(a.astype(BF16), wo_ref[...])

    @pl.when(j == pl.num_programs(1) - 1)
    def _():
        y = x_ref[...] + 0.5 * acc_scr[...]
        if final:
            y = _rmsnorm(y, gf_ref[...])
        o_ref[...] = y


def _ffn(x2, gain, w_in, w_out, final_gain=None, tm=1024, tf=256):
    m, d = x2.shape
    dff = w_out.shape[0]
    nf = dff // tf
    final = final_gain is not None
    in_specs = [
        pl.BlockSpec((tm, d), lambda i, j: (i, 0)),
        pl.BlockSpec((1, d), lambda i, j: (0, 0)),
        pl.BlockSpec((d, tf), lambda i, j: (0, j)),
        pl.BlockSpec((d, tf), lambda i, j: (0, j + nf)),
        pl.BlockSpec((tf, d), lambda i, j: (j, 0)),
    ]
    args = [x2, gain.reshape(1, d), w_in, w_in, w_out]
    if final:
        in_specs.append(pl.BlockSpec((1, d), lambda i, j: (0, 0)))
        args.append(final_gain.reshape(1, d))
    return pl.pallas_call(
        functools.partial(_ffn_kernel, final=final),
        grid=(m // tm, nf),
        in_specs=in_specs,
        out_specs=pl.BlockSpec((tm, d), lambda i, j: (i, 0)),
        out_shape=jax.ShapeDtypeStruct((m, d), F32),
        scratch_shapes=[pltpu.VMEM((tm, d), BF16), pltpu.VMEM((tm, d), F32)],
        compiler_params=_params(("parallel", "arbitrary")),
        name="ffn_final" if final else "ffn",
    )(*args)


def _rope_coeffs(pos_col, inv_lane):
    ang = pos_col.astype(F32) * inv_lane
    c, s = jnp.cos(ang), jnp.sin(ang)
    d = lax.broadcasted_iota(jnp.int32, ang.shape, 1) & (HEAD_DIM - 1)
    half = ROPE_DIM // 2
    coef_c = jnp.where(d < ROPE_DIM, c, 1.0)
    coef_hi = jnp.where(d < half, -s, 0.0)
    coef_lo = jnp.where((d >= half) & (d < ROPE_DIM), s, 0.0)
    return coef_c, coef_hi, coef_lo


def _proj_kernel(*refs, rope):
    if rope:
        x_ref, g_ref, w_ref, pos_ref, inv_ref, o_ref = refs
    else:
        x_ref, g_ref, w_ref, o_ref = refs
    h = _rmsnorm(x_ref[...], g_ref[...]).astype(BF16)
    y = _dot(h, w_ref[...])
    if not rope:
        o_ref[...] = y.astype(o_ref.dtype)
        return
    coef_c, coef_hi, coef_lo = _rope_coeffs(pos_ref[...], inv_ref[...])
    half = ROPE_DIM // 2
    for t in range(y.shape[1] // LANES):
        yt = y[:, t * LANES:(t + 1) * LANES]
        hi = pltpu.roll(yt, LANES - half, axis=1)
        lo = pltpu.roll(yt, half, axis=1)
        o_ref[:, t * LANES:(t + 1) * LANES] = (yt * coef_c + hi * coef_hi + lo * coef_lo).astype(o_ref.dtype)


def _proj(x2, gain, w, out_dtype, pos=None, inv_lane=None, tm=512):
    m, d = x2.shape
    n = w.shape[1]
    rope = pos is not None
    in_specs = [
        pl.BlockSpec((tm, d), lambda i: (i, 0)),
        pl.BlockSpec((1, d), lambda i: (0, 0)),
        pl.BlockSpec((d, n), lambda i: (0, 0)),
    ]
    args = [x2, gain.reshape(1, d), w]
    if rope:
        in_specs += [pl.BlockSpec((tm, 1), lambda i: (i, 0)), pl.BlockSpec((1, LANES), lambda i: (0, 0))]
        args += [pos, inv_lane]
    return pl.pallas_call(
        functools.partial(_proj_kernel, rope=rope),
        grid=(m // tm,),
        in_specs=in_specs,
        out_specs=pl.BlockSpec((tm, n), lambda i: (i, 0)),
        out_shape=jax.ShapeDtypeStruct((m, n), out_dtype),
        compiler_params=_params(("parallel",)),
        name="proj_rope" if rope else ("proj_f32" if out_dtype == F32 else "proj_bf16"),
    )(*args)


def _compress_kernel(seg_ref, pos_ref, w1_ref, w2_ref, o_ref):
    seg = seg_ref[0, 0, 0]
    a = _dot((seg + pos_ref[0, 0]).astype(BF16), w1_ref[0, 0])
    b = _dot((seg + pos_ref[0, 1]).astype(BF16), w1_ref[0, 1])
    n_seg = seg.shape[0]
    pre = a + pltpu.roll(b, n_seg - 1, axis=0)
    hid = pre * jax.nn.sigmoid(pre)
    o_ref[0, 0, 0] = _dot(hid.astype(BF16), w2_ref[0])


def _compress(seg, pos, w1, w2):
    _, b, g, n_seg, k = seg.shape
    hid = w1.shape[-1]
    dh = w2.shape[-1]
    return pl.pallas_call(
        _compress_kernel,
        grid=(2, b, g),
        in_specs=[
            pl.BlockSpec((1, 1, 1, n_seg, k), lambda c, i, j: (c, i, j, 0, 0)),
            pl.BlockSpec((1, 2, 1, k), lambda c, i, j: (c, 0, 0, 0)),
            pl.BlockSpec((1, 2, k, hid), lambda c, i, j: (c, 0, 0, 0)),
            pl.BlockSpec((1, hid, dh), lambda c, i, j: (c, 0, 0)),
        ],
        out_specs=pl.BlockSpec((1, 1, 1, n_seg, dh), lambda c, i, j: (c, i, j, 0, 0)),
        out_shape=jax.ShapeDtypeStruct((2, b, g, n_seg, dh), F32),
        compiler_params=_params(("parallel", "parallel", "parallel")),
        name="compress",
    )(seg, pos, w1, w2)


MASK_BIAS = NEG
M_INIT = -1e29


def _stack_heads_t(q, g):
    zeros = jnp.zeros((HEAD_DIM, q.shape[0]), F32)
    parts = []
    for h in range(NSA_HPG):
        hg = g * NSA_HPG + h
        tile_t = (q[:, (hg // 2) * LANES:(hg // 2 + 1) * LANES] * (SCALE * LOG2E)).T
        feat = tile_t[(hg % 2) * HEAD_DIM:(hg % 2 + 1) * HEAD_DIM]
        parts.append(jnp.concatenate([feat, zeros] if g == 0 else [zeros, feat], axis=0))
    return jnp.concatenate(parts, axis=1).astype(BF16)


def _softmax_t(s, bias, hg):
    sb = s + jnp.concatenate([bias] * hg, axis=1)
    m = jnp.maximum(jnp.max(sb, axis=0, keepdims=True), M_INIT)
    e = jnp.exp2(sb - m)
    d = jnp.sum(e, axis=0, keepdims=True)
    return e, 1.0 / jnp.where(d > 0, d, 1.0)


def _nsa_kernel(qp_ref, qr_ref, gate_ref, kc_ref, vct_ref, ks_ref, vst_ref, kw_ref, vwt_ref, cmapt_ref, et_ref,
                o_ref, cnt_scr, sa_scr, sb_scr, *, n_sel, win_keys):
    qb = pl.program_id(1)
    t0 = qb * Q_BLOCK
    nq = Q_BLOCK
    hg = NSA_HPG
    tk = SEL_KEY_TILE
    n_cmp_pad = kc_ref.shape[1]
    t_row = t0 + lax.broadcasted_iota(jnp.int32, (1, nq), 1)
    blk = lax.broadcasted_iota(jnp.int32, (LANES, nq), 0)
    sub8 = lax.broadcasted_iota(jnp.int32, (8, nq), 0)

    q_plain = qp_ref[0]
    q_rope = qr_ref[0].astype(F32)
    gates_t = jax.nn.sigmoid(gate_ref[0]).T

    cmp_end = lax.broadcasted_iota(jnp.int32, (n_cmp_pad, nq), 0) * CMP_STRIDE + (CMP_BLOCK - 1)
    bias_c = jnp.where(cmp_end <= t_row, 0.0, MASK_BIAS)
    cur = t_row >> 6
    forced = (blk == 0) | (blk == cur) | (blk == cur - 1)
    valid = blk <= cur
    n_blocks = (qb + 1) * (Q_BLOCK // SLC_BLOCK)

    win_start = pl.multiple_of(jnp.maximum(t0 - WIN, 0), Q_BLOCK)
    diff_w = t_row - (win_start + lax.broadcasted_iota(jnp.int32, (win_keys, nq), 0))
    bias_w = jnp.where((diff_w >= 0) & (diff_w < WIN), 0.0, MASK_BIAS)

    first_blk = qb * (Q_BLOCK // SLC_BLOCK)
    n_sweep = (t0 + tk - 1) // tk
    n_key_tiles = ks_ref.shape[1] // tk
    bias_own = jnp.where(lax.broadcasted_iota(jnp.int32, (nq, nq), 0) <= lax.broadcasted_iota(jnp.int32, (nq, nq), 1),
                         0.0, MASK_BIAS)

    for g in range(NSA_KV_GROUPS):
        qt_plain = _stack_heads_t(q_plain, g)
        qt_rope = _stack_heads_t(q_rope, g)
        feat = slice(g * HEAD_DIM, (g + 1) * HEAD_DIM)

        e_c, r_c = _softmax_t(_dot(kc_ref[0], qt_plain), bias_c, hg)
        p_c = e_c * r_c
        o_c = _dot(vct_ref[0], p_c.astype(BF16))
        p_sum = p_c[:, 0:nq]
        for h in range(1, hg):
            p_sum = p_sum + p_c[:, h * nq:(h + 1) * nq]
        imp = jnp.dot(cmapt_ref[...], p_sum, preferred_element_type=F32, precision=lax.Precision.HIGHEST)
        imp = jnp.where(forced, FORCE, jnp.where(valid, imp, -FORCE))

        cnt_scr[...] = jnp.zeros_like(cnt_scr)
        for ig in range(LANES // 8):
            @pl.when(ig * 8 < n_blocks)
            def _(ig=ig):
                parts = []
                for jv in range(LANES // 8):
                    x = imp[jv * 8:(jv + 1) * 8]
                    c = jnp.zeros((8, nq), jnp.int32)
                    for r in range(8):
                        row = imp[ig * 8 + r:ig * 8 + r + 1]
                        if jv > ig:
                            ahead = row >= x
                        elif jv < ig:
                            ahead = row > x
                        else:
                            ahead = (row > x) | ((row == x) & (sub8 > r))
                        c = c + ahead.astype(jnp.int32)
                    parts.append(c)
                cnt_scr[...] += jnp.concatenate(parts, axis=0)
        picked = (cnt_scr[...] < n_sel) & (imp > -0.5 * FORCE)
        bias_blk = jnp.where(picked & (blk < first_blk), 0.0, MASK_BIAS).astype(BF16)
        q_aug = jnp.concatenate([qt_rope, jnp.concatenate([bias_blk] * hg, axis=1)], axis=0)

        def absorb(s, vt, carry):
            m, acc = carry
            m_new = jnp.maximum(m, jnp.max(s, axis=0, keepdims=True))
            e = jnp.exp2(s - m_new)
            return m_new, jnp.exp2(m - m_new) * acc + _dot(vt, e.astype(BF16))

        def key_tile(kt):
            return pl.multiple_of(jnp.minimum(kt, n_key_tiles - 1) * tk, tk)

        def scores_into(scr, kt):
            k0 = key_tile(kt)
            scr[...] = _dot(jnp.concatenate([ks_ref[0, pl.ds(k0, tk), :], et_ref[pl.ds(k0, tk), :]], axis=1), q_aug)

        def sweep_pair(i, carry):
            scores_into(sb_scr, 2 * i + 1)
            carry = absorb(sa_scr[...], vst_ref[0, g, :, pl.ds(key_tile(2 * i), tk)], carry)
            scores_into(sa_scr, 2 * i + 2)
            return absorb(sb_scr[...], vst_ref[0, g, :, pl.ds(key_tile(2 * i + 1), tk)], carry)

        init = (jnp.full((1, hg * nq), M_INIT, F32), jnp.zeros((vst_ref.shape[2], hg * nq), F32))
        scores_into(sa_scr, 0)
        carry = lax.fori_loop(0, (n_sweep + 1) // 2, sweep_pair, init)
        own = pl.ds(pl.multiple_of(t0, nq), nq)
        s_own = _dot(ks_ref[0, own, :], qt_rope) + jnp.concatenate([bias_own] * hg, axis=1)
        _, acc_s = absorb(s_own, vst_ref[0, g, :, own], carry)
        o_s = acc_s[0:HEAD_DIM] * (1.0 / acc_s[HEAD_DIM:HEAD_DIM + 1])

        s_w = _dot(kw_ref[0, pl.ds(win_start, win_keys), :], qt_rope) + jnp.concatenate([bias_w] * hg, axis=1)
        e_w = jnp.exp2(s_w - jnp.max(s_w, axis=0, keepdims=True))
        acc_w = _dot(vwt_ref[0, g, :, pl.ds(win_start, win_keys)], e_w.astype(BF16))
        o_w = acc_w[0:HEAD_DIM] * (1.0 / acc_w[HEAD_DIM:HEAD_DIM + 1])

        mixes = []
        for h in range(hg):
            row = g * hg * 3 + h * 3
            cols = slice(h * nq, (h + 1) * nq)
            mixes.append(gates_t[row:row + 1] * o_c[feat, cols] + gates_t[row + 1:row + 2] * o_s[:, cols]
                         + gates_t[row + 2:row + 3] * o_w[:, cols])
        for hp in range(hg // 2):
            c = (g * hg) // 2 + hp
            pair = jnp.concatenate([mixes[2 * hp], mixes[2 * hp + 1]], axis=0)
            o_ref[0, :, c * LANES:(c + 1) * LANES] = pair.T.astype(o_ref.dtype)


def _nsa(pa3, r3, kc, vct, vst, vwt, cmapt, expand, *, qp_blk, gate_blk, ks_blk, kw_blk):
    b, t, _ = pa3.shape
    n_cmp_pad = kc.shape[1]
    n_sel = min(SLC_TOPN, t // SLC_BLOCK)
    win_keys = min(WIN + Q_BLOCK, t)
    keys = lambda blk: pl.BlockSpec((1, t, LANES), lambda i, j, blk=blk: (i, 0, blk))
    vals_t = pl.BlockSpec((1,) + vst.shape[1:], lambda i, j: (i, 0, 0, 0))
    return pl.pallas_call(
        functools.partial(_nsa_kernel, n_sel=n_sel, win_keys=win_keys),
        grid=(b, t // Q_BLOCK),
        in_specs=[
            pl.BlockSpec((1, Q_BLOCK, NSA_Q), lambda i, j: (i, j, qp_blk)),
            pl.BlockSpec((1, Q_BLOCK, NSA_Q), lambda i, j: (i, j, 0)),
            pl.BlockSpec((1, Q_BLOCK, LANES), lambda i, j: (i, j, gate_blk)),
            pl.BlockSpec((1, n_cmp_pad, LANES), lambda i, j: (i, 0, 0)),
            pl.BlockSpec((1, LANES, n_cmp_pad), lambda i, j: (i, 0, 0)),
            keys(ks_blk), vals_t, keys(kw_blk), vals_t,
            pl.BlockSpec((LANES, n_cmp_pad), lambda i, j: (0, 0)),
            pl.BlockSpec((t, LANES), lambda i, j: (0, 0)),
        ],
        out_specs=pl.BlockSpec((1, Q_BLOCK, NSA_Q), lambda i, j: (i, j, 0)),
        out_shape=jax.ShapeDtypeStruct((b, t, NSA_Q), BF16),
        scratch_shapes=[pltpu.VMEM((LANES, Q_BLOCK), jnp.int32),
                        pltpu.VMEM((SEL_KEY_TILE, NSA_HPG * Q_BLOCK), F32), pltpu.VMEM((SEL_KEY_TILE, NSA_HPG * Q_BLOCK), F32)],
        compiler_params=_params(("parallel", "arbitrary")),
        name="nsa",
    )(pa3, r3, pa3, kc, vct, r3, vst, r3, vwt, cmapt, expand)


def _values_t(v):
    b, t, _ = v.shape
    vt = v.reshape(b, t, NSA_KV_GROUPS, HEAD_DIM).transpose(0, 2, 3, 1)
    ones = jnp.ones((b, NSA_KV_GROUPS, 1, t), v.dtype)
    pad = jnp.zeros((b, NSA_KV_GROUPS, 15, t), v.dtype)
    return jnp.concatenate([vt, ones, pad], axis=2)


def _slc_map_t(n_cmp_pad, n_slc):
    i = np.arange(n_cmp_pad)[None, :]
    j = np.arange(LANES)[:, None]
    lo = np.maximum(i * CMP_STRIDE, j * SLC_BLOCK)
    hi = np.minimum(i * CMP_STRIDE + CMP_BLOCK, (j + 1) * SLC_BLOCK)
    m = (np.maximum(hi - lo, 0) / CMP_BLOCK).astype(np.float32)
    m[n_slc:, :] = 0.0
    m[:, n_cmp_pad - 1:] = 0.0
    return m


def _dil_kernel(q_ref, kp_ref, kc_ref, vp_ref, vc_ref, o_ref, lse_ref, *, span):
    ub = pl.program_id(2)
    nq = DIL_BLOCK
    lane = lax.broadcasted_iota(jnp.int32, (nq, LANES), 1)
    row = lax.broadcasted_iota(jnp.int32, (2 * nq, 2 * nq), 0) & (nq - 1)
    col = lax.broadcasted_iota(jnp.int32, (2 * nq, 2 * nq), 1)
    diff = row + nq - col
    mask = (diff >= 0) & (diff <= span) & ((col >= nq) | (ub > 0))
    for hp in range(DIL_HEADS // 2):
        cols = slice(hp * LANES, (hp + 1) * LANES)
        qt = q_ref[0, :, cols] * jnp.asarray(SCALE, BF16)
        zero = jnp.zeros_like(qt)
        qs = jnp.concatenate([jnp.where(lane < HEAD_DIM, qt, zero), jnp.where(lane >= HEAD_DIM, qt, zero)], axis=0)
        k = jnp.concatenate([kp_ref[0, :, cols], kc_ref[0, :, cols]], axis=0)
        v = jnp.concatenate([vp_ref[0, :, cols], vc_ref[0, :, cols]], axis=0)
        s = _dot_nt(qs, k)
        m = jnp.max(jnp.where(mask, s, NEG), axis=-1, keepdims=True)
        e = jnp.exp(jnp.where(mask, s - m, NEG))
        d = jnp.sum(e, axis=-1, keepdims=True)
        p = e / jnp.where(d > 0, d, 1.0)
        o = _dot(p.astype(BF16), v)
        lse = m + jnp.log(d)
        o_ref[0, :, cols] = jnp.where(lane < HEAD_DIM, o[:nq], o[nq:]).astype(o_ref.dtype)
        lse_ref[0, :, cols] = jnp.where(lane < HEAD_DIM, lse[:nq], lse[nq:])


def _dilated(r3, v3, gidx, dil, span, q_blk0, k_blk0, v_blk0):
    b, t, cr = r3.shape
    cv = v3.shape[-1]
    ln = t // dil
    rr = r3.reshape(b, ln, dil * cr)
    vv = v3.reshape(b, ln, dil * cv)
    nr, nv = cr // DIL_W, cv // DIL_W
    blk = (1, DIL_BLOCK, DIL_W)
    cur = lambda n, off: pl.BlockSpec(blk, lambda i, r, u: (i, u, r * n + off))
    prev = lambda n, off: pl.BlockSpec(blk, lambda i, r, u: (i, jnp.maximum(u - 1, 0), r * n + off))
    out_spec = pl.BlockSpec(blk, lambda i, r, u: (i, u, r))
    o, lse = pl.pallas_call(
        functools.partial(_dil_kernel, span=span),
        grid=(b, dil, ln // DIL_BLOCK),
        in_specs=[cur(nr, q_blk0 + gidx), prev(nr, k_blk0 + gidx), cur(nr, k_blk0 + gidx),
                  prev(nv, v_blk0 + gidx), cur(nv, v_blk0 + gidx)],
        out_specs=[out_spec, out_spec],
        out_shape=[jax.ShapeDtypeStruct((b, ln, dil * DIL_W), F32), jax.ShapeDtypeStruct((b, ln, dil * DIL_W), F32)],
        compiler_params=_params(("parallel", "parallel", "arbitrary")),
        name=f"dilated_{dil}",
    )(rr, rr, rr, vv, vv)
    return o.reshape(b * t, DIL_W), lse.reshape(b * t, DIL_W)


def _out_kernel(x_ref, yn_ref, o1_ref, o2_ref, o3_ref, l1_ref, l2_ref, l3_ref, g0_ref, g1_ref,
                wn_ref, wd_ref, wo_ref, o_ref):
    l1, l2, l3 = l1_ref[...], l2_ref[...], l3_ref[...]
    m = jnp.maximum(jnp.maximum(l1, l2), l3)
    a1, a2, a3 = jnp.exp(l1 - m), jnp.exp(l2 - m), jnp.exp(l3 - m)
    den = a1 + a2 + a3
    y_dil = (a1 / den) * o1_ref[...] + (a2 / den) * o2_ref[...] + (a3 / den) * o3_ref[...]
    merged = (jax.nn.sigmoid(g0_ref[...]) * _dot(yn_ref[...], wn_ref[...])
              + jax.nn.sigmoid(g1_ref[...]) * _dot(y_dil.astype(BF16), wd_ref[...]))
    o_ref[...] = x_ref[...] + _dot(merged.astype(BF16), wo_ref[...])


def _out_proj(x2, y_nsa, outs, lses, pa, w_bn, w_bd, w_o, tm=512):
    m, d = x2.shape
    row = lambda w: pl.BlockSpec((tm, w), lambda i: (i, 0))
    whole = lambda a: pl.BlockSpec(a.shape, lambda i: (0, 0))
    return pl.pallas_call(
        _out_kernel,
        grid=(m // tm,),
        in_specs=[row(d), row(NSA_Q)] + [row(DIL_W)] * 6
        + [pl.BlockSpec((tm, d), lambda i: (i, 0)), pl.BlockSpec((tm, d), lambda i: (i, 1))]
        + [whole(w_bn), whole(w_bd), whole(w_o)],
        out_specs=row(d),
        out_shape=jax.ShapeDtypeStruct((m, d), F32),
        compiler_params=_params(("parallel",)),
        name="out_proj",
    )(x2, y_nsa, *outs, *lses, pa, pa, w_bn, w_bd, w_o)


def _mixer(x2, pos_col, inv_lane, b, t, norm_mix, w_in, cmp_pos, cmp_w1, cmp_w2, w_branch_nsa, w_branch_dil, w_out):
    d = x2.shape[1]
    dh = HEAD_DIM
    o_q, o_kv, o_gn = 0, NSA_Q, NSA_Q + 6 * NSA_KV
    o_d = o_gn + 3 * NSA_HEADS
    o_gm = o_d + 3 * DIL_GROUPS * DIL_W
    kv = lambda i: w_in[:, o_kv + i * NSA_KV:o_kv + (i + 1) * NSA_KV]
    dd = lambda i: w_in[:, o_d + i * DIL_GROUPS * DIL_W:o_d + (i + 1) * DIL_GROUPS * DIL_W]
    w_q = w_in[:, o_q:o_q + NSA_Q]
    gn_pad = jnp.zeros((d, 2 * LANES - 3 * NSA_HEADS), w_in.dtype)
    w_a = jnp.concatenate([w_in[:, o_gm:o_gm + 2 * d], w_q, kv(0), kv(1), w_in[:, o_gn:o_gn + 3 * NSA_HEADS], gn_pad],
                          axis=1).astype(BF16)
    w_b = jnp.concatenate([w_q, kv(2), kv(4), dd(0), dd(1)], axis=1).astype(BF16)
    w_c = jnp.concatenate([kv(3), kv(5), dd(2)], axis=1).astype(BF16)

    pa = _proj(x2, norm_mix, w_a, F32)
    rb = _proj(x2, norm_mix, w_b, BF16, pos=pos_col, inv_lane=inv_lane)
    vc_ = _proj(x2, norm_mix, w_c, BF16)

    n_seg = t // CMP_STRIDE
    src0 = 2 * d + NSA_Q
    seg = pa[:, src0:src0 + 2 * NSA_KV].reshape(b, t, 2, NSA_KV_GROUPS, dh)
    seg = seg.transpose(2, 0, 3, 1, 4).reshape(2, b, NSA_KV_GROUPS, n_seg, CMP_STRIDE * dh)
    pos = cmp_pos.reshape(2, 2, 1, CMP_STRIDE * dh)
    w1 = cmp_w1.reshape(2, 2, CMP_STRIDE * dh, -1).astype(BF16)
    cmp = _compress(seg, pos, w1, cmp_w2.astype(BF16))
    cmp = cmp.transpose(0, 1, 3, 2, 4).reshape(2, b, n_seg, NSA_KV).astype(BF16)

    cmapt = jnp.asarray(_slc_map_t(n_seg, t // SLC_BLOCK))
    pa3 = pa.reshape(b, t, -1)
    r3 = rb.reshape(b, t, -1)
    v3 = vc_.reshape(b, t, -1)
    vst = _values_t(v3[:, :, 0:NSA_KV])
    vwt = _values_t(v3[:, :, NSA_KV:2 * NSA_KV])
    expand = jnp.asarray(np.arange(t)[:, None] // SLC_BLOCK == np.arange(LANES)[None, :], BF16)
    y_nsa = _nsa(pa3, r3, cmp[0], cmp[1].transpose(0, 2, 1), vst, vwt, cmapt, expand,
                 qp_blk=2 * d // NSA_Q, gate_blk=(2 * d + NSA_Q + 2 * NSA_KV) // LANES,
                 ks_blk=NSA_Q // LANES, kw_blk=NSA_Q // LANES + 1)

    outs, lses = [], []
    q_blk0 = (NSA_Q + 2 * NSA_KV) // DIL_W
    for gidx, (window, dil) in enumerate(DIL_PAIRS):
        o, lse = _dilated(r3, v3, gidx, dil, window // dil, q_blk0, q_blk0 + DIL_GROUPS, (2 * NSA_KV) // DIL_W)
        outs.append(o)
        lses.append(lse)

    return _out_proj(x2, y_nsa.reshape(b * t, NSA_Q), outs, lses, pa,
                     w_branch_nsa.astype(BF16), w_branch_dil.astype(BF16), w_out.astype(BF16))


def kernel(x, positions, norm_ffn1, ffn1_w_in, ffn1_w_out, norm_mix, w_in, cmp_pos, cmp_w1, cmp_w2,
           w_branch_nsa, w_branch_dil, w_out, norm_ffn2, ffn2_w_in, ffn2_w_out, norm_final):
    b, t, d = x.shape
    depth = norm_ffn1.shape[0]
    x2 = x.reshape(b * t, d)
    pos_col = positions.reshape(b * t, 1).astype(jnp.int32)
    inv = jnp.power(ROPE_THETA, -jnp.arange(0, ROPE_DIM, 2, dtype=F32) / ROPE_DIM)
    inv_head = jnp.concatenate([inv, inv, jnp.zeros((HEAD_DIM - ROPE_DIM,), F32)])
    inv_lane = jnp.tile(inv_head, LANES // HEAD_DIM).reshape(1, LANES)
    for l in range(depth):
        last = l == depth - 1
        x2 = _ffn(x2, norm_ffn1[l], ffn1_w_in[l].astype(BF16), ffn1_w_out[l].astype(BF16))
        x2 = _mixer(x2, pos_col, inv_lane, b, t, norm_mix[l], w_in[l], cmp_pos[l], cmp_w1[l], cmp_w2[l],
                    w_branch_nsa[l], w_branch_dil[l], w_out[l])
        x2 = _ffn(x2, norm_ffn2[l], ffn2_w_in[l].astype(BF16), ffn2_w_out[l].astype(BF16),
                  final_gain=norm_final if last else None)
    if depth == 0:
        raise ValueError("depth must be positive")
    return x2.reshape(b, t, d)
```

```python
import functools

import numpy as np
import jax
import jax.numpy as jnp
from jax import lax
from jax.experimental import pallas as pl
from jax.experimental.pallas import tpu as pltpu

HEAD_DIM = 64
ROPE_DIM = HEAD_DIM // 4
ROPE_THETA = 500000.0
NORM_EPS = 1e-6

NSA_HEADS = 8
NSA_KV_GROUPS = 2
NSA_HPG = NSA_HEADS // NSA_KV_GROUPS
CMP_BLOCK = 32
CMP_STRIDE = 16
SLC_BLOCK = 64
SLC_TOPN = 16
WIN = 512
Q_BLOCK = 128
FORCE = 1e9
NEG = -1e30

DIL_PAIRS = ((128, 1), (512, 4), (2048, 16))
DIL_GROUPS = len(DIL_PAIRS)
DIL_HEADS = 4
DIL_BLOCK = 128

NSA_Q = NSA_HEADS * HEAD_DIM
NSA_KV = NSA_KV_GROUPS * HEAD_DIM
DIL_W = DIL_HEADS * HEAD_DIM

LANES = 128
SEL_KEY_TILE = 512
VMEM_LIMIT = 56 * 1024 * 1024

F32 = jnp.float32
BF16 = jnp.bfloat16
SCALE = HEAD_DIM ** -0.5
LOG2E = 1.4426950408889634


def _dot(a, b):
    return jnp.dot(a, b, preferred_element_type=F32)


def _dot_nt(a, b):
    return lax.dot_general(a, b, (((1,), (1,)), ((), ())), preferred_element_type=F32)


def _rmsnorm(x, g):
    return x * lax.rsqrt(jnp.mean(x * x, axis=-1, keepdims=True) + NORM_EPS) * g


def _params(sem):
    return pltpu.CompilerParams(dimension_semantics=sem, vmem_limit_bytes=VMEM_LIMIT)


def _ffn_kernel(*refs, final):
    if final:
        x_ref, g_ref, wu_ref, wv_ref, wo_ref, gf_ref, o_ref, h_scr, acc_scr = refs
    else:
        x_ref, g_ref, wu_ref, wv_ref, wo_ref, o_ref, h_scr, acc_scr = refs
    j = pl.program_id(1)

    @pl.when(j == 0)
    def _():
        h_scr[...] = _rmsnorm(x_ref[...], g_ref[...]).astype(BF16)
        acc_scr[...] = jnp.zeros_like(acc_scr)

    h = h_scr[...]
    u = _dot(h, wu_ref[...])
    v = _dot(h, wv_ref[...])
    a = (u * jax.nn.sigmoid(u)) * v
    acc_scr[...] += _dot(a.astype(BF16), wo_ref[...])

    @pl.when(j == pl.num_programs(1) - 1)
    def _():
        y = x_ref[...] + 0.5 * acc_scr[...]
        if final:
            y = _rmsnorm(y, gf_ref[...])
        o_ref[...] = y


def _ffn(x2, gain, w_in, w_out, final_gain=None, tm=1024, tf=256):
    m, d = x2.shape
    dff = w_out.shape[0]
    nf = dff // tf
    final = final_gain is not None
    in_specs = [
        pl.BlockSpec((tm, d), lambda i, j: (i, 0)),
        pl.BlockSpec((1, d), lambda i, j: (0, 0)),
        pl.BlockSpec((d, tf), lambda i, j: (0, j)),
        pl.BlockSpec((d, tf), lambda i, j: (0, j + nf)),
        pl.BlockSpec((tf, d), lambda i, j: (j, 0)),
    ]
    args = [x2, gain.reshape(1, d), w_in, w_in, w_out]
    if final:
        in_specs.append(pl.BlockSpec((1, d), lambda i, j: (0, 0)))
        args.append(final_gain.reshape(1, d))
    return pl.pallas_call(
        functools.partial(_ffn_kernel, final=final),
        grid=(m // tm, nf),
        in_specs=in_specs,
        out_specs=pl.BlockSpec((tm, d), lambda i, j: (i, 0)),
        out_shape=jax.ShapeDtypeStruct((m, d), F32),
        scratch_shapes=[pltpu.VMEM((tm, d), BF16), pltpu.VMEM((tm, d), F32)],
        compiler_params=_params(("parallel", "arbitrary")),
        name="ffn_final" if final else "ffn",
    )(*args)


def _rope_coeffs(pos_col, inv_lane):
    ang = pos_col.astype(F32) * inv_lane
    c, s = jnp.cos(ang), jnp.sin(ang)
    d = lax.broadcasted_iota(jnp.int32, ang.shape, 1) & (HEAD_DIM - 1)
    half = ROPE_DIM // 2
    coef_c = jnp.where(d < ROPE_DIM, c, 1.0)
    coef_hi = jnp.where(d < half, -s, 0.0)
    coef_lo = jnp.where((d >= half) & (d < ROPE_DIM), s, 0.0)
    return coef_c, coef_hi, coef_lo


def _proj_kernel(*refs, rope, segments):
    n_in = 5 if rope else 3
    x_ref, g_ref, w_ref = refs[:3]
    o_refs = refs[n_in:n_in + len(segments)]
    h = _rmsnorm(x_ref[...], g_ref[...]).astype(BF16)
    y = _dot(h, w_ref[...])
    if len(refs) == n_in + len(segments):
        o_refs[0][...] = y.astype(o_refs[0].dtype)
        return
    y_scr = refs[-1]
    if rope:
        coef_c, coef_hi, coef_lo = _rope_coeffs(refs[3][...], refs[4][...])
        half = ROPE_DIM // 2
        for t in range(y.shape[1] // LANES):
            yt = y[:, t * LANES:(t + 1) * LANES]
            hi = pltpu.roll(yt, LANES - half, axis=1)
            lo = pltpu.roll(yt, half, axis=1)
            y_scr[t] = yt * coef_c + hi * coef_hi + lo * coef_lo
    else:
        for t in range(y.shape[1] // LANES):
            y_scr[t] = y[:, t * LANES:(t + 1) * LANES]
    tm = y.shape[0]
    for o_ref, (c0, w, dil) in zip(o_refs, segments):
        for k in range(w // LANES):
            src = c0 // LANES + k
            for res in range(dil):
                rows = y_scr[src] if dil == 1 else y_scr[src, pl.ds(res, tm // dil, stride=dil), :]
                o_ref[:, res * w + k * LANES:res * w + (k + 1) * LANES] = rows.astype(o_ref.dtype)


def _proj(x2, gain, w, out_dtype, segments=None, pos=None, inv_lane=None, tm=512, name="proj"):
    m, d = x2.shape
    n = w.shape[1]
    rope = pos is not None
    segments = segments or ((0, n, 1),)
    staged = rope or len(segments) > 1 or segments[0][2] != 1
    in_specs = [
        pl.BlockSpec((tm, d), lambda i: (i, 0)),
        pl.BlockSpec((1, d), lambda i: (0, 0)),
        pl.BlockSpec((d, n), lambda i: (0, 0)),
    ]
    args = [x2, gain.reshape(1, d), w]
    if rope:
        in_specs += [pl.BlockSpec((tm, 1), lambda i: (i, 0)), pl.BlockSpec((1, LANES), lambda i: (0, 0))]
        args += [pos, inv_lane]
    outs = pl.pallas_call(
        functools.partial(_proj_kernel, rope=rope, segments=segments),
        grid=(m // tm,),
        in_specs=in_specs,
        out_specs=[pl.BlockSpec((tm // dil, dil * wd), lambda i: (i, 0)) for _, wd, dil in segments],
        out_shape=[jax.ShapeDtypeStruct((m // dil, dil * wd), out_dtype) for _, wd, dil in segments],
        scratch_shapes=[pltpu.VMEM((n // LANES, tm, LANES), F32)] if staged else [],
        compiler_params=_params(("parallel",)),
        name=name,
    )(*args)
    return outs


def _compress_kernel(seg_ref, pos_ref, w1_ref, w2_ref, o_ref):
    seg = seg_ref[0, 0, 0]
    a = _dot((seg + pos_ref[0, 0]).astype(BF16), w1_ref[0, 0])
    b = _dot((seg + pos_ref[0, 1]).astype(BF16), w1_ref[0, 1])
    n_seg = seg.shape[0]
    pre = a + pltpu.roll(b, n_seg - 1, axis=0)
    hid = pre * jax.nn.sigmoid(pre)
    o_ref[0, 0, 0] = _dot(hid.astype(BF16), w2_ref[0])


def _compress(seg, pos, w1, w2):
    _, b, g, n_seg, k = seg.shape
    hid = w1.shape[-1]
    dh = w2.shape[-1]
    return pl.pallas_call(
        _compress_kernel,
        grid=(2, b, g),
        in_specs=[
            pl.BlockSpec((1, 1, 1, n_seg, k), lambda c, i, j: (c, i, j, 0, 0)),
            pl.BlockSpec((1, 2, 1, k), lambda c, i, j: (c, 0, 0, 0)),
            pl.BlockSpec((1, 2, k, hid), lambda c, i, j: (c, 0, 0, 0)),
            pl.BlockSpec((1, hid, dh), lambda c, i, j: (c, 0, 0)),
        ],
        out_specs=pl.BlockSpec((1, 1, 1, n_seg, dh), lambda c, i, j: (c, i, j, 0, 0)),
        out_shape=jax.ShapeDtypeStruct((2, b, g, n_seg, dh), F32),
        compiler_params=_params(("parallel", "parallel", "parallel")),
        name="compress",
    )(seg, pos, w1, w2)


MASK_BIAS = NEG
M_INIT = -1e29


def _stack_heads_t(q, g):
    zeros = jnp.zeros((HEAD_DIM, q.shape[0]), F32)
    parts = []
    for h in range(NSA_HPG):
        hg = g * NSA_HPG + h
        tile_t = (q[:, (hg // 2) * LANES:(hg // 2 + 1) * LANES] * (SCALE * LOG2E)).T
        feat = tile_t[(hg % 2) * HEAD_DIM:(hg % 2 + 1) * HEAD_DIM]
        parts.append(jnp.concatenate([feat, zeros] if g == 0 else [zeros, feat], axis=0))
    return jnp.concatenate(parts, axis=1).astype(BF16)


def _softmax_t(s, bias, hg):
    sb = s + jnp.concatenate([bias] * hg, axis=1)
    m = jnp.maximum(jnp.max(sb, axis=0, keepdims=True), M_INIT)
    e = jnp.exp2(sb - m)
    d = jnp.sum(e, axis=0, keepdims=True)
    return e, 1.0 / jnp.where(d > 0, d, 1.0)


def _nsa_kernel(qp_ref, qr_ref, gate_ref, kc_ref, vct_ref, ks_ref, vst_ref, kw_ref, vwt_ref, cmapt_ref, et_ref,
                o_ref, cnt_scr, sa_scr, sb_scr, *, n_sel, win_keys):
    qb = pl.program_id(1)
    t0 = qb * Q_BLOCK
    nq = Q_BLOCK
    hg = NSA_HPG
    tk = SEL_KEY_TILE
    n_cmp_pad = kc_ref.shape[1]
    t_row = t0 + lax.broadcasted_iota(jnp.int32, (1, nq), 1)
    blk = lax.broadcasted_iota(jnp.int32, (LANES, nq), 0)
    sub8 = lax.broadcasted_iota(jnp.int32, (8, nq), 0)

    q_plain = qp_ref[0]
    q_rope = qr_ref[0].astype(F32)
    gates_t = jax.nn.sigmoid(gate_ref[0]).T

    cmp_end = lax.broadcasted_iota(jnp.int32, (n_cmp_pad, nq), 0) * CMP_STRIDE + (CMP_BLOCK - 1)
    bias_c = jnp.where(cmp_end <= t_row, 0.0, MASK_BIAS)
    cur = t_row >> 6
    forced = (blk == 0) | (blk == cur) | (blk == cur - 1)
    valid = blk <= cur
    n_blocks = (qb + 1) * (Q_BLOCK // SLC_BLOCK)

    win_start = pl.multiple_of(jnp.maximum(t0 - WIN, 0), Q_BLOCK)
    diff_w = t_row - (win_start + lax.broadcasted_iota(jnp.int32, (win_keys, nq), 0))
    bias_w = jnp.where((diff_w >= 0) & (diff_w < WIN), 0.0, MASK_BIAS)

    first_blk = qb * (Q_BLOCK // SLC_BLOCK)
    n_sweep = (t0 + tk - 1) // tk
    n_key_tiles = ks_ref.shape[1] // tk
    bias_own = jnp.where(lax.broadcasted_iota(jnp.int32, (nq, nq), 0) <= lax.broadcasted_iota(jnp.int32, (nq, nq), 1),
                         0.0, MASK_BIAS)

    for g in range(NSA_KV_GROUPS):
        qt_plain = _stack_heads_t(q_plain, g)
        qt_rope = _stack_heads_t(q_rope, g)
        feat = slice(g * HEAD_DIM, (g + 1) * HEAD_DIM)

        e_c, r_c = _softmax_t(_dot(kc_ref[0], qt_plain), bias_c, hg)
        p_c = e_c * r_c
        o_c = _dot(vct_ref[0], p_c.astype(BF16))
        p_sum = p_c[:, 0:nq]
        for h in range(1, hg):
            p_sum = p_sum + p_c[:, h * nq:(h + 1) * nq]
        imp = jnp.dot(cmapt_ref[...], p_sum, preferred_element_type=F32, precision=lax.Precision.HIGHEST)
        imp = jnp.where(forced, FORCE, jnp.where(valid, imp, -FORCE))

        cnt_scr[...] = jnp.zeros_like(cnt_scr)
        for ig in range(LANES // 8):
            @pl.when(ig * 8 < n_blocks)
            def _(ig=ig):
                parts = []
                for jv in range(LANES // 8):
                    x = imp[jv * 8:(jv + 1) * 8]
                    c = jnp.zeros((8, nq), jnp.int32)
                    for r in range(8):
                        row = imp[ig * 8 + r:ig * 8 + r + 1]
                        if jv > ig:
                            ahead = row >= x
                        elif jv < ig:
                            ahead = row > x
                        else:
                            ahead = (row > x) | ((row == x) & (sub8 > r))
                        c = c + ahead.astype(jnp.int32)
                    parts.append(c)
                cnt_scr[...] += jnp.concatenate(parts, axis=0)
        picked = (cnt_scr[...] < n_sel) & (imp > -0.5 * FORCE)
        bias_blk = jnp.where(picked & (blk < first_blk), 0.0, MASK_BIAS).astype(BF16)
        q_aug = jnp.concatenate([qt_rope, jnp.concatenate([bias_blk] * hg, axis=1)], axis=0)

        def absorb(s, vt, carry):
            m, acc = carry
            m_new = jnp.maximum(m, jnp.max(s, axis=0, keepdims=True))
            e = jnp.exp2(s - m_new)
            return m_new, jnp.exp2(m - m_new) * acc + _dot(vt, e.astype(BF16))

        def key_tile(kt):
            return pl.multiple_of(jnp.minimum(kt, n_key_tiles - 1) * tk, tk)

        def scores_into(scr, kt):
            k0 = key_tile(kt)
            scr[...] = _dot(jnp.concatenate([ks_ref[0, pl.ds(k0, tk), :], et_ref[pl.ds(k0, tk), :]], axis=1), q_aug)

        def sweep_pair(i, carry):
            scores_into(sb_scr, 2 * i + 1)
            carry = absorb(sa_scr[...], vst_ref[0, g, :, pl.ds(key_tile(2 * i), tk)], carry)
            scores_into(sa_scr, 2 * i + 2)
            return absorb(sb_scr[...], vst_ref[0, g, :, pl.ds(key_tile(2 * i + 1), tk)], carry)

        init = (jnp.full((1, hg * nq), M_INIT, F32), jnp.zeros((vst_ref.shape[2], hg * nq), F32))
        scores_into(sa_scr, 0)
        carry = lax.fori_loop(0, (n_sweep + 1) // 2, sweep_pair, init)
        own = pl.ds(pl.multiple_of(t0, nq), nq)
        s_own = _dot(ks_ref[0, own, :], qt_rope) + jnp.concatenate([bias_own] * hg, axis=1)
        _, acc_s = absorb(s_own, vst_ref[0, g, :, own], carry)
        o_s = acc_s[0:HEAD_DIM] * (1.0 / acc_s[HEAD_DIM:HEAD_DIM + 1])

        s_w = _dot(kw_ref[0, pl.ds(win_start, win_keys), :], qt_rope) + jnp.concatenate([bias_w] * hg, axis=1)
        e_w = jnp.exp2(s_w - jnp.max(s_w, axis=0, keepdims=True))
        acc_w = _dot(vwt_ref[0, g, :, pl.ds(win_start, win_keys)], e_w.astype(BF16))
        o_w = acc_w[0:HEAD_DIM] * (1.0 / acc_w[HEAD_DIM:HEAD_DIM + 1])

        mixes = []
        for h in range(hg):
            row = g * hg * 3 + h * 3
            cols = slice(h * nq, (h + 1) * nq)
            mixes.append(gates_t[row:row + 1] * o_c[feat, cols] + gates_t[row + 1:row + 2] * o_s[:, cols]
                         + gates_t[row + 2:row + 3] * o_w[:, cols])
        for hp in range(hg // 2):
            c = (g * hg) // 2 + hp
            pair = jnp.concatenate([mixes[2 * hp], mixes[2 * hp + 1]], axis=0)
            o_ref[0, :, c * LANES:(c + 1) * LANES] = pair.T.astype(o_ref.dtype)


def _nsa(pa3, r3, kc, vct, vst, vwt, cmapt, expand, *, qp_blk, gate_blk, ks_blk, kw_blk):
    b, t, _ = pa3.shape
    n_cmp_pad = kc.shape[1]
    n_sel = min(SLC_TOPN, t // SLC_BLOCK)
    win_keys = min(WIN + Q_BLOCK, t)
    keys = lambda blk: pl.BlockSpec((1, t, LANES), lambda i, j, blk=blk: (i, 0, blk))
    vals_t = pl.BlockSpec((1,) + vst.shape[1:], lambda i, j: (i, 0, 0, 0))
    return pl.pallas_call(
        functools.partial(_nsa_kernel, n_sel=n_sel, win_keys=win_keys),
        grid=(b, t // Q_BLOCK),
        in_specs=[
            pl.BlockSpec((1, Q_BLOCK, NSA_Q), lambda i, j: (i, j, qp_blk)),
            pl.BlockSpec((1, Q_BLOCK, NSA_Q), lambda i, j: (i, j, 0)),
            pl.BlockSpec((1, Q_BLOCK, LANES), lambda i, j: (i, j, gate_blk)),
            pl.BlockSpec((1, n_cmp_pad, LANES), lambda i, j: (i, 0, 0)),
            pl.BlockSpec((1, LANES, n_cmp_pad), lambda i, j: (i, 0, 0)),
            keys(ks_blk), vals_t, keys(kw_blk), vals_t,
            pl.BlockSpec((LANES, n_cmp_pad), lambda i, j: (0, 0)),
            pl.BlockSpec((t, LANES), lambda i, j: (0, 0)),
        ],
        out_specs=pl.BlockSpec((1, Q_BLOCK, NSA_Q), lambda i, j: (i, j, 0)),
        out_shape=jax.ShapeDtypeStruct((b, t, NSA_Q), BF16),
        scratch_shapes=[pltpu.VMEM((LANES, Q_BLOCK), jnp.int32),
                        pltpu.VMEM((SEL_KEY_TILE, NSA_HPG * Q_BLOCK), F32), pltpu.VMEM((SEL_KEY_TILE, NSA_HPG * Q_BLOCK), F32)],
        compiler_params=_params(("parallel", "arbitrary")),
        name="nsa",
    )(pa3, r3, pa3, kc, vct, r3, vst, r3, vwt, cmapt, expand)


def _values_t(v):
    b, t, _ = v.shape
    vt = v.reshape(b, t, NSA_KV_GROUPS, HEAD_DIM).transpose(0, 2, 3, 1)
    ones = jnp.ones((b, NSA_KV_GROUPS, 1, t), v.dtype)
    pad = jnp.zeros((b, NSA_KV_GROUPS, 15, t), v.dtype)
    return jnp.concatenate([vt, ones, pad], axis=2)


def _slc_map_t(n_cmp_pad, n_slc):
    i = np.arange(n_cmp_pad)[None, :]
    j = np.arange(LANES)[:, None]
    lo = np.maximum(i * CMP_STRIDE, j * SLC_BLOCK)
    hi = np.minimum(i * CMP_STRIDE + CMP_BLOCK, (j + 1) * SLC_BLOCK)
    m = (np.maximum(hi - lo, 0) / CMP_BLOCK).astype(np.float32)
    m[n_slc:, :] = 0.0
    m[:, n_cmp_pad - 1:] = 0.0
    return m


def _dil_kernel(q_ref, kp_ref, kc_ref, vp_ref, vc_ref, o_ref, lse_ref, *, span):
    ub = pl.program_id(2)
    nq = DIL_BLOCK
    lane = lax.broadcasted_iota(jnp.int32, (nq, LANES), 1)
    row = lax.broadcasted_iota(jnp.int32, (2 * nq, 2 * nq), 0) & (nq - 1)
    col = lax.broadcasted_iota(jnp.int32, (2 * nq, 2 * nq), 1)
    diff = row + nq - col
    band = (diff >= 0) & (diff <= span)
    band_first = band & ((col >= nq) | (ub > 0))
    for j in range(q_ref.shape[0] // nq):
        rows = slice(j * nq, (j + 1) * nq)
        before = slice((j - 1) * nq, j * nq)
        mask = band_first if j == 0 else band
        for hp in range(DIL_HEADS // 2):
            cols = slice(hp * LANES, (hp + 1) * LANES)
            qt = q_ref[rows, cols] * jnp.asarray(SCALE, BF16)
            zero = jnp.zeros_like(qt)
            qs = jnp.concatenate([jnp.where(lane < HEAD_DIM, qt, zero), jnp.where(lane >= HEAD_DIM, qt, zero)], axis=0)
            k = jnp.concatenate([kp_ref[:, cols] if j == 0 else kc_ref[before, cols], kc_ref[rows, cols]], axis=0)
            v = jnp.concatenate([vp_ref[:, cols] if j == 0 else vc_ref[before, cols], vc_ref[rows, cols]], axis=0)
            s = _dot_nt(qs, k)
            m = jnp.max(jnp.where(mask, s, NEG), axis=-1, keepdims=True)
            e = jnp.exp(jnp.where(mask, s - m, NEG))
            d = jnp.sum(e, axis=-1, keepdims=True)
            p = e / jnp.where(d > 0, d, 1.0)
            o = _dot(p.astype(BF16), v)
            lse = m + jnp.log(d)
            o_ref[rows, cols] = jnp.where(lane < HEAD_DIM, o[:nq], o[nq:]).astype(o_ref.dtype)
            lse_ref[rows, cols] = jnp.where(lane < HEAD_DIM, lse[:nq], lse[nq:])


def _dilated(qd, kd, vd, b, dil, span):
    rows = qd.shape[0]
    ln = rows // b
    ub = min(ln, 4 * DIL_BLOCK)
    nub = ln // ub
    per = ub // DIL_BLOCK
    cur = pl.BlockSpec((ub, DIL_W), lambda i, r, u: (i * nub + u, r))
    prev = pl.BlockSpec((DIL_BLOCK, DIL_W), lambda i, r, u: (jnp.maximum((i * nub + u) * per - 1, 0), r))
    o, lse = pl.pallas_call(
        functools.partial(_dil_kernel, span=span),
        grid=(b, dil, nub),
        in_specs=[cur, prev, cur, prev, cur],
        out_specs=[cur, cur],
        out_shape=[jax.ShapeDtypeStruct(qd.shape, F32), jax.ShapeDtypeStruct(qd.shape, F32)],
        compiler_params=_params(("parallel", "parallel", "arbitrary")),
        name=f"dilated_{dil}",
    )(qd, kd, kd, vd, vd)
    return o, lse


def _out_kernel(*refs, dils):
    ng = len(dils)
    x_ref, yn_ref = refs[:2]
    o_refs, l_refs = refs[2:2 + ng], refs[2 + ng:2 + 2 * ng]
    g0_ref, g1_ref, wn_ref, wd_ref, wo_ref, o_ref = refs[2 + 2 * ng:8 + 2 * ng]
    scrs = list(refs[8 + 2 * ng:])

    def natural(ref, dil):
        if dil == 1:
            return ref[...]
        scr = scrs.pop(0)
        n = scr.shape[1] // dil
        for res in range(dil):
            for k in range(DIL_W // LANES):
                scr[k, pl.ds(res, n, stride=dil), :] = ref[:, res * DIL_W + k * LANES:res * DIL_W + (k + 1) * LANES]
        return jnp.concatenate([scr[k] for k in range(DIL_W // LANES)], axis=1)

    outs = [natural(r, d) for r, d in zip(o_refs, dils)]
    lses = [natural(r, d) for r, d in zip(l_refs, dils)]
    m = functools.reduce(jnp.maximum, lses)
    ws = [jnp.exp(l - m) for l in lses]
    den = functools.reduce(lambda a, c: a + c, ws)
    y_dil = functools.reduce(lambda a, c: a + c, [(w / den) * o for w, o in zip(ws, outs)])
    merged = (jax.nn.sigmoid(g0_ref[...]) * _dot(yn_ref[...], wn_ref[...])
              + jax.nn.sigmoid(g1_ref[...]) * _dot(y_dil.astype(BF16), wd_ref[...]))
    o_ref[...] = x_ref[...] + _dot(merged.astype(BF16), wo_ref[...])


def _out_proj(x2, y_nsa, outs, lses, dils, pa, w_bn, w_bd, w_o, tm=512):
    m, d = x2.shape
    row = lambda w: pl.BlockSpec((tm, w), lambda i: (i, 0))
    dil_row = lambda dil: pl.BlockSpec((tm // dil, dil * DIL_W), lambda i: (i, 0))
    whole = lambda a: pl.BlockSpec(a.shape, lambda i: (0, 0))
    n_staged = 2 * sum(dil != 1 for dil in dils)
    return pl.pallas_call(
        functools.partial(_out_kernel, dils=tuple(dils)),
        grid=(m // tm,),
        in_specs=[row(d), row(NSA_Q)] + [dil_row(dil) for dil in dils] * 2
        + [pl.BlockSpec((tm, d), lambda i: (i, 0)), pl.BlockSpec((tm, d), lambda i: (i, 1))]
        + [whole(w_bn), whole(w_bd), whole(w_o)],
        out_specs=row(d),
        out_shape=jax.ShapeDtypeStruct((m, d), F32),
        scratch_shapes=[pltpu.VMEM((DIL_W // LANES, tm, LANES), F32)] * n_staged,
        compiler_params=_params(("parallel",)),
        name="out_proj",
    )(x2, y_nsa, *outs, *lses, pa, pa, w_bn, w_bd, w_o)


def _mixer(x2, pos_col, inv_lane, b, t, norm_mix, w_in, cmp_pos, cmp_w1, cmp_w2, w_branch_nsa, w_branch_dil, w_out):
    d = x2.shape[1]
    dh = HEAD_DIM
    o_q, o_kv, o_gn = 0, NSA_Q, NSA_Q + 6 * NSA_KV
    o_d = o_gn + 3 * NSA_HEADS
    o_gm = o_d + 3 * DIL_GROUPS * DIL_W
    kv = lambda i: w_in[:, o_kv + i * NSA_KV:o_kv + (i + 1) * NSA_KV]
    dd = lambda i: w_in[:, o_d + i * DIL_GROUPS * DIL_W:o_d + (i + 1) * DIL_GROUPS * DIL_W]
    w_q = w_in[:, o_q:o_q + NSA_Q]
    gn_pad = jnp.zeros((d, 2 * LANES - 3 * NSA_HEADS), w_in.dtype)
    w_a = jnp.concatenate([w_in[:, o_gm:o_gm + 2 * d], w_q, kv(0), kv(1), w_in[:, o_gn:o_gn + 3 * NSA_HEADS], gn_pad],
                          axis=1).astype(BF16)
    w_b = jnp.concatenate([w_q, kv(2), kv(4), dd(0), dd(1)], axis=1).astype(BF16)
    w_c = jnp.concatenate([kv(3), kv(5), dd(2)], axis=1).astype(BF16)
    dils = [dil for _, dil in DIL_PAIRS]
    n_r = NSA_Q + 2 * NSA_KV
    seg_b = ((0, n_r, 1),) + tuple((n_r + i * DIL_W, DIL_W, dil) for i, dil in enumerate(dils)) \
        + tuple((n_r + (DIL_GROUPS + i) * DIL_W, DIL_W, dil) for i, dil in enumerate(dils))
    seg_c = ((0, 2 * NSA_KV, 1),) + tuple((2 * NSA_KV + i * DIL_W, DIL_W, dil) for i, dil in enumerate(dils))

    pa, = _proj(x2, norm_mix, w_a, F32, name="proj_f32")
    rb, *qk_d = _proj(x2, norm_mix, w_b, BF16, segments=seg_b, pos=pos_col, inv_lane=inv_lane, name="proj_rope")
    vc_, *v_d = _proj(x2, norm_mix, w_c, BF16, segments=seg_c, name="proj_bf16")

    n_seg = t // CMP_STRIDE
    src0 = 2 * d + NSA_Q
    seg = pa[:, src0:src0 + 2 * NSA_KV].reshape(b, t, 2, NSA_KV_GROUPS, dh)
    seg = seg.transpose(2, 0, 3, 1, 4).reshape(2, b, NSA_KV_GROUPS, n_seg, CMP_STRIDE * dh)
    pos = cmp_pos.reshape(2, 2, 1, CMP_STRIDE * dh)
    w1 = cmp_w1.reshape(2, 2, CMP_STRIDE * dh, -1).astype(BF16)
    cmp = _compress(seg, pos, w1, cmp_w2.astype(BF16))
    cmp = cmp.transpose(0, 1, 3, 2, 4).reshape(2, b, n_seg, NSA_KV).astype(BF16)

    cmapt = jnp.asarray(_slc_map_t(n_seg, t // SLC_BLOCK))
    pa3 = pa.reshape(b, t, -1)
    r3 = rb.reshape(b, t, -1)
    v3 = vc_.reshape(b, t, -1)
    vst = _values_t(v3[:, :, 0:NSA_KV])
    vwt = _values_t(v3[:, :, NSA_KV:2 * NSA_KV])
    expand = jnp.asarray(np.arange(t)[:, None] // SLC_BLOCK == np.arange(LANES)[None, :], BF16)
    y_nsa = _nsa(pa3, r3, cmp[0], cmp[1].transpose(0, 2, 1), vst, vwt, cmapt, expand,
                 qp_blk=2 * d // NSA_Q, gate_blk=(2 * d + NSA_Q + 2 * NSA_KV) // LANES,
                 ks_blk=NSA_Q // LANES, kw_blk=NSA_Q // LANES + 1)

    outs, lses = [], []
    for gidx, (window, dil) in enumerate(DIL_PAIRS):
        o, lse = _dilated(qk_d[gidx], qk_d[DIL_GROUPS + gidx], v_d[gidx], b, dil, window // dil)
        outs.append(o)
        lses.append(lse)

    return _out_proj(x2, y_nsa.reshape(b * t, NSA_Q), outs, lses, dils, pa,
                     w_branch_nsa.astype(BF16), w_branch_dil.astype(BF16), w_out.astype(BF16))


def kernel(x, positions, norm_ffn1, ffn1_w_in, ffn1_w_out, norm_mix, w_in, cmp_pos, cmp_w1, cmp_w2,
           w_branch_nsa, w_branch_dil, w_out, norm_ffn2, ffn2_w_in, ffn2_w_out, norm_final):
    b, t, d = x.shape
    depth = norm_ffn1.shape[0]
    x2 = x.reshape(b * t, d)
    pos_col = positions.reshape(b * t, 1).astype(jnp.int32)
    inv = jnp.power(ROPE_THETA, -jnp.arange(0, ROPE_DIM, 2, dtype=F32) / ROPE_DIM)
    inv_head = jnp.concatenate([inv, inv, jnp.zeros((HEAD_DIM - ROPE_DIM,), F32)])
    inv_lane = jnp.tile(inv_head, LANES // HEAD_DIM).reshape(1, LANES)
    for l in range(depth):
        last = l == depth - 1
        x2 = _ffn(x2, norm_ffn1[l], ffn1_w_in[l].astype(BF16), ffn1_w_out[l].astype(BF16))
        x2 = _mixer(x2, pos_col, inv_lane, b, t, norm_mix[l], w_in[l], cmp_pos[l], cmp_w1[l], cmp_w2[l],
                    w_branch_nsa[l], w_branch_dil[l], w_out[l])
        x2 = _ffn(x2, norm_ffn2[l], ffn2_w_in[l].astype(BF16), ffn2_w_out[l].astype(BF16),
                  final_gain=norm_final if last else None)
    if depth == 0:
        raise ValueError("depth must be positive")
    return x2.reshape(b, t, d)
```

```python
import functools

import numpy as np
import jax
import jax.numpy as jnp
from jax import lax
from jax.experimental import pallas as pl
from jax.experimental.pallas import tpu as pltpu

HEAD_DIM = 64
ROPE_DIM = HEAD_DIM // 4
ROPE_THETA = 500000.0
NORM_EPS = 1e-6

NSA_HEADS = 8
NSA_KV_GROUPS = 2
NSA_HPG = NSA_HEADS // NSA_KV_GROUPS
CMP_BLOCK = 32
CMP_STRIDE = 16
SLC_BLOCK = 64
SLC_TOPN = 16
WIN = 512
Q_BLOCK = 128
FORCE = 1e9
NEG = -1e30

DIL_PAIRS = ((128, 1), (512, 4), (2048, 16))
DIL_GROUPS = len(DIL_PAIRS)
DIL_HEADS = 4
DIL_BLOCK = 128

NSA_Q = NSA_HEADS * HEAD_DIM
NSA_KV = NSA_KV_GROUPS * HEAD_DIM
DIL_W = DIL_HEADS * HEAD_DIM

LANES = 128
SEL_KEY_TILE = 512
VMEM_LIMIT = 56 * 1024 * 1024

F32 = jnp.float32
BF16 = jnp.bfloat16
SCALE = HEAD_DIM ** -0.5
LOG2E = 1.4426950408889634


def _dot(a, b):
    return jnp.dot(a, b, preferred_element_type=F32)


def _dot_nt(a, b):
    return lax.dot_general(a, b, (((1,), (1,)), ((), ())), preferred_element_type=F32)


def _rmsnorm(x, g):
    return x * lax.rsqrt(jnp.mean(x * x, axis=-1, keepdims=True) + NORM_EPS) * g


def _params(sem):
    return pltpu.CompilerParams(dimension_semantics=sem, vmem_limit_bytes=VMEM_LIMIT)


def _ffn_kernel(*refs, final):
    if final:
        x_ref, g_ref, wu_ref, wv_ref, wo_ref, gf_ref, o_ref, h_scr, acc_scr = refs
    else:
        x_ref, g_ref, wu_ref, wv_ref, wo_ref, o_ref, h_scr, acc_scr = refs
    j = pl.program_id(1)

    @pl.when(j == 0)
    def _():
        h_scr[...] = _rmsnorm(x_ref[...], g_ref[...]).astype(BF16)
        acc_scr[...] = jnp.zeros_like(acc_scr)

    h = h_scr[...]
    u = _dot(h, wu_ref[...])
    v = _dot(h, wv_ref[...])
    a = (u * jax.nn.sigmoid(u)) * v
    acc_scr[...] += _dot(a.astype(BF16), wo_ref[...])

    @pl.when(j == pl.num_programs(1) - 1)
    def _():
        y = x_ref[...] + 0.5 * acc_scr[...]
        if final:
            y = _rmsnorm(y, gf_ref[...])
        o_ref[...] = y


def _ffn(x2, gain, w_in, w_out, final_gain=None, tm=1024, tf=256):
    m, d = x2.shape
    dff = w_out.shape[0]
    nf = dff // tf
    final = final_gain is not None
    in_specs = [
        pl.BlockSpec((tm, d), lambda i, j: (i, 0)),
        pl.BlockSpec((1, d), lambda i, j: (0, 0)),
        pl.BlockSpec((d, tf), lambda i, j: (0, j)),
        pl.BlockSpec((d, tf), lambda i, j: (0, j + nf)),
        pl.BlockSpec((tf, d), lambda i, j: (j, 0)),
    ]
    args = [x2, gain.reshape(1, d), w_in, w_in, w_out]
    if final:
        in_specs.append(pl.BlockSpec((1, d), lambda i, j: (0, 0)))
        args.append(final_gain.reshape(1, d))
    return pl.pallas_call(
        functools.partial(_ffn_kernel, final=final),
        grid=(m // tm, nf),
        in_specs=in_specs,
        out_specs=pl.BlockSpec((tm, d), lambda i, j: (i, 0)),
        out_shape=jax.ShapeDtypeStruct((m, d), F32),
        scratch_shapes=[pltpu.VMEM((tm, d), BF16), pltpu.VMEM((tm, d), F32)],
        compiler_params=_params(("parallel", "arbitrary")),
        name="ffn_final" if final else "ffn",
    )(*args)


def _rope_coeffs(pos_col, inv_lane):
    ang = pos_col.astype(F32) * inv_lane
    c, s = jnp.cos(ang), jnp.sin(ang)
    d = lax.broadcasted_iota(jnp.int32, ang.shape, 1) & (HEAD_DIM - 1)
    half = ROPE_DIM // 2
    coef_c = jnp.where(d < ROPE_DIM, c, 1.0)
    coef_hi = jnp.where(d < half, -s, 0.0)
    coef_lo = jnp.where((d >= half) & (d < ROPE_DIM), s, 0.0)
    return coef_c, coef_hi, coef_lo


def _proj_kernel(*refs, rope, segments):
    n_in = 5 if rope else 3
    x_ref, g_ref, w_ref = refs[:3]
    o_refs = refs[n_in:n_in + len(segments)]
    h = _rmsnorm(x_ref[...], g_ref[...]).astype(BF16)
    y = _dot(h, w_ref[...])
    if len(refs) == n_in + len(segments):
        o_refs[0][...] = y.astype(o_refs[0].dtype)
        return
    y_scr = refs[-1]
    if rope:
        coef_c, coef_hi, coef_lo = _rope_coeffs(refs[3][...], refs[4][...])
        half = ROPE_DIM // 2
        for t in range(y.shape[1] // LANES):
            yt = y[:, t * LANES:(t + 1) * LANES]
            hi = pltpu.roll(yt, LANES - half, axis=1)
            lo = pltpu.roll(yt, half, axis=1)
            y_scr[t] = yt * coef_c + hi * coef_hi + lo * coef_lo
    else:
        for t in range(y.shape[1] // LANES):
            y_scr[t] = y[:, t * LANES:(t + 1) * LANES]
    tm = y.shape[0]
    for o_ref, (c0, w, dil) in zip(o_refs, segments):
        for k in range(w // LANES):
            src = c0 // LANES + k
            for res in range(dil):
                rows = y_scr[src] if dil == 1 else y_scr[src, pl.ds(res, tm // dil, stride=dil), :]
                o_ref[:, res * w + k * LANES:res * w + (k + 1) * LANES] = rows.astype(o_ref.dtype)


def _proj(x2, gain, w, out_dtype, segments=None, pos=None, inv_lane=None, tm=512, name="proj"):
    m, d = x2.shape
    n = w.shape[1]
    rope = pos is not None
    segments = segments or ((0, n, 1),)
    staged = rope or len(segments) > 1 or segments[0][2] != 1
    in_specs = [
        pl.BlockSpec((tm, d), lambda i: (i, 0)),
        pl.BlockSpec((1, d), lambda i: (0, 0)),
        pl.BlockSpec((d, n), lambda i: (0, 0)),
    ]
    args = [x2, gain.reshape(1, d), w]
    if rope:
        in_specs += [pl.BlockSpec((tm, 1), lambda i: (i, 0)), pl.BlockSpec((1, LANES), lambda i: (0, 0))]
        args += [pos, inv_lane]
    outs = pl.pallas_call(
        functools.partial(_proj_kernel, rope=rope, segments=segments),
        grid=(m // tm,),
        in_specs=in_specs,
        out_specs=[pl.BlockSpec((tm // dil, dil * wd), lambda i: (i, 0)) for _, wd, dil in segments],
        out_shape=[jax.ShapeDtypeStruct((m // dil, dil * wd), out_dtype) for _, wd, dil in segments],
        scratch_shapes=[pltpu.VMEM((n // LANES, tm, LANES), F32)] if staged else [],
        compiler_params=_params(("parallel",)),
        name=name,
    )(*args)
    return outs


def _compress_kernel(seg_ref, pos_ref, w1_ref, w2_ref, o_ref):
    seg = seg_ref[0, 0, 0]
    a = _dot((seg + pos_ref[0, 0]).astype(BF16), w1_ref[0, 0])
    b = _dot((seg + pos_ref[0, 1]).astype(BF16), w1_ref[0, 1])
    n_seg = seg.shape[0]
    pre = a + pltpu.roll(b, n_seg - 1, axis=0)
    hid = pre * jax.nn.sigmoid(pre)
    o_ref[0, 0, 0] = _dot(hid.astype(BF16), w2_ref[0])


def _compress(seg, pos, w1, w2):
    _, b, g, n_seg, k = seg.shape
    hid = w1.shape[-1]
    dh = w2.shape[-1]
    return pl.pallas_call(
        _compress_kernel,
        grid=(2, b, g),
        in_specs=[
            pl.BlockSpec((1, 1, 1, n_seg, k), lambda c, i, j: (c, i, j, 0, 0)),
            pl.BlockSpec((1, 2, 1, k), lambda c, i, j: (c, 0, 0, 0)),
            pl.BlockSpec((1, 2, k, hid), lambda c, i, j: (c, 0, 0, 0)),
            pl.BlockSpec((1, hid, dh), lambda c, i, j: (c, 0, 0)),
        ],
        out_specs=pl.BlockSpec((1, 1, 1, n_seg, dh), lambda c, i, j: (c, i, j, 0, 0)),
        out_shape=jax.ShapeDtypeStruct((2, b, g, n_seg, dh), F32),
        compiler_params=_params(("parallel", "parallel", "parallel")),
        name="compress",
    )(seg, pos, w1, w2)


MASK_BIAS = NEG
M_INIT = -1e29


def _stack_heads_t(q, g):
    zeros = jnp.zeros((HEAD_DIM, q.shape[0]), F32)
    parts = []
    for h in range(NSA_HPG):
        hg = g * NSA_HPG + h
        tile_t = (q[:, (hg // 2) * LANES:(hg // 2 + 1) * LANES] * (SCALE * LOG2E)).T
        feat = tile_t[(hg % 2) * HEAD_DIM:(hg % 2 + 1) * HEAD_DIM]
        parts.append(jnp.concatenate([feat, zeros] if g == 0 else [zeros, feat], axis=0))
    return jnp.concatenate(parts, axis=1).astype(BF16)


def _softmax_t(s, bias, hg):
    sb = s + jnp.concatenate([bias] * hg, axis=1)
    m = jnp.maximum(jnp.max(sb, axis=0, keepdims=True), M_INIT)
    e = jnp.exp2(sb - m)
    d = jnp.sum(e, axis=0, keepdims=True)
    return e, 1.0 / jnp.where(d > 0, d, 1.0)


def _nsa_kernel(qp_ref, qr_ref, gate_ref, kc_ref, vct_ref, ks_ref, vst_ref, kw_ref, vwt_ref, et_ref,
                o_ref, cnt_scr, sa_scr, sb_scr, oc_scr, psum_scr, *, n_sel, win_keys):
    qb = pl.program_id(1)
    t0 = qb * Q_BLOCK
    nq = Q_BLOCK
    hg = NSA_HPG
    tk = SEL_KEY_TILE
    n_cmp_pad = kc_ref.shape[1]
    t_row = t0 + lax.broadcasted_iota(jnp.int32, (1, nq), 1)
    blk = lax.broadcasted_iota(jnp.int32, (LANES, nq), 0)
    sub8 = lax.broadcasted_iota(jnp.int32, (8, nq), 0)

    q_plain = qp_ref[0]
    q_rope = qr_ref[0].astype(F32)
    gates_t = jax.nn.sigmoid(gate_ref[0]).T

    cmp_chunk = min(LANES, n_cmp_pad)
    n_cmp_rows = (qb + 1) * (Q_BLOCK // CMP_STRIDE)
    n_cmp_chunks = jnp.minimum((n_cmp_rows + cmp_chunk - 1) // cmp_chunk, n_cmp_pad // cmp_chunk)
    cur = t_row >> 6
    forced = (blk == 0) | (blk == cur) | (blk == cur - 1)
    valid = blk <= cur
    n_blocks = (qb + 1) * (Q_BLOCK // SLC_BLOCK)

    win_start = pl.multiple_of(jnp.maximum(t0 - WIN, 0), Q_BLOCK)
    diff_w = t_row - (win_start + lax.broadcasted_iota(jnp.int32, (win_keys, nq), 0))
    bias_w = jnp.where((diff_w >= 0) & (diff_w < WIN), 0.0, MASK_BIAS)

    first_blk = qb * (Q_BLOCK // SLC_BLOCK)
    n_sweep = (t0 + tk - 1) // tk
    n_key_tiles = ks_ref.shape[1] // tk
    bias_own = jnp.where(lax.broadcasted_iota(jnp.int32, (nq, nq), 0) <= lax.broadcasted_iota(jnp.int32, (nq, nq), 1),
                         0.0, MASK_BIAS)

    for g in range(NSA_KV_GROUPS):
        qt_plain = _stack_heads_t(q_plain, g)
        qt_rope = _stack_heads_t(q_rope, g)
        feat = slice(g * HEAD_DIM, (g + 1) * HEAD_DIM)

        for c in range(1, n_cmp_pad // cmp_chunk + 1):
            @pl.when(n_cmp_chunks == c)
            def _(c=c):
                rows = c * cmp_chunk
                cmp_end = lax.broadcasted_iota(jnp.int32, (rows, nq), 0) * CMP_STRIDE + (CMP_BLOCK - 1)
                bias_c = jnp.where(cmp_end <= t_row, 0.0, MASK_BIAS)
                e_c, r_c = _softmax_t(_dot(kc_ref[0, 0:rows, :], qt_plain), bias_c, hg)
                p_c = e_c * r_c
                oc_scr[...] = _dot(vct_ref[0, :, 0:rows], p_c.astype(BF16))
                p_sum = p_c[:, 0:nq]
                for h in range(1, hg):
                    p_sum = p_sum + p_c[:, h * nq:(h + 1) * nq]
                psum_scr[0:rows, :] = p_sum
                if rows < psum_scr.shape[0]:
                    psum_scr[rows:, :] = jnp.zeros((psum_scr.shape[0] - rows, nq), F32)
        o_c = oc_scr[...]
        per = SLC_BLOCK // CMP_STRIDE
        parts = [psum_scr[pl.ds(k, LANES, stride=per), :] for k in range(per)]
        tail_prev = jnp.where(blk == 0, 0.0, pltpu.roll(parts[per - 1], 1, axis=0))
        imp = parts[0] + parts[1] + parts[2] + 0.5 * parts[3] + 0.5 * tail_prev
        imp = jnp.where(forced, FORCE, jnp.where(valid, imp, -FORCE))

        cnt_scr[...] = jnp.zeros_like(cnt_scr)
        quarter = LANES // 4
        for ig in range(LANES // 8):
            @pl.when((ig * 8 < n_blocks) & (n_blocks > n_sel))
            def _(ig=ig):
                for jq in range(4):
                    @pl.when(jq * quarter < n_blocks)
                    def _(jq=jq):
                        parts = []
                        for jv in range(jq * quarter // 8, (jq + 1) * quarter // 8):
                            x = imp[jv * 8:(jv + 1) * 8]
                            c = jnp.zeros((8, nq), jnp.int32)
                            for r in range(8):
                                row = imp[ig * 8 + r:ig * 8 + r + 1]
                                if jv > ig:
                                    ahead = row >= x
                                elif jv < ig:
                                    ahead = row > x
                                else:
                                    ahead = (row > x) | ((row == x) & (sub8 > r))
                                c = c + ahead.astype(jnp.int32)
                            parts.append(c)
                        cnt_scr[jq * quarter:(jq + 1) * quarter, :] += jnp.concatenate(parts, axis=0)
        picked = (cnt_scr[...] < n_sel) & (imp > -0.5 * FORCE)
        bias_blk = jnp.where(picked & (blk < first_blk), 0.0, MASK_BIAS).astype(BF16)
        q_aug = jnp.concatenate([qt_rope, jnp.concatenate([bias_blk] * hg, axis=1)], axis=0)

        def absorb(s, vt, carry):
            m, acc = carry
            m_new = jnp.maximum(m, jnp.max(s, axis=0, keepdims=True))
            e = jnp.exp2(s - m_new)
            return m_new, jnp.exp2(m - m_new) * acc + _dot(vt, e.astype(BF16))

        def key_tile(kt):
            return pl.multiple_of(jnp.minimum(kt, n_key_tiles - 1) * tk, tk)

        def scores_into(scr, kt):
            k0 = key_tile(kt)
            scr[...] = _dot(jnp.concatenate([ks_ref[0, pl.ds(k0, tk), :], et_ref[pl.ds(k0, tk), :]], axis=1), q_aug)

        def sweep_pair(i, carry):
            scores_into(sb_scr, 2 * i + 1)
            carry = absorb(sa_scr[...], vst_ref[0, g, :, pl.ds(key_tile(2 * i), tk)], carry)
            scores_into(sa_scr, 2 * i + 2)
            return absorb(sb_scr[...], vst_ref[0, g, :, pl.ds(key_tile(2 * i + 1), tk)], carry)

        init = (jnp.full((1, hg * nq), M_INIT, F32), jnp.zeros((vst_ref.shape[2], hg * nq), F32))
        scores_into(sa_scr, 0)
        carry = lax.fori_loop(0, (n_sweep + 1) // 2, sweep_pair, init)
        own = pl.ds(pl.multiple_of(t0, nq), nq)
        s_own = _dot(ks_ref[0, own, :], qt_rope) + jnp.concatenate([bias_own] * hg, axis=1)
        _, acc_s = absorb(s_own, vst_ref[0, g, :, own], carry)
        o_s = acc_s[0:HEAD_DIM] * (1.0 / acc_s[HEAD_DIM:HEAD_DIM + 1])

        s_w = _dot(kw_ref[0, pl.ds(win_start, win_keys), :], qt_rope) + jnp.concatenate([bias_w] * hg, axis=1)
        e_w = jnp.exp2(s_w - jnp.max(s_w, axis=0, keepdims=True))
        acc_w = _dot(vwt_ref[0, g, :, pl.ds(win_start, win_keys)], e_w.astype(BF16))
        o_w = acc_w[0:HEAD_DIM] * (1.0 / acc_w[HEAD_DIM:HEAD_DIM + 1])

        mixes = []
        for h in range(hg):
            row = g * hg * 3 + h * 3
            cols = slice(h * nq, (h + 1) * nq)
            mixes.append(gates_t[row:row + 1] * o_c[feat, cols] + gates_t[row + 1:row + 2] * o_s[:, cols]
                         + gates_t[row + 2:row + 3] * o_w[:, cols])
        for hp in range(hg // 2):
            c = (g * hg) // 2 + hp
            pair = jnp.concatenate([mixes[2 * hp], mixes[2 * hp + 1]], axis=0)
            o_ref[0, :, c * LANES:(c + 1) * LANES] = pair.T.astype(o_ref.dtype)


def _nsa(pa3, r3, kc, vct, vst, vwt, expand, *, qp_blk, gate_blk, ks_blk, kw_blk):
    b, t, _ = pa3.shape
    n_cmp_pad = kc.shape[1]
    n_sel = min(SLC_TOPN, t // SLC_BLOCK)
    win_keys = min(WIN + Q_BLOCK, t)
    keys = lambda blk: pl.BlockSpec((1, t, LANES), lambda i, j, blk=blk: (i, 0, blk))
    vals_t = pl.BlockSpec((1,) + vst.shape[1:], lambda i, j: (i, 0, 0, 0))
    return pl.pallas_call(
        functools.partial(_nsa_kernel, n_sel=n_sel, win_keys=win_keys),
        grid=(b, t // Q_BLOCK),
        in_specs=[
            pl.BlockSpec((1, Q_BLOCK, NSA_Q), lambda i, j: (i, j, qp_blk)),
            pl.BlockSpec((1, Q_BLOCK, NSA_Q), lambda i, j: (i, j, 0)),
            pl.BlockSpec((1, Q_BLOCK, LANES), lambda i, j: (i, j, gate_blk)),
            pl.BlockSpec((1, n_cmp_pad, LANES), lambda i, j: (i, 0, 0)),
            pl.BlockSpec((1, LANES, n_cmp_pad), lambda i, j: (i, 0, 0)),
            keys(ks_blk), vals_t, keys(kw_blk), vals_t,
            pl.BlockSpec((t, LANES), lambda i, j: (0, 0)),
        ],
        out_specs=pl.BlockSpec((1, Q_BLOCK, NSA_Q), lambda i, j: (i, j, 0)),
        out_shape=jax.ShapeDtypeStruct((b, t, NSA_Q), BF16),
        scratch_shapes=[pltpu.VMEM((LANES, Q_BLOCK), jnp.int32),
                        pltpu.VMEM((SEL_KEY_TILE, NSA_HPG * Q_BLOCK), F32), pltpu.VMEM((SEL_KEY_TILE, NSA_HPG * Q_BLOCK), F32),
                        pltpu.VMEM((LANES, NSA_HPG * Q_BLOCK), F32),
                        pltpu.VMEM((max(n_cmp_pad, LANES * SLC_BLOCK // CMP_STRIDE), Q_BLOCK), F32)],
        compiler_params=_params(("parallel", "arbitrary")),
        name="nsa",
    )(pa3, r3, pa3, kc, vct, r3, vst, r3, vwt, expand)


def _values_t(v):
    b, t, _ = v.shape
    vt = v.reshape(b, t, NSA_KV_GROUPS, HEAD_DIM).transpose(0, 2, 3, 1)
    ones = jnp.ones((b, NSA_KV_GROUPS, 1, t), v.dtype)
    pad = jnp.zeros((b, NSA_KV_GROUPS, 15, t), v.dtype)
    return jnp.concatenate([vt, ones, pad], axis=2)


def _dil_kernel(q_ref, kp_ref, kc_ref, vp_ref, vc_ref, o_ref, lse_ref, *, span):
    ub = pl.program_id(2)
    nq = DIL_BLOCK
    lane = lax.broadcasted_iota(jnp.int32, (nq, LANES), 1)
    row = lax.broadcasted_iota(jnp.int32, (2 * nq, 2 * nq), 0) & (nq - 1)
    col = lax.broadcasted_iota(jnp.int32, (2 * nq, 2 * nq), 1)
    diff = row + nq - col
    band = (diff >= 0) & (diff <= span)
    band_first = band & ((col >= nq) | (ub > 0))
    for j in range(q_ref.shape[0] // nq):
        rows = slice(j * nq, (j + 1) * nq)
        before = slice((j - 1) * nq, j * nq)
        mask = band_first if j == 0 else band
        for hp in range(DIL_HEADS // 2):
            cols = slice(hp * LANES, (hp + 1) * LANES)
            qt = q_ref[rows, cols] * jnp.asarray(SCALE, BF16)
            zero = jnp.zeros_like(qt)
            qs = jnp.concatenate([jnp.where(lane < HEAD_DIM, qt, zero), jnp.where(lane >= HEAD_DIM, qt, zero)], axis=0)
            k = jnp.concatenate([kp_ref[:, cols] if j == 0 else kc_ref[before, cols], kc_ref[rows, cols]], axis=0)
            v = jnp.concatenate([vp_ref[:, cols] if j == 0 else vc_ref[before, cols], vc_ref[rows, cols]], axis=0)
            s = _dot_nt(qs, k)
            m = jnp.max(jnp.where(mask, s, NEG), axis=-1, keepdims=True)
            e = jnp.exp(jnp.where(mask, s - m, NEG))
            d = jnp.sum(e, axis=-1, keepdims=True)
            p = e / jnp.where(d > 0, d, 1.0)
            o = _dot(p.astype(BF16), v)
            lse = m + jnp.log(d)
            o_ref[rows, cols] = jnp.where(lane < HEAD_DIM, o[:nq], o[nq:]).astype(o_ref.dtype)
            lse_ref[rows, cols] = jnp.where(lane < HEAD_DIM, lse[:nq], lse[nq:])


def _dilated(qd, kd, vd, b, dil, span):
    rows = qd.shape[0]
    ln = rows // b
    ub = min(ln, 4 * DIL_BLOCK)
    nub = ln // ub
    per = ub // DIL_BLOCK
    cur = pl.BlockSpec((ub, DIL_W), lambda i, r, u: (i * nub + u, r))
    prev = pl.BlockSpec((DIL_BLOCK, DIL_W), lambda i, r, u: (jnp.maximum((i * nub + u) * per - 1, 0), r))
    o, lse = pl.pallas_call(
        functools.partial(_dil_kernel, span=span),
        grid=(b, dil, nub),
        in_specs=[cur, prev, cur, prev, cur],
        out_specs=[cur, cur],
        out_shape=[jax.ShapeDtypeStruct(qd.shape, F32), jax.ShapeDtypeStruct(qd.shape, F32)],
        compiler_params=_params(("parallel", "parallel", "arbitrary")),
        name=f"dilated_{dil}",
    )(qd, kd, kd, vd, vd)
    return o, lse


def _out_kernel(*refs, dils):
    ng = len(dils)
    x_ref, yn_ref = refs[:2]
    o_refs, l_refs = refs[2:2 + ng], refs[2 + ng:2 + 2 * ng]
    g0_ref, g1_ref, wn_ref, wd_ref, wo_ref, o_ref = refs[2 + 2 * ng:8 + 2 * ng]
    scrs = list(refs[8 + 2 * ng:])

    def natural(ref, dil):
        if dil == 1:
            return ref[...]
        scr = scrs.pop(0)
        n = scr.shape[1] // dil
        for res in range(dil):
            for k in range(DIL_W // LANES):
                scr[k, pl.ds(res, n, stride=dil), :] = ref[:, res * DIL_W + k * LANES:res * DIL_W + (k + 1) * LANES]
        return jnp.concatenate([scr[k] for k in range(DIL_W // LANES)], axis=1)

    outs = [natural(r, d) for r, d in zip(o_refs, dils)]
    lses = [natural(r, d) for r, d in zip(l_refs, dils)]
    m = functools.reduce(jnp.maximum, lses)
    ws = [jnp.exp(l - m) for l in lses]
    den = functools.reduce(lambda a, c: a + c, ws)
    y_dil = functools.reduce(lambda a, c: a + c, [(w / den) * o for w, o in zip(ws, outs)])
    merged = (jax.nn.sigmoid(g0_ref[...]) * _dot(yn_ref[...], wn_ref[...])
              + jax.nn.sigmoid(g1_ref[...]) * _dot(y_dil.astype(BF16), wd_ref[...]))
    o_ref[...] = x_ref[...] + _dot(merged.astype(BF16), wo_ref[...])


def _out_proj(x2, y_nsa, outs, lses, dils, pa, w_bn, w_bd, w_o, tm=512):
    m, d = x2.shape
    row = lambda w: pl.BlockSpec((tm, w), lambda i: (i, 0))
    dil_row = lambda dil: pl.BlockSpec((tm // dil, dil * DIL_W), lambda i: (i, 0))
    whole = lambda a: pl.BlockSpec(a.shape, lambda i: (0, 0))
    n_staged = 2 * sum(dil != 1 for dil in dils)
    return pl.pallas_call(
        functools.partial(_out_kernel, dils=tuple(dils)),
        grid=(m // tm,),
        in_specs=[row(d), row(NSA_Q)] + [dil_row(dil) for dil in dils] * 2
        + [pl.BlockSpec((tm, d), lambda i: (i, 0)), pl.BlockSpec((tm, d), lambda i: (i, 1))]
        + [whole(w_bn), whole(w_bd), whole(w_o)],
        out_specs=row(d),
        out_shape=jax.ShapeDtypeStruct((m, d), F32),
        scratch_shapes=[pltpu.VMEM((DIL_W // LANES, tm, LANES), F32)] * n_staged,
        compiler_params=_params(("parallel",)),
        name="out_proj",
    )(x2, y_nsa, *outs, *lses, pa, pa, w_bn, w_bd, w_o)


def _mixer(x2, pos_col, inv_lane, b, t, norm_mix, w_in, cmp_pos, cmp_w1, cmp_w2, w_branch_nsa, w_branch_dil, w_out):
    d = x2.shape[1]
    dh = HEAD_DIM
    o_q, o_kv, o_gn = 0, NSA_Q, NSA_Q + 6 * NSA_KV
    o_d = o_gn + 3 * NSA_HEADS
    o_gm = o_d + 3 * DIL_GROUPS * DIL_W
    kv = lambda i: w_in[:, o_kv + i * NSA_KV:o_kv + (i + 1) * NSA_KV]
    dd = lambda i: w_in[:, o_d + i * DIL_GROUPS * DIL_W:o_d + (i + 1) * DIL_GROUPS * DIL_W]
    w_q = w_in[:, o_q:o_q + NSA_Q]
    gn_pad = jnp.zeros((d, 2 * LANES - 3 * NSA_HEADS), w_in.dtype)
    w_a = jnp.concatenate([w_in[:, o_gm:o_gm + 2 * d], w_q, kv(0), kv(1), w_in[:, o_gn:o_gn + 3 * NSA_HEADS], gn_pad],
                          axis=1).astype(BF16)
    w_b = jnp.concatenate([w_q, kv(2), kv(4), dd(0), dd(1)], axis=1).astype(BF16)
    w_c = jnp.concatenate([kv(3), kv(5), dd(2)], axis=1).astype(BF16)
    dils = [dil for _, dil in DIL_PAIRS]
    n_r = NSA_Q + 2 * NSA_KV
    seg_b = ((0, n_r, 1),) + tuple((n_r + i * DIL_W, DIL_W, dil) for i, dil in enumerate(dils)) \
        + tuple((n_r + (DIL_GROUPS + i) * DIL_W, DIL_W, dil) for i, dil in enumerate(dils))
    seg_c = ((0, 2 * NSA_KV, 1),) + tuple((2 * NSA_KV + i * DIL_W, DIL_W, dil) for i, dil in enumerate(dils))

    pa, = _proj(x2, norm_mix, w_a, F32, name="proj_f32")
    rb, *qk_d = _proj(x2, norm_mix, w_b, BF16, segments=seg_b, pos=pos_col, inv_lane=inv_lane, name="proj_rope")
    vc_, *v_d = _proj(x2, norm_mix, w_c, BF16, segments=seg_c, name="proj_bf16")

    n_seg = t // CMP_STRIDE
    src0 = 2 * d + NSA_Q
    seg = pa[:, src0:src0 + 2 * NSA_KV].reshape(b, t, 2, NSA_KV_GROUPS, dh)
    seg = seg.transpose(2, 0, 3, 1, 4).reshape(2, b, NSA_KV_GROUPS, n_seg, CMP_STRIDE * dh)
    pos = cmp_pos.reshape(2, 2, 1, CMP_STRIDE * dh)
    w1 = cmp_w1.reshape(2, 2, CMP_STRIDE * dh, -1).astype(BF16)
    cmp = _compress(seg, pos, w1, cmp_w2.astype(BF16))
    cmp = cmp.transpose(0, 1, 3, 2, 4).reshape(2, b, n_seg, NSA_KV).astype(BF16)

    pa3 = pa.reshape(b, t, -1)
    r3 = rb.reshape(b, t, -1)
    v3 = vc_.reshape(b, t, -1)
    vst = _values_t(v3[:, :, 0:NSA_KV])
    vwt = _values_t(v3[:, :, NSA_KV:2 * NSA_KV])
    expand = jnp.asarray(np.arange(t)[:, None] // SLC_BLOCK == np.arange(LANES)[None, :], BF16)
    y_nsa = _nsa(pa3, r3, cmp[0], cmp[1].transpose(0, 2, 1), vst, vwt, expand,
                 qp_blk=2 * d // NSA_Q, gate_blk=(2 * d + NSA_Q + 2 * NSA_KV) // LANES,
                 ks_blk=NSA_Q // LANES, kw_blk=NSA_Q // LANES + 1)

    outs, lses = [], []
    for gidx, (window, dil) in enumerate(DIL_PAIRS):
        o, lse = _dilated(qk_d[gidx], qk_d[DIL_GROUPS + gidx], v_d[gidx], b, dil, window // dil)
        outs.append(o)
        lses.append(lse)

    return _out_proj(x2, y_nsa.reshape(b * t, NSA_Q), outs, lses, dils, pa,
                     w_branch_nsa.astype(BF16), w_branch_dil.astype(BF16), w_out.astype(BF16))


def kernel(x, positions, norm_ffn1, ffn1_w_in, ffn1_w_out, norm_mix, w_in, cmp_pos, cmp_w1, cmp_w2,
           w_branch_nsa, w_branch_dil, w_out, norm_ffn2, ffn2_w_in, ffn2_w_out, norm_final):
    b, t, d = x.shape
    depth = norm_ffn1.shape[0]
    x2 = x.reshape(b * t, d)
    pos_col = positions.reshape(b * t, 1).astype(jnp.int32)
    inv = jnp.power(ROPE_THETA, -jnp.arange(0, ROPE_DIM, 2, dtype=F32) / ROPE_DIM)
    inv_head = jnp.concatenate([inv, inv, jnp.zeros((HEAD_DIM - ROPE_DIM,), F32)])
    inv_lane = jnp.tile(inv_head, LANES // HEAD_DIM).reshape(1, LANES)
    for l in range(depth):
        last = l == depth - 1
        x2 = _ffn(x2, norm_ffn1[l], ffn1_w_in[l].astype(BF16), ffn1_w_out[l].astype(BF16))
        x2 = _mixer(x2, pos_col, inv_lane, b, t, norm_mix[l], w_in[l], cmp_pos[l], cmp_w1[l], cmp_w2[l],
                    w_branch_nsa[l], w_branch_dil[l], w_out[l])
        x2 = _ffn(x2, norm_ffn2[l], ffn2_w_in[l].astype(BF16), ffn2_w_out[l].astype(BF16),
                  final_gain=norm_final if last else None)
    if depth == 0:
        raise ValueError("depth must be positive")
    return x2.reshape(b, t, d)
```

```python
import functools

import numpy as np
import jax
import jax.numpy as jnp
from jax import lax
from jax.experimental import pallas as pl
from jax.experimental.pallas import tpu as pltpu

HEAD_DIM = 64
ROPE_DIM = HEAD_DIM // 4
ROPE_THETA = 500000.0
NORM_EPS = 1e-6

NSA_HEADS = 8
NSA_KV_GROUPS = 2
NSA_HPG = NSA_HEADS // NSA_KV_GROUPS
CMP_BLOCK = 32
CMP_STRIDE = 16
SLC_BLOCK = 64
SLC_TOPN = 16
WIN = 512
Q_BLOCK = 128
FORCE = 1e9
NEG = -1e30

DIL_PAIRS = ((128, 1), (512, 4), (2048, 16))
DIL_GROUPS = len(DIL_PAIRS)
DIL_HEADS = 4
DIL_BLOCK = 128

NSA_Q = NSA_HEADS * HEAD_DIM
NSA_KV = NSA_KV_GROUPS * HEAD_DIM
DIL_W = DIL_HEADS * HEAD_DIM

LANES = 128
SEL_KEY_TILE = 512
VMEM_LIMIT = 56 * 1024 * 1024

F32 = jnp.float32
BF16 = jnp.bfloat16
SCALE = HEAD_DIM ** -0.5
LOG2E = 1.4426950408889634


def _dot(a, b):
    return jnp.dot(a, b, preferred_element_type=F32)


def _dot_nt(a, b):
    return lax.dot_general(a, b, (((1,), (1,)), ((), ())), preferred_element_type=F32)


def _rmsnorm(x, g):
    return x * lax.rsqrt(jnp.mean(x * x, axis=-1, keepdims=True) + NORM_EPS) * g


def _params(sem):
    return pltpu.CompilerParams(dimension_semantics=sem, vmem_limit_bytes=VMEM_LIMIT)


def _ffn_kernel(*refs, final):
    x_ref, g_ref, wi_ref, wo_ref = refs[:4]
    o_ref = refs[-1]
    dff = wo_ref.shape[0]
    x = x_ref[...]
    h = _rmsnorm(x, g_ref[...]).astype(BF16)
    u = _dot(h, wi_ref[:, :dff])
    v = _dot(h, wi_ref[:, dff:])
    a = (u * jax.nn.sigmoid(u)) * v
    y = x + 0.5 * _dot(a.astype(BF16), wo_ref[...])
    if final:
        y = _rmsnorm(y, refs[4][...])
    o_ref[...] = y


def _ffn(x2, gain, w_in, w_out, final_gain=None, tm=512):
    m, d = x2.shape
    final = final_gain is not None
    whole = lambda a: pl.BlockSpec(a.shape, lambda i: (0, 0))
    in_specs = [pl.BlockSpec((tm, d), lambda i: (i, 0)), pl.BlockSpec((1, d), lambda i: (0, 0)), whole(w_in), whole(w_out)]
    args = [x2, gain.reshape(1, d), w_in, w_out]
    if final:
        in_specs.append(pl.BlockSpec((1, d), lambda i: (0, 0)))
        args.append(final_gain.reshape(1, d))
    return pl.pallas_call(
        functools.partial(_ffn_kernel, final=final),
        grid=(m // tm,),
        in_specs=in_specs,
        out_specs=pl.BlockSpec((tm, d), lambda i: (i, 0)),
        out_shape=jax.ShapeDtypeStruct((m, d), F32),
        compiler_params=_params(("parallel",)),
        name="ffn_final" if final else "ffn",
    )(*args)


def _rope_coeffs(pos_col, inv_lane):
    ang = pos_col.astype(F32) * inv_lane
    c, s = jnp.cos(ang), jnp.sin(ang)
    d = lax.broadcasted_iota(jnp.int32, ang.shape, 1) & (HEAD_DIM - 1)
    half = ROPE_DIM // 2
    coef_c = jnp.where(d < ROPE_DIM, c, 1.0)
    coef_hi = jnp.where(d < half, -s, 0.0)
    coef_lo = jnp.where((d >= half) & (d < ROPE_DIM), s, 0.0)
    return coef_c, coef_hi, coef_lo


def _proj_kernel(*refs, rope, segments):
    n_in = 5 if rope else 3
    x_ref, g_ref, w_ref = refs[:3]
    o_refs = refs[n_in:n_in + len(segments)]
    h = _rmsnorm(x_ref[...], g_ref[...]).astype(BF16)
    y = _dot(h, w_ref[...])
    if len(refs) == n_in + len(segments):
        o_refs[0][...] = y.astype(o_refs[0].dtype)
        return
    y_scr = refs[-1]
    if rope:
        coef_c, coef_hi, coef_lo = _rope_coeffs(refs[3][...], refs[4][...])
        half = ROPE_DIM // 2
        for t in range(y.shape[1] // LANES):
            yt = y[:, t * LANES:(t + 1) * LANES]
            hi = pltpu.roll(yt, LANES - half, axis=1)
            lo = pltpu.roll(yt, half, axis=1)
            y_scr[t] = yt * coef_c + hi * coef_hi + lo * coef_lo
    else:
        for t in range(y.shape[1] // LANES):
            y_scr[t] = y[:, t * LANES:(t + 1) * LANES]
    tm = y.shape[0]
    for o_ref, (c0, w, dil) in zip(o_refs, segments):
        for k in range(w // LANES):
            src = c0 // LANES + k
            for res in range(dil):
                rows = y_scr[src] if dil == 1 else y_scr[src, pl.ds(res, tm // dil, stride=dil), :]
                o_ref[:, res * w + k * LANES:res * w + (k + 1) * LANES] = rows.astype(o_ref.dtype)


def _proj(x2, gain, w, out_dtype, segments=None, pos=None, inv_lane=None, tm=512, name="proj"):
    m, d = x2.shape
    n = w.shape[1]
    rope = pos is not None
    segments = segments or ((0, n, 1),)
    staged = rope or len(segments) > 1 or segments[0][2] != 1
    in_specs = [
        pl.BlockSpec((tm, d), lambda i: (i, 0)),
        pl.BlockSpec((1, d), lambda i: (0, 0)),
        pl.BlockSpec((d, n), lambda i: (0, 0)),
    ]
    args = [x2, gain.reshape(1, d), w]
    if rope:
        in_specs += [pl.BlockSpec((tm, 1), lambda i: (i, 0)), pl.BlockSpec((1, LANES), lambda i: (0, 0))]
        args += [pos, inv_lane]
    outs = pl.pallas_call(
        functools.partial(_proj_kernel, rope=rope, segments=segments),
        grid=(m // tm,),
        in_specs=in_specs,
        out_specs=[pl.BlockSpec((tm // dil, dil * wd), lambda i: (i, 0)) for _, wd, dil in segments],
        out_shape=[jax.ShapeDtypeStruct((m // dil, dil * wd), out_dtype) for _, wd, dil in segments],
        scratch_shapes=[pltpu.VMEM((n // LANES, tm, LANES), F32)] if staged else [],
        compiler_params=_params(("parallel",)),
        name=name,
    )(*args)
    return outs


def _compress_kernel(seg_ref, pos_ref, w1_ref, w2_ref, o_ref):
    seg = seg_ref[0, 0, 0]
    a = _dot((seg + pos_ref[0, 0]).astype(BF16), w1_ref[0, 0])
    b = _dot((seg + pos_ref[0, 1]).astype(BF16), w1_ref[0, 1])
    n_seg = seg.shape[0]
    pre = a + pltpu.roll(b, n_seg - 1, axis=0)
    hid = pre * jax.nn.sigmoid(pre)
    o_ref[0, 0, 0] = _dot(hid.astype(BF16), w2_ref[0])


def _compress(seg, pos, w1, w2):
    _, b, g, n_seg, k = seg.shape
    hid = w1.shape[-1]
    dh = w2.shape[-1]
    return pl.pallas_call(
        _compress_kernel,
        grid=(2, b, g),
        in_specs=[
            pl.BlockSpec((1, 1, 1, n_seg, k), lambda c, i, j: (c, i, j, 0, 0)),
            pl.BlockSpec((1, 2, 1, k), lambda c, i, j: (c, 0, 0, 0)),
            pl.BlockSpec((1, 2, k, hid), lambda c, i, j: (c, 0, 0, 0)),
            pl.BlockSpec((1, hid, dh), lambda c, i, j: (c, 0, 0)),
        ],
        out_specs=pl.BlockSpec((1, 1, 1, n_seg, dh), lambda c, i, j: (c, i, j, 0, 0)),
        out_shape=jax.ShapeDtypeStruct((2, b, g, n_seg, dh), F32),
        compiler_params=_params(("parallel", "parallel", "parallel")),
        name="compress",
    )(seg, pos, w1, w2)


MASK_BIAS = NEG
M_INIT = -1e29


def _stack_heads_t(q, g):
    zeros = jnp.zeros((HEAD_DIM, q.shape[0]), F32)
    parts = []
    for h in range(NSA_HPG):
        hg = g * NSA_HPG + h
        tile_t = (q[:, (hg // 2) * LANES:(hg // 2 + 1) * LANES] * (SCALE * LOG2E)).T
        feat = tile_t[(hg % 2) * HEAD_DIM:(hg % 2 + 1) * HEAD_DIM]
        parts.append(jnp.concatenate([feat, zeros] if g == 0 else [zeros, feat], axis=0))
    return jnp.concatenate(parts, axis=1).astype(BF16)


def _softmax_t(s, bias, hg):
    sb = s + jnp.concatenate([bias] * hg, axis=1)
    m = jnp.maximum(jnp.max(sb, axis=0, keepdims=True), M_INIT)
    e = jnp.exp2(sb - m)
    d = jnp.sum(e, axis=0, keepdims=True)
    return e, 1.0 / jnp.where(d > 0, d, 1.0)


def _nsa_kernel(qp_ref, qr_ref, gate_ref, kc_ref, vct_ref, ks_ref, vst_ref, kw_ref, vwt_ref, et_ref,
                o_ref, cnt_scr, sa_scr, sb_scr, oc_scr, psum_scr, *, n_sel, win_keys):
    qb = pl.program_id(1)
    t0 = qb * Q_BLOCK
    nq = Q_BLOCK
    hg = NSA_HPG
    tk = SEL_KEY_TILE
    n_cmp_pad = kc_ref.shape[1]
    t_row = t0 + lax.broadcasted_iota(jnp.int32, (1, nq), 1)
    blk = lax.broadcasted_iota(jnp.int32, (LANES, nq), 0)
    sub8 = lax.broadcasted_iota(jnp.int32, (8, nq), 0)

    q_plain = qp_ref[0]
    q_rope = qr_ref[0].astype(F32)
    gates_t = jax.nn.sigmoid(gate_ref[0]).T

    cmp_chunk = min(LANES, n_cmp_pad)
    n_cmp_rows = (qb + 1) * (Q_BLOCK // CMP_STRIDE)
    n_cmp_chunks = jnp.minimum((n_cmp_rows + cmp_chunk - 1) // cmp_chunk, n_cmp_pad // cmp_chunk)
    cur = t_row >> 6
    forced = (blk == 0) | (blk == cur) | (blk == cur - 1)
    valid = blk <= cur
    n_blocks = (qb + 1) * (Q_BLOCK // SLC_BLOCK)

    win_start = pl.multiple_of(jnp.maximum(t0 - WIN, 0), Q_BLOCK)
    diff_w = t_row - (win_start + lax.broadcasted_iota(jnp.int32, (win_keys, nq), 0))
    bias_w = jnp.where((diff_w >= 0) & (diff_w < WIN), 0.0, MASK_BIAS)

    first_blk = qb * (Q_BLOCK // SLC_BLOCK)
    n_sweep = (t0 + tk - 1) // tk
    n_key_tiles = ks_ref.shape[1] // tk
    bias_own = jnp.where(lax.broadcasted_iota(jnp.int32, (nq, nq), 0) <= lax.broadcasted_iota(jnp.int32, (nq, nq), 1),
                         0.0, MASK_BIAS)

    for g in range(NSA_KV_GROUPS):
        qt_plain = _stack_heads_t(q_plain, g)
        qt_rope = _stack_heads_t(q_rope, g)
        feat = slice(g * HEAD_DIM, (g + 1) * HEAD_DIM)

        for c in range(1, n_cmp_pad // cmp_chunk + 1):
            @pl.when(n_cmp_chunks == c)
            def _(c=c):
                rows = c * cmp_chunk
                cmp_end = lax.broadcasted_iota(jnp.int32, (rows, nq), 0) * CMP_STRIDE + (CMP_BLOCK - 1)
                bias_c = jnp.where(cmp_end <= t_row, 0.0, MASK_BIAS)
                e_c, r_c = _softmax_t(_dot(kc_ref[0, 0:rows, :], qt_plain), bias_c, hg)
                p_c = e_c * r_c
                oc_scr[...] = _dot(vct_ref[0, :, 0:rows], p_c.astype(BF16))
                p_sum = p_c[:, 0:nq]
                for h in range(1, hg):
                    p_sum = p_sum + p_c[:, h * nq:(h + 1) * nq]
                psum_scr[0:rows, :] = p_sum
                if rows < psum_scr.shape[0]:
                    psum_scr[rows:, :] = jnp.zeros((psum_scr.shape[0] - rows, nq), F32)
        o_c = oc_scr[...]
        per = SLC_BLOCK // CMP_STRIDE
        parts = [psum_scr[pl.ds(k, LANES, stride=per), :] for k in range(per)]
        tail_prev = jnp.where(blk == 0, 0.0, pltpu.roll(parts[per - 1], 1, axis=0))
        imp = parts[0] + parts[1] + parts[2] + 0.5 * parts[3] + 0.5 * tail_prev
        imp = jnp.where(forced, FORCE, jnp.where(valid, imp, -FORCE))

        cnt_scr[...] = jnp.zeros_like(cnt_scr)
        quarter = LANES // 4
        for ig in range(LANES // 8):
            @pl.when((ig * 8 < n_blocks) & (n_blocks > n_sel))
            def _(ig=ig):
                for jq in range(4):
                    @pl.when(jq * quarter < n_blocks)
                    def _(jq=jq):
                        parts = []
                        for jv in range(jq * quarter // 8, (jq + 1) * quarter // 8):
                            x = imp[jv * 8:(jv + 1) * 8]
                            c = jnp.zeros((8, nq), jnp.int32)
                            for r in range(8):
                                row = imp[ig * 8 + r:ig * 8 + r + 1]
                                if jv > ig:
                                    ahead = row >= x
                                elif jv < ig:
                                    ahead = row > x
                                else:
                                    ahead = (row > x) | ((row == x) & (sub8 > r))
                                c = c + ahead.astype(jnp.int32)
                            parts.append(c)
                        cnt_scr[jq * quarter:(jq + 1) * quarter, :] += jnp.concatenate(parts, axis=0)
        picked = (cnt_scr[...] < n_sel) & (imp > -0.5 * FORCE)
        bias_blk = jnp.where(picked & (blk < first_blk), 0.0, MASK_BIAS).astype(BF16)
        q_aug = jnp.concatenate([qt_rope, jnp.concatenate([bias_blk] * hg, axis=1)], axis=0)

        def absorb(s, vt, carry):
            m, acc = carry
            m_new = jnp.maximum(m, jnp.max(s, axis=0, keepdims=True))
            e = jnp.exp2(s - m_new)
            return m_new, jnp.exp2(m - m_new) * acc + _dot(vt, e.astype(BF16))

        def key_tile(kt):
            return pl.multiple_of(jnp.minimum(kt, n_key_tiles - 1) * tk, tk)

        def scores_into(scr, kt):
            k0 = key_tile(kt)
            scr[...] = _dot(jnp.concatenate([ks_ref[0, pl.ds(k0, tk), :], et_ref[pl.ds(k0, tk), :]], axis=1), q_aug)

        def sweep_pair(i, carry):
            scores_into(sb_scr, 2 * i + 1)
            carry = absorb(sa_scr[...], vst_ref[0, g, :, pl.ds(key_tile(2 * i), tk)], carry)
            scores_into(sa_scr, 2 * i + 2)
            return absorb(sb_scr[...], vst_ref[0, g, :, pl.ds(key_tile(2 * i + 1), tk)], carry)

        init = (jnp.full((1, hg * nq), M_INIT, F32), jnp.zeros((vst_ref.shape[2], hg * nq), F32))
        scores_into(sa_scr, 0)
        carry = lax.fori_loop(0, (n_sweep + 1) // 2, sweep_pair, init)
        own = pl.ds(pl.multiple_of(t0, nq), nq)
        s_own = _dot(ks_ref[0, own, :], qt_rope) + jnp.concatenate([bias_own] * hg, axis=1)
        _, acc_s = absorb(s_own, vst_ref[0, g, :, own], carry)
        o_s = acc_s[0:HEAD_DIM] * (1.0 / acc_s[HEAD_DIM:HEAD_DIM + 1])

        s_w = _dot(kw_ref[0, pl.ds(win_start, win_keys), :], qt_rope) + jnp.concatenate([bias_w] * hg, axis=1)
        e_w = jnp.exp2(s_w - jnp.max(s_w, axis=0, keepdims=True))
        acc_w = _dot(vwt_ref[0, g, :, pl.ds(win_start, win_keys)], e_w.astype(BF16))
        o_w = acc_w[0:HEAD_DIM] * (1.0 / acc_w[HEAD_DIM:HEAD_DIM + 1])

        mixes = []
        for h in range(hg):
            row = g * hg * 3 + h * 3
            cols = slice(h * nq, (h + 1) * nq)
            mixes.append(gates_t[row:row + 1] * o_c[feat, cols] + gates_t[row + 1:row + 2] * o_s[:, cols]
                         + gates_t[row + 2:row + 3] * o_w[:, cols])
        for hp in range(hg // 2):
            c = (g * hg) // 2 + hp
            pair = jnp.concatenate([mixes[2 * hp], mixes[2 * hp + 1]], axis=0)
            o_ref[0, :, c * LANES:(c + 1) * LANES] = pair.T.astype(o_ref.dtype)


def _nsa(pa3, r3, kc, vct, vst, vwt, expand, *, qp_blk, gate_blk, ks_blk, kw_blk):
    b, t, _ = pa3.shape
    n_cmp_pad = kc.shape[1]
    n_sel = min(SLC_TOPN, t // SLC_BLOCK)
    win_keys = min(WIN + Q_BLOCK, t)
    keys = lambda blk: pl.BlockSpec((1, t, LANES), lambda i, j, blk=blk: (i, 0, blk))
    vals_t = pl.BlockSpec((1,) + vst.shape[1:], lambda i, j: (i, 0, 0, 0))
    return pl.pallas_call(
        functools.partial(_nsa_kernel, n_sel=n_sel, win_keys=win_keys),
        grid=(b, t // Q_BLOCK),
        in_specs=[
            pl.BlockSpec((1, Q_BLOCK, NSA_Q), lambda i, j: (i, j, qp_blk)),
            pl.BlockSpec((1, Q_BLOCK, NSA_Q), lambda i, j: (i, j, 0)),
            pl.BlockSpec((1, Q_BLOCK, LANES), lambda i, j: (i, j, gate_blk)),
            pl.BlockSpec((1, n_cmp_pad, LANES), lambda i, j: (i, 0, 0)),
            pl.BlockSpec((1, LANES, n_cmp_pad), lambda i, j: (i, 0, 0)),
            keys(ks_blk), vals_t, keys(kw_blk), vals_t,
            pl.BlockSpec((t, LANES), lambda i, j: (0, 0)),
        ],
        out_specs=pl.BlockSpec((1, Q_BLOCK, NSA_Q), lambda i, j: (i, j, 0)),
        out_shape=jax.ShapeDtypeStruct((b, t, NSA_Q), BF16),
        scratch_shapes=[pltpu.VMEM((LANES, Q_BLOCK), jnp.int32),
                        pltpu.VMEM((SEL_KEY_TILE, NSA_HPG * Q_BLOCK), F32), pltpu.VMEM((SEL_KEY_TILE, NSA_HPG * Q_BLOCK), F32),
                        pltpu.VMEM((LANES, NSA_HPG * Q_BLOCK), F32),
                        pltpu.VMEM((max(n_cmp_pad, LANES * SLC_BLOCK // CMP_STRIDE), Q_BLOCK), F32)],
        compiler_params=_params(("parallel", "arbitrary")),
        name="nsa",
    )(pa3, r3, pa3, kc, vct, r3, vst, r3, vwt, expand)


def _values_t(v):
    b, t, _ = v.shape
    vt = v.reshape(b, t, NSA_KV_GROUPS, HEAD_DIM).transpose(0, 2, 3, 1)
    ones = jnp.ones((b, NSA_KV_GROUPS, 1, t), v.dtype)
    pad = jnp.zeros((b, NSA_KV_GROUPS, 15, t), v.dtype)
    return jnp.concatenate([vt, ones, pad], axis=2)


def _dil_kernel(q_ref, kp_ref, kc_ref, vp_ref, vc_ref, o_ref, lse_ref, *, span):
    ub = pl.program_id(2)
    nq = DIL_BLOCK
    lane = lax.broadcasted_iota(jnp.int32, (nq, LANES), 1)
    row = lax.broadcasted_iota(jnp.int32, (2 * nq, 2 * nq), 0) & (nq - 1)
    col = lax.broadcasted_iota(jnp.int32, (2 * nq, 2 * nq), 1)
    diff = row + nq - col
    band = (diff >= 0) & (diff <= span)
    band_first = band & ((col >= nq) | (ub > 0))
    for j in range(q_ref.shape[0] // nq):
        rows = slice(j * nq, (j + 1) * nq)
        before = slice((j - 1) * nq, j * nq)
        mask = band_first if j == 0 else band
        for hp in range(DIL_HEADS // 2):
            cols = slice(hp * LANES, (hp + 1) * LANES)
            qt = q_ref[rows, cols] * jnp.asarray(SCALE, BF16)
            zero = jnp.zeros_like(qt)
            qs = jnp.concatenate([jnp.where(lane < HEAD_DIM, qt, zero), jnp.where(lane >= HEAD_DIM, qt, zero)], axis=0)
            k = jnp.concatenate([kp_ref[:, cols] if j == 0 else kc_ref[before, cols], kc_ref[rows, cols]], axis=0)
            v = jnp.concatenate([vp_ref[:, cols] if j == 0 else vc_ref[before, cols], vc_ref[rows, cols]], axis=0)
            s = _dot_nt(qs, k)
            m = jnp.max(jnp.where(mask, s, NEG), axis=-1, keepdims=True)
            e = jnp.exp(jnp.where(mask, s - m, NEG))
            d = jnp.sum(e, axis=-1, keepdims=True)
            p = e / jnp.where(d > 0, d, 1.0)
            o = _dot(p.astype(BF16), v)
            lse = m + jnp.log(d)
            o_ref[rows, cols] = jnp.where(lane < HEAD_DIM, o[:nq], o[nq:]).astype(o_ref.dtype)
            lse_ref[rows, cols] = jnp.where(lane < HEAD_DIM, lse[:nq], lse[nq:])


def _dilated(qd, kd, vd, b, dil, span):
    rows = qd.shape[0]
    ln = rows // b
    ub = min(ln, 4 * DIL_BLOCK)
    nub = ln // ub
    per = ub // DIL_BLOCK
    cur = pl.BlockSpec((ub, DIL_W), lambda i, r, u: (i * nub + u, r))
    prev = pl.BlockSpec((DIL_BLOCK, DIL_W), lambda i, r, u: (jnp.maximum((i * nub + u) * per - 1, 0), r))
    o, lse = pl.pallas_call(
        functools.partial(_dil_kernel, span=span),
        grid=(b, dil, nub),
        in_specs=[cur, prev, cur, prev, cur],
        out_specs=[cur, cur],
        out_shape=[jax.ShapeDtypeStruct(qd.shape, F32), jax.ShapeDtypeStruct(qd.shape, F32)],
        compiler_params=_params(("parallel", "parallel", "arbitrary")),
        name=f"dilated_{dil}",
    )(qd, kd, kd, vd, vd)
    return o, lse


def _out_kernel(*refs, dils):
    ng = len(dils)
    x_ref, yn_ref = refs[:2]
    o_refs, l_refs = refs[2:2 + ng], refs[2 + ng:2 + 2 * ng]
    g0_ref, g1_ref, wn_ref, wd_ref, wo_ref, o_ref = refs[2 + 2 * ng:8 + 2 * ng]
    scrs = list(refs[8 + 2 * ng:])

    def natural(ref, dil):
        if dil == 1:
            return ref[...]
        scr = scrs.pop(0)
        n = scr.shape[1] // dil
        for res in range(dil):
            for k in range(DIL_W // LANES):
                scr[k, pl.ds(res, n, stride=dil), :] = ref[:, res * DIL_W + k * LANES:res * DIL_W + (k + 1) * LANES]
        return jnp.concatenate([scr[k] for k in range(DIL_W // LANES)], axis=1)

    outs = [natural(r, d) for r, d in zip(o_refs, dils)]
    lses = [natural(r, d) for r, d in zip(l_refs, dils)]
    m = functools.reduce(jnp.maximum, lses)
    ws = [jnp.exp(l - m) for l in lses]
    den = functools.reduce(lambda a, c: a + c, ws)
    y_dil = functools.reduce(lambda a, c: a + c, [(w / den) * o for w, o in zip(ws, outs)])
    merged = (jax.nn.sigmoid(g0_ref[...]) * _dot(yn_ref[...], wn_ref[...])
              + jax.nn.sigmoid(g1_ref[...]) * _dot(y_dil.astype(BF16), wd_ref[...]))
    o_ref[...] = x_ref[...] + _dot(merged.astype(BF16), wo_ref[...])


def _out_proj(x2, y_nsa, outs, lses, dils, pa, w_bn, w_bd, w_o, tm=512):
    m, d = x2.shape
    row = lambda w: pl.BlockSpec((tm, w), lambda i: (i, 0))
    dil_row = lambda dil: pl.BlockSpec((tm // dil, dil * DIL_W), lambda i: (i, 0))
    whole = lambda a: pl.BlockSpec(a.shape, lambda i: (0, 0))
    n_staged = 2 * sum(dil != 1 for dil in dils)
    return pl.pallas_call(
        functools.partial(_out_kernel, dils=tuple(dils)),
        grid=(m // tm,),
        in_specs=[row(d), row(NSA_Q)] + [dil_row(dil) for dil in dils] * 2
        + [pl.BlockSpec((tm, d), lambda i: (i, 0)), pl.BlockSpec((tm, d), lambda i: (i, 1))]
        + [whole(w_bn), whole(w_bd), whole(w_o)],
        out_specs=row(d),
        out_shape=jax.ShapeDtypeStruct((m, d), F32),
        scratch_shapes=[pltpu.VMEM((DIL_W // LANES, tm, LANES), F32)] * n_staged,
        compiler_params=_params(("parallel",)),
        name="out_proj",
    )(x2, y_nsa, *outs, *lses, pa, pa, w_bn, w_bd, w_o)


def _mixer(x2, pos_col, inv_lane, b, t, norm_mix, w_in, cmp_pos, cmp_w1, cmp_w2, w_branch_nsa, w_branch_dil, w_out):
    d = x2.shape[1]
    dh = HEAD_DIM
    o_q, o_kv, o_gn = 0, NSA_Q, NSA_Q + 6 * NSA_KV
    o_d = o_gn + 3 * NSA_HEADS
    o_gm = o_d + 3 * DIL_GROUPS * DIL_W
    kv = lambda i: w_in[:, o_kv + i * NSA_KV:o_kv + (i + 1) * NSA_KV]
    dd = lambda i: w_in[:, o_d + i * DIL_GROUPS * DIL_W:o_d + (i + 1) * DIL_GROUPS * DIL_W]
    w_q = w_in[:, o_q:o_q + NSA_Q]
    gn_pad = jnp.zeros((d, 2 * LANES - 3 * NSA_HEADS), w_in.dtype)
    w_a = jnp.concatenate([w_in[:, o_gm:o_gm + 2 * d], w_q, kv(0), kv(1), w_in[:, o_gn:o_gn + 3 * NSA_HEADS], gn_pad],
                          axis=1).astype(BF16)
    w_b = jnp.concatenate([w_q, kv(2), kv(4), dd(0), dd(1)], axis=1).astype(BF16)
    w_c = jnp.concatenate([kv(3), kv(5), dd(2)], axis=1).astype(BF16)
    dils = [dil for _, dil in DIL_PAIRS]
    n_r = NSA_Q + 2 * NSA_KV
    seg_b = ((0, n_r, 1),) + tuple((n_r + i * DIL_W, DIL_W, dil) for i, dil in enumerate(dils)) \
        + tuple((n_r + (DIL_GROUPS + i) * DIL_W, DIL_W, dil) for i, dil in enumerate(dils))
    seg_c = ((0, 2 * NSA_KV, 1),) + tuple((2 * NSA_KV + i * DIL_W, DIL_W, dil) for i, dil in enumerate(dils))

    pa, = _proj(x2, norm_mix, w_a, F32, name="proj_f32")
    rb, *qk_d = _proj(x2, norm_mix, w_b, BF16, segments=seg_b, pos=pos_col, inv_lane=inv_lane, name="proj_rope")
    vc_, *v_d = _proj(x2, norm_mix, w_c, BF16, segments=seg_c, name="proj_bf16")

    n_seg = t // CMP_STRIDE
    src0 = 2 * d + NSA_Q
    seg = pa[:, src0:src0 + 2 * NSA_KV].reshape(b, t, 2, NSA_KV_GROUPS, dh)
    seg = seg.transpose(2, 0, 3, 1, 4).reshape(2, b, NSA_KV_GROUPS, n_seg, CMP_STRIDE * dh)
    pos = cmp_pos.reshape(2, 2, 1, CMP_STRIDE * dh)
    w1 = cmp_w1.reshape(2, 2, CMP_STRIDE * dh, -1).astype(BF16)
    cmp = _compress(seg, pos, w1, cmp_w2.astype(BF16))
    cmp = cmp.transpose(0, 1, 3, 2, 4).reshape(2, b, n_seg, NSA_KV).astype(BF16)

    pa3 = pa.reshape(b, t, -1)
    r3 = rb.reshape(b, t, -1)
    v3 = vc_.reshape(b, t, -1)
    vst = _values_t(v3[:, :, 0:NSA_KV])
    vwt = _values_t(v3[:, :, NSA_KV:2 * NSA_KV])
    expand = jnp.asarray(np.arange(t)[:, None] // SLC_BLOCK == np.arange(LANES)[None, :], BF16)
    y_nsa = _nsa(pa3, r3, cmp[0], cmp[1].transpose(0, 2, 1), vst, vwt, expand,
                 qp_blk=2 * d // NSA_Q, gate_blk=(2 * d + NSA_Q + 2 * NSA_KV) // LANES,
                 ks_blk=NSA_Q // LANES, kw_blk=NSA_Q // LANES + 1)

    outs, lses = [], []
    for gidx, (window, dil) in enumerate(DIL_PAIRS):
        o, lse = _dilated(qk_d[gidx], qk_d[DIL_GROUPS + gidx], v_d[gidx], b, dil, window // dil)
        outs.append(o)
        lses.append(lse)

    return _out_proj(x2, y_nsa.reshape(b * t, NSA_Q), outs, lses, dils, pa,
                     w_branch_nsa.astype(BF16), w_branch_dil.astype(BF16), w_out.astype(BF16))


def kernel(x, positions, norm_ffn1, ffn1_w_in, ffn1_w_out, norm_mix, w_in, cmp_pos, cmp_w1, cmp_w2,
           w_branch_nsa, w_branch_dil, w_out, norm_ffn2, ffn2_w_in, ffn2_w_out, norm_final):
    b, t, d = x.shape
    depth = norm_ffn1.shape[0]
    x2 = x.reshape(b * t, d)
    pos_col = positions.reshape(b * t, 1).astype(jnp.int32)
    inv = jnp.power(ROPE_THETA, -jnp.arange(0, ROPE_DIM, 2, dtype=F32) / ROPE_DIM)
    inv_head = jnp.concatenate([inv, inv, jnp.zeros((HEAD_DIM - ROPE_DIM,), F32)])
    inv_lane = jnp.tile(inv_head, LANES // HEAD_DIM).reshape(1, LANES)
    for l in range(depth):
        last = l == depth - 1
        x2 = _ffn(x2, norm_ffn1[l], ffn1_w_in[l].astype(BF16), ffn1_w_out[l].astype(BF16))
        x2 = _mixer(x2, pos_col, inv_lane, b, t, norm_mix[l], w_in[l], cmp_pos[l], cmp_w1[l], cmp_w2[l],
                    w_branch_nsa[l], w_branch_dil[l], w_out[l])
        x2 = _ffn(x2, norm_ffn2[l], ffn2_w_in[l].astype(BF16), ffn2_w_out[l].astype(BF16),
                  final_gain=norm_final if last else None)
    if depth == 0:
        raise ValueError("depth must be positive")
    return x2.reshape(b, t, d)
```

```python
import functools

import numpy as np
import jax
import jax.numpy as jnp
from jax import lax
from jax.experimental import pallas as pl
from jax.experimental.pallas import tpu as pltpu

HEAD_DIM = 64
ROPE_DIM = HEAD_DIM // 4
ROPE_THETA = 500000.0
NORM_EPS = 1e-6

NSA_HEADS = 8
NSA_KV_GROUPS = 2
NSA_HPG = NSA_HEADS // NSA_KV_GROUPS
CMP_BLOCK = 32
CMP_STRIDE = 16
SLC_BLOCK = 64
SLC_TOPN = 16
WIN = 512
Q_BLOCK = 128
FORCE = 1e9
NEG = -1e30

DIL_PAIRS = ((128, 1), (512, 4), (2048, 16))
DIL_GROUPS = len(DIL_PAIRS)
DIL_HEADS = 4
DIL_BLOCK = 128

NSA_Q = NSA_HEADS * HEAD_DIM
NSA_KV = NSA_KV_GROUPS * HEAD_DIM
DIL_W = DIL_HEADS * HEAD_DIM

LANES = 128
SEL_KEY_TILE = 512
VMEM_LIMIT = 56 * 1024 * 1024

F32 = jnp.float32
BF16 = jnp.bfloat16
SCALE = HEAD_DIM ** -0.5
LOG2E = 1.4426950408889634


def _dot(a, b):
    return jnp.dot(a, b, preferred_element_type=F32)


def _dot_nt(a, b):
    return lax.dot_general(a, b, (((1,), (1,)), ((), ())), preferred_element_type=F32)


def _rmsnorm(x, g):
    return x * lax.rsqrt(jnp.mean(x * x, axis=-1, keepdims=True) + NORM_EPS) * g


def _params(sem):
    return pltpu.CompilerParams(dimension_semantics=sem, vmem_limit_bytes=VMEM_LIMIT)


def _ffn_kernel(*refs, final):
    x_ref, g_ref, wi_ref, wo_ref = refs[:4]
    o_ref = refs[-1]
    dff = wo_ref.shape[0]
    x = x_ref[...]
    h = _rmsnorm(x, g_ref[...]).astype(BF16)
    u = _dot(h, wi_ref[:, :dff])
    v = _dot(h, wi_ref[:, dff:])
    a = (u * jax.nn.sigmoid(u)) * v
    y = x + 0.5 * _dot(a.astype(BF16), wo_ref[...])
    if final:
        y = _rmsnorm(y, refs[4][...])
    o_ref[...] = y


def _ffn(x2, gain, w_in, w_out, final_gain=None, tm=512):
    m, d = x2.shape
    final = final_gain is not None
    whole = lambda a: pl.BlockSpec(a.shape, lambda i: (0, 0))
    in_specs = [pl.BlockSpec((tm, d), lambda i: (i, 0)), pl.BlockSpec((1, d), lambda i: (0, 0)), whole(w_in), whole(w_out)]
    args = [x2, gain.reshape(1, d), w_in, w_out]
    if final:
        in_specs.append(pl.BlockSpec((1, d), lambda i: (0, 0)))
        args.append(final_gain.reshape(1, d))
    return pl.pallas_call(
        functools.partial(_ffn_kernel, final=final),
        grid=(m // tm,),
        in_specs=in_specs,
        out_specs=pl.BlockSpec((tm, d), lambda i: (i, 0)),
        out_shape=jax.ShapeDtypeStruct((m, d), F32),
        compiler_params=_params(("parallel",)),
        name="ffn_final" if final else "ffn",
    )(*args)


def _rope_coeffs(pos_col, inv_lane):
    ang = pos_col.astype(F32) * inv_lane
    c, s = jnp.cos(ang), jnp.sin(ang)
    d = lax.broadcasted_iota(jnp.int32, ang.shape, 1) & (HEAD_DIM - 1)
    half = ROPE_DIM // 2
    coef_c = jnp.where(d < ROPE_DIM, c, 1.0)
    coef_hi = jnp.where(d < half, -s, 0.0)
    coef_lo = jnp.where((d >= half) & (d < ROPE_DIM), s, 0.0)
    return coef_c, coef_hi, coef_lo


def _proj_kernel(*refs, rope, segments):
    n_in = 5 if rope else 3
    x_ref, g_ref, w_ref = refs[:3]
    o_refs = refs[n_in:n_in + len(segments)]
    h = _rmsnorm(x_ref[...], g_ref[...]).astype(BF16)
    y = _dot(h, w_ref[...])
    if len(refs) == n_in + len(segments):
        o_refs[0][...] = y.astype(o_refs[0].dtype)
        return
    y_scr = refs[-1]
    if rope:
        coef_c, coef_hi, coef_lo = _rope_coeffs(refs[3][...], refs[4][...])
        half = ROPE_DIM // 2
        for t in range(y.shape[1] // LANES):
            yt = y[:, t * LANES:(t + 1) * LANES]
            hi = pltpu.roll(yt, LANES - half, axis=1)
            lo = pltpu.roll(yt, half, axis=1)
            y_scr[t] = yt * coef_c + hi * coef_hi + lo * coef_lo
    else:
        for t in range(y.shape[1] // LANES):
            y_scr[t] = y[:, t * LANES:(t + 1) * LANES]
    tm = y.shape[0]
    for o_ref, (c0, w, dil) in zip(o_refs, segments):
        for k in range(w // LANES):
            src = c0 // LANES + k
            for res in range(dil):
                rows = y_scr[src] if dil == 1 else y_scr[src, pl.ds(res, tm // dil, stride=dil), :]
                o_ref[:, res * w + k * LANES:res * w + (k + 1) * LANES] = rows.astype(o_ref.dtype)


def _proj(x2, gain, w, out_dtype, segments=None, pos=None, inv_lane=None, tm=512, name="proj"):
    m, d = x2.shape
    n = w.shape[1]
    rope = pos is not None
    segments = segments or ((0, n, 1),)
    staged = rope or len(segments) > 1 or segments[0][2] != 1
    in_specs = [
        pl.BlockSpec((tm, d), lambda i: (i, 0)),
        pl.BlockSpec((1, d), lambda i: (0, 0)),
        pl.BlockSpec((d, n), lambda i: (0, 0)),
    ]
    args = [x2, gain.reshape(1, d), w]
    if rope:
        in_specs += [pl.BlockSpec((tm, 1), lambda i: (i, 0)), pl.BlockSpec((1, LANES), lambda i: (0, 0))]
        args += [pos, inv_lane]
    outs = pl.pallas_call(
        functools.partial(_proj_kernel, rope=rope, segments=segments),
        grid=(m // tm,),
        in_specs=in_specs,
        out_specs=[pl.BlockSpec((tm // dil, dil * wd), lambda i: (i, 0)) for _, wd, dil in segments],
        out_shape=[jax.ShapeDtypeStruct((m // dil, dil * wd), out_dtype) for _, wd, dil in segments],
        scratch_shapes=[pltpu.VMEM((n // LANES, tm, LANES), F32)] if staged else [],
        compiler_params=_params(("parallel",)),
        name=name,
    )(*args)
    return outs


def _compress_kernel(ksrc_ref, vsrc_ref, pos_ref, w1_ref, w2_ref, kc_ref, vct_ref):
    n_seg = kc_ref.shape[1]
    outs = []
    for c, src in enumerate((ksrc_ref, vsrc_ref)):
        first = second = None
        for p in range(CMP_STRIDE):
            tok = src[pl.ds(p, n_seg, stride=CMP_STRIDE), :]
            a = _dot((tok + pos_ref[c, p:p + 1, :]).astype(BF16), w1_ref[c, p])
            b = _dot((tok + pos_ref[c, CMP_STRIDE + p:CMP_STRIDE + p + 1, :]).astype(BF16), w1_ref[c, CMP_STRIDE + p])
            first = a if first is None else first + a
            second = b if second is None else second + b
        pre = first + pltpu.roll(second, n_seg - 1, axis=0)
        hid = pre * jax.nn.sigmoid(pre)
        outs.append(_dot(hid.astype(BF16), w2_ref[c]))
    kc_ref[0] = outs[0].astype(kc_ref.dtype)
    vct_ref[0] = outs[1].T.astype(vct_ref.dtype)


def _compress(pa, b, ksrc_blk, pos, w1, w2):
    t = pa.shape[0] // b
    n_seg = t // CMP_STRIDE
    whole = lambda a: pl.BlockSpec(a.shape, lambda i: (0,) * a.ndim)
    return pl.pallas_call(
        _compress_kernel,
        grid=(b,),
        in_specs=[pl.BlockSpec((t, LANES), lambda i: (i, ksrc_blk)), pl.BlockSpec((t, LANES), lambda i: (i, ksrc_blk + 1)),
                  whole(pos), whole(w1), whole(w2)],
        out_specs=[pl.BlockSpec((1, n_seg, LANES), lambda i: (i, 0, 0)), pl.BlockSpec((1, LANES, n_seg), lambda i: (i, 0, 0))],
        out_shape=[jax.ShapeDtypeStruct((b, n_seg, LANES), BF16), jax.ShapeDtypeStruct((b, LANES, n_seg), BF16)],
        compiler_params=_params(("parallel",)),
        name="compress",
    )(pa, pa, pos, w1, w2)


def _both_groups(w):
    z = jnp.zeros_like(w)
    return jnp.concatenate([jnp.concatenate([w, z], axis=-1), jnp.concatenate([z, w], axis=-1)], axis=-2)


MASK_BIAS = NEG
M_INIT = -1e29


def _stack_heads_t(q, g):
    zeros = jnp.zeros((HEAD_DIM, q.shape[0]), F32)
    parts = []
    for h in range(NSA_HPG):
        hg = g * NSA_HPG + h
        tile_t = (q[:, (hg // 2) * LANES:(hg // 2 + 1) * LANES] * (SCALE * LOG2E)).T
        feat = tile_t[(hg % 2) * HEAD_DIM:(hg % 2 + 1) * HEAD_DIM]
        parts.append(jnp.concatenate([feat, zeros] if g == 0 else [zeros, feat], axis=0))
    return jnp.concatenate(parts, axis=1).astype(BF16)


def _softmax_t(s, bias, hg):
    sb = s + jnp.concatenate([bias] * hg, axis=1)
    m = jnp.maximum(jnp.max(sb, axis=0, keepdims=True), M_INIT)
    e = jnp.exp2(sb - m)
    d = jnp.sum(e, axis=0, keepdims=True)
    return e, 1.0 / jnp.where(d > 0, d, 1.0)


def _nsa_kernel(qp_ref, qr_ref, gate_ref, kc_ref, vct_ref, ks_ref, vst_ref, kw_ref, vwt_ref, et_ref,
                o_ref, cnt_scr, sa_scr, sb_scr, oc_scr, psum_scr, *, n_sel, win_keys):
    qb = pl.program_id(1)
    t0 = qb * Q_BLOCK
    nq = Q_BLOCK
    hg = NSA_HPG
    tk = SEL_KEY_TILE
    n_cmp_pad = kc_ref.shape[1]
    t_row = t0 + lax.broadcasted_iota(jnp.int32, (1, nq), 1)
    blk = lax.broadcasted_iota(jnp.int32, (LANES, nq), 0)
    sub8 = lax.broadcasted_iota(jnp.int32, (8, nq), 0)

    q_plain = qp_ref[0]
    q_rope = qr_ref[0].astype(F32)
    gates_t = jax.nn.sigmoid(gate_ref[0]).T

    cmp_chunk = min(LANES, n_cmp_pad)
    n_cmp_rows = (qb + 1) * (Q_BLOCK // CMP_STRIDE)
    n_cmp_chunks = jnp.minimum((n_cmp_rows + cmp_chunk - 1) // cmp_chunk, n_cmp_pad // cmp_chunk)
    cur = t_row >> 6
    forced = (blk == 0) | (blk == cur) | (blk == cur - 1)
    valid = blk <= cur
    n_blocks = (qb + 1) * (Q_BLOCK // SLC_BLOCK)

    win_start = pl.multiple_of(jnp.maximum(t0 - WIN, 0), Q_BLOCK)
    diff_w = t_row - (win_start + lax.broadcasted_iota(jnp.int32, (win_keys, nq), 0))
    bias_w = jnp.where((diff_w >= 0) & (diff_w < WIN), 0.0, MASK_BIAS)

    first_blk = qb * (Q_BLOCK // SLC_BLOCK)
    n_sweep = (t0 + tk - 1) // tk
    n_key_tiles = ks_ref.shape[1] // tk
    bias_own = jnp.where(lax.broadcasted_iota(jnp.int32, (nq, nq), 0) <= lax.broadcasted_iota(jnp.int32, (nq, nq), 1),
                         0.0, MASK_BIAS)

    for g in range(NSA_KV_GROUPS):
        qt_plain = _stack_heads_t(q_plain, g)
        qt_rope = _stack_heads_t(q_rope, g)
        feat = slice(g * HEAD_DIM, (g + 1) * HEAD_DIM)

        for c in range(1, n_cmp_pad // cmp_chunk + 1):
            @pl.when(n_cmp_chunks == c)
            def _(c=c):
                rows = c * cmp_chunk
                cmp_end = lax.broadcasted_iota(jnp.int32, (rows, nq), 0) * CMP_STRIDE + (CMP_BLOCK - 1)
                bias_c = jnp.where(cmp_end <= t_row, 0.0, MASK_BIAS)
                e_c, r_c = _softmax_t(_dot(kc_ref[0, 0:rows, :], qt_plain), bias_c, hg)
                p_c = e_c * r_c
                oc_scr[...] = _dot(vct_ref[0, :, 0:rows], p_c.astype(BF16))
                p_sum = p_c[:, 0:nq]
                for h in range(1, hg):
                    p_sum = p_sum + p_c[:, h * nq:(h + 1) * nq]
                psum_scr[0:rows, :] = p_sum
                if rows < psum_scr.shape[0]:
                    psum_scr[rows:, :] = jnp.zeros((psum_scr.shape[0] - rows, nq), F32)
        o_c = oc_scr[...]
        per = SLC_BLOCK // CMP_STRIDE
        parts = [psum_scr[pl.ds(k, LANES, stride=per), :] for k in range(per)]
        tail_prev = jnp.where(blk == 0, 0.0, pltpu.roll(parts[per - 1], 1, axis=0))
        imp = parts[0] + parts[1] + parts[2] + 0.5 * parts[3] + 0.5 * tail_prev
        imp = jnp.where(forced, FORCE, jnp.where(valid, imp, -FORCE))

        cnt_scr[...] = jnp.zeros_like(cnt_scr)
        quarter = LANES // 4
        for ig in range(LANES // 8):
            @pl.when((ig * 8 < n_blocks) & (n_blocks > n_sel))
            def _(ig=ig):
                for jq in range(4):
                    @pl.when(jq * quarter < n_blocks)
                    def _(jq=jq):
                        parts = []
                        for jv in range(jq * quarter // 8, (jq + 1) * quarter // 8):
                            x = imp[jv * 8:(jv + 1) * 8]
                            c = jnp.zeros((8, nq), jnp.int32)
                            for r in range(8):
                                row = imp[ig * 8 + r:ig * 8 + r + 1]
                                if jv > ig:
                                    ahead = row >= x
                                elif jv < ig:
                                    ahead = row > x
                                else:
                                    ahead = (row > x) | ((row == x) & (sub8 > r))
                                c = c + ahead.astype(jnp.int32)
                            parts.append(c)
                        cnt_scr[jq * quarter:(jq + 1) * quarter, :] += jnp.concatenate(parts, axis=0)
        picked = (cnt_scr[...] < n_sel) & (imp > -0.5 * FORCE)
        bias_blk = jnp.where(picked & (blk < first_blk), 0.0, MASK_BIAS).astype(BF16)
        q_aug = jnp.concatenate([qt_rope, jnp.concatenate([bias_blk] * hg, axis=1)], axis=0)

        def absorb(s, vt, carry):
            m, acc = carry
            m_new = jnp.maximum(m, jnp.max(s, axis=0, keepdims=True))
            e = jnp.exp2(s - m_new)
            return m_new, jnp.exp2(m - m_new) * acc + _dot(vt, e.astype(BF16))

        def key_tile(kt):
            return pl.multiple_of(jnp.minimum(kt, n_key_tiles - 1) * tk, tk)

        def scores_into(scr, kt):
            k0 = key_tile(kt)
            scr[...] = _dot(jnp.concatenate([ks_ref[0, pl.ds(k0, tk), :], et_ref[pl.ds(k0, tk), :]], axis=1), q_aug)

        def sweep_pair(i, carry):
            scores_into(sb_scr, 2 * i + 1)
            carry = absorb(sa_scr[...], vst_ref[0, g, :, pl.ds(key_tile(2 * i), tk)], carry)
            scores_into(sa_scr, 2 * i + 2)
            return absorb(sb_scr[...], vst_ref[0, g, :, pl.ds(key_tile(2 * i + 1), tk)], carry)

        init = (jnp.full((1, hg * nq), M_INIT, F32), jnp.zeros((vst_ref.shape[2], hg * nq), F32))
        scores_into(sa_scr, 0)
        carry = lax.fori_loop(0, (n_sweep + 1) // 2, sweep_pair, init)
        own = pl.ds(pl.multiple_of(t0, nq), nq)
        s_own = _dot(ks_ref[0, own, :], qt_rope) + jnp.concatenate([bias_own] * hg, axis=1)
        _, acc_s = absorb(s_own, vst_ref[0, g, :, own], carry)
        o_s = acc_s[0:HEAD_DIM] * (1.0 / acc_s[HEAD_DIM:HEAD_DIM + 1])

        s_w = _dot(kw_ref[0, pl.ds(win_start, win_keys), :], qt_rope) + jnp.concatenate([bias_w] * hg, axis=1)
        e_w = jnp.exp2(s_w - jnp.max(s_w, axis=0, keepdims=True))
        acc_w = _dot(vwt_ref[0, g, :, pl.ds(win_start, win_keys)], e_w.astype(BF16))
        o_w = acc_w[0:HEAD_DIM] * (1.0 / acc_w[HEAD_DIM:HEAD_DIM + 1])

        mixes = []
        for h in range(hg):
            row = g * hg * 3 + h * 3
            cols = slice(h * nq, (h + 1) * nq)
            mixes.append(gates_t[row:row + 1] * o_c[feat, cols] + gates_t[row + 1:row + 2] * o_s[:, cols]
                         + gates_t[row + 2:row + 3] * o_w[:, cols])
        for hp in range(hg // 2):
            c = (g * hg) // 2 + hp
            pair = jnp.concatenate([mixes[2 * hp], mixes[2 * hp + 1]], axis=0)
            o_ref[0, :, c * LANES:(c + 1) * LANES] = pair.T.astype(o_ref.dtype)


def _nsa(pa3, r3, kc, vct, vst, vwt, expand, *, qp_blk, gate_blk, ks_blk, kw_blk):
    b, t, _ = pa3.shape
    n_cmp_pad = kc.shape[1]
    n_sel = min(SLC_TOPN, t // SLC_BLOCK)
    win_keys = min(WIN + Q_BLOCK, t)
    keys = lambda blk: pl.BlockSpec((1, t, LANES), lambda i, j, blk=blk: (i, 0, blk))
    vals_t = pl.BlockSpec((1,) + vst.shape[1:], lambda i, j: (i, 0, 0, 0))
    return pl.pallas_call(
        functools.partial(_nsa_kernel, n_sel=n_sel, win_keys=win_keys),
        grid=(b, t // Q_BLOCK),
        in_specs=[
            pl.BlockSpec((1, Q_BLOCK, NSA_Q), lambda i, j: (i, j, qp_blk)),
            pl.BlockSpec((1, Q_BLOCK, NSA_Q), lambda i, j: (i, j, 0)),
            pl.BlockSpec((1, Q_BLOCK, LANES), lambda i, j: (i, j, gate_blk)),
            pl.BlockSpec((1, n_cmp_pad, LANES), lambda i, j: (i, 0, 0)),
            pl.BlockSpec((1, LANES, n_cmp_pad), lambda i, j: (i, 0, 0)),
            keys(ks_blk), vals_t, keys(kw_blk), vals_t,
            pl.BlockSpec((t, LANES), lambda i, j: (0, 0)),
        ],
        out_specs=pl.BlockSpec((1, Q_BLOCK, NSA_Q), lambda i, j: (i, j, 0)),
        out_shape=jax.ShapeDtypeStruct((b, t, NSA_Q), BF16),
        scratch_shapes=[pltpu.VMEM((LANES, Q_BLOCK), jnp.int32),
                        pltpu.VMEM((SEL_KEY_TILE, NSA_HPG * Q_BLOCK), F32), pltpu.VMEM((SEL_KEY_TILE, NSA_HPG * Q_BLOCK), F32),
                        pltpu.VMEM((LANES, NSA_HPG * Q_BLOCK), F32),
                        pltpu.VMEM((max(n_cmp_pad, LANES * SLC_BLOCK // CMP_STRIDE), Q_BLOCK), F32)],
        compiler_params=_params(("parallel", "arbitrary")),
        name="nsa",
    )(pa3, r3, pa3, kc, vct, r3, vst, r3, vwt, expand)


def _values_t(v):
    b, t, _ = v.shape
    vt = v.reshape(b, t, NSA_KV_GROUPS, HEAD_DIM).transpose(0, 2, 3, 1)
    ones = jnp.ones((b, NSA_KV_GROUPS, 1, t), v.dtype)
    pad = jnp.zeros((b, NSA_KV_GROUPS, 15, t), v.dtype)
    return jnp.concatenate([vt, ones, pad], axis=2)


def _dil_kernel(q_ref, kp_ref, kc_ref, vp_ref, vc_ref, o_ref, lse_ref, *, span):
    ub = pl.program_id(2)
    nq = DIL_BLOCK
    lane = lax.broadcasted_iota(jnp.int32, (nq, LANES), 1)
    row = lax.broadcasted_iota(jnp.int32, (2 * nq, 2 * nq), 0) & (nq - 1)
    col = lax.broadcasted_iota(jnp.int32, (2 * nq, 2 * nq), 1)
    diff = row + nq - col
    band = (diff >= 0) & (diff <= span)
    bias = jnp.where(band, 0.0, MASK_BIAS)
    bias_first = jnp.where(band & ((col >= nq) | (ub > 0)), 0.0, MASK_BIAS)
    for j in range(q_ref.shape[0] // nq):
        rows = slice(j * nq, (j + 1) * nq)
        before = slice((j - 1) * nq, j * nq)
        for hp in range(DIL_HEADS // 2):
            cols = slice(hp * LANES, (hp + 1) * LANES)
            qt = (q_ref[rows, cols].astype(F32) * (SCALE * LOG2E)).astype(BF16)
            zero = jnp.zeros_like(qt)
            qs = jnp.concatenate([jnp.where(lane < HEAD_DIM, qt, zero), jnp.where(lane >= HEAD_DIM, qt, zero)], axis=0)
            k = jnp.concatenate([kp_ref[:, cols] if j == 0 else kc_ref[before, cols], kc_ref[rows, cols]], axis=0)
            v = jnp.concatenate([vp_ref[:, cols] if j == 0 else vc_ref[before, cols], vc_ref[rows, cols]], axis=0)
            s = _dot_nt(qs, k) + (bias_first if j == 0 else bias)
            m = jnp.max(s, axis=-1, keepdims=True)
            e = jnp.exp2(s - m)
            d = jnp.sum(e, axis=-1, keepdims=True)
            o = _dot(e.astype(BF16), v) * (1.0 / d)
            lse = (m + jnp.log2(d)) * (1.0 / LOG2E)
            o_ref[rows, cols] = jnp.where(lane < HEAD_DIM, o[:nq], o[nq:]).astype(o_ref.dtype)
            lse_ref[rows, cols] = jnp.where(lane < HEAD_DIM, lse[:nq], lse[nq:])


def _dilated(qd, kd, vd, b, dil, span):
    rows = qd.shape[0]
    ln = rows // b
    ub = min(ln, 4 * DIL_BLOCK)
    nub = ln // ub
    per = ub // DIL_BLOCK
    cur = pl.BlockSpec((ub, DIL_W), lambda i, r, u: (i * nub + u, r))
    prev = pl.BlockSpec((DIL_BLOCK, DIL_W), lambda i, r, u: (jnp.maximum((i * nub + u) * per - 1, 0), r))
    o, lse = pl.pallas_call(
        functools.partial(_dil_kernel, span=span),
        grid=(b, dil, nub),
        in_specs=[cur, prev, cur, prev, cur],
        out_specs=[cur, cur],
        out_shape=[jax.ShapeDtypeStruct(qd.shape, F32), jax.ShapeDtypeStruct(qd.shape, F32)],
        compiler_params=_params(("parallel", "parallel", "arbitrary")),
        name=f"dilated_{dil}",
    )(qd, kd, kd, vd, vd)
    return o, lse


def _out_kernel(*refs, dils):
    ng = len(dils)
    x_ref, yn_ref = refs[:2]
    o_refs, l_refs = refs[2:2 + ng], refs[2 + ng:2 + 2 * ng]
    g0_ref, g1_ref, wn_ref, wd_ref, wo_ref, o_ref = refs[2 + 2 * ng:8 + 2 * ng]
    scrs = list(refs[8 + 2 * ng:])

    def natural(ref, dil):
        if dil == 1:
            return ref[...]
        scr = scrs.pop(0)
        n = scr.shape[1] // dil
        for res in range(dil):
            for k in range(DIL_W // LANES):
                scr[k, pl.ds(res, n, stride=dil), :] = ref[:, res * DIL_W + k * LANES:res * DIL_W + (k + 1) * LANES]
        return jnp.concatenate([scr[k] for k in range(DIL_W // LANES)], axis=1)

    outs = [natural(r, d) for r, d in zip(o_refs, dils)]
    lses = [natural(r, d) for r, d in zip(l_refs, dils)]
    m = functools.reduce(jnp.maximum, lses)
    ws = [jnp.exp(l - m) for l in lses]
    inv_den = 1.0 / functools.reduce(lambda a, c: a + c, ws)
    y_dil = functools.reduce(lambda a, c: a + c, [w * o for w, o in zip(ws, outs)]) * inv_den
    merged = (jax.nn.sigmoid(g0_ref[...]) * _dot(yn_ref[...], wn_ref[...])
              + jax.nn.sigmoid(g1_ref[...]) * _dot(y_dil.astype(BF16), wd_ref[...]))
    o_ref[...] = x_ref[...] + _dot(merged.astype(BF16), wo_ref[...])


def _out_proj(x2, y_nsa, outs, lses, dils, pa, w_bn, w_bd, w_o, tm=512):
    m, d = x2.shape
    row = lambda w: pl.BlockSpec((tm, w), lambda i: (i, 0))
    dil_row = lambda dil: pl.BlockSpec((tm // dil, dil * DIL_W), lambda i: (i, 0))
    whole = lambda a: pl.BlockSpec(a.shape, lambda i: (0, 0))
    n_staged = 2 * sum(dil != 1 for dil in dils)
    return pl.pallas_call(
        functools.partial(_out_kernel, dils=tuple(dils)),
        grid=(m // tm,),
        in_specs=[row(d), row(NSA_Q)] + [dil_row(dil) for dil in dils] * 2
        + [pl.BlockSpec((tm, d), lambda i: (i, 0)), pl.BlockSpec((tm, d), lambda i: (i, 1))]
        + [whole(w_bn), whole(w_bd), whole(w_o)],
        out_specs=row(d),
        out_shape=jax.ShapeDtypeStruct((m, d), F32),
        scratch_shapes=[pltpu.VMEM((DIL_W // LANES, tm, LANES), F32)] * n_staged,
        compiler_params=_params(("parallel",)),
        name="out_proj",
    )(x2, y_nsa, *outs, *lses, pa, pa, w_bn, w_bd, w_o)


def _mixer(x2, pos_col, inv_lane, b, t, norm_mix, w_in, cmp_pos, cmp_w1, cmp_w2, w_branch_nsa, w_branch_dil, w_out):
    d = x2.shape[1]
    dh = HEAD_DIM
    o_q, o_kv, o_gn = 0, NSA_Q, NSA_Q + 6 * NSA_KV
    o_d = o_gn + 3 * NSA_HEADS
    o_gm = o_d + 3 * DIL_GROUPS * DIL_W
    kv = lambda i: w_in[:, o_kv + i * NSA_KV:o_kv + (i + 1) * NSA_KV]
    dd = lambda i: w_in[:, o_d + i * DIL_GROUPS * DIL_W:o_d + (i + 1) * DIL_GROUPS * DIL_W]
    w_q = w_in[:, o_q:o_q + NSA_Q]
    gn_pad = jnp.zeros((d, 2 * LANES - 3 * NSA_HEADS), w_in.dtype)
    w_a = jnp.concatenate([w_in[:, o_gm:o_gm + 2 * d], w_q, kv(0), kv(1), w_in[:, o_gn:o_gn + 3 * NSA_HEADS], gn_pad],
                          axis=1).astype(BF16)
    w_b = jnp.concatenate([w_q, kv(2), kv(4), dd(0), dd(1)], axis=1).astype(BF16)
    w_c = jnp.concatenate([kv(3), kv(5), dd(2)], axis=1).astype(BF16)
    dils = [dil for _, dil in DIL_PAIRS]
    n_r = NSA_Q + 2 * NSA_KV
    seg_b = ((0, n_r, 1),) + tuple((n_r + i * DIL_W, DIL_W, dil) for i, dil in enumerate(dils)) \
        + tuple((n_r + (DIL_GROUPS + i) * DIL_W, DIL_W, dil) for i, dil in enumerate(dils))
    seg_c = ((0, 2 * NSA_KV, 1),) + tuple((2 * NSA_KV + i * DIL_W, DIL_W, dil) for i, dil in enumerate(dils))

    pa, = _proj(x2, norm_mix, w_a, F32, name="proj_f32")
    rb, *qk_d = _proj(x2, norm_mix, w_b, BF16, segments=seg_b, pos=pos_col, inv_lane=inv_lane, name="proj_rope")
    vc_, *v_d = _proj(x2, norm_mix, w_c, BF16, segments=seg_c, name="proj_bf16")

    pos = jnp.concatenate([cmp_pos] * NSA_KV_GROUPS, axis=-1)
    w1 = _both_groups(cmp_w1.reshape(2, CMP_BLOCK, dh, -1)).astype(BF16)
    kc, vct = _compress(pa, b, (2 * d + NSA_Q) // LANES, pos, w1, _both_groups(cmp_w2).astype(BF16))

    pa3 = pa.reshape(b, t, -1)
    r3 = rb.reshape(b, t, -1)
    v3 = vc_.reshape(b, t, -1)
    vst = _values_t(v3[:, :, 0:NSA_KV])
    vwt = _values_t(v3[:, :, NSA_KV:2 * NSA_KV])
    expand = jnp.asarray(np.arange(t)[:, None] // SLC_BLOCK == np.arange(LANES)[None, :], BF16)
    y_nsa = _nsa(pa3, r3, kc, vct, vst, vwt, expand,
                 qp_blk=2 * d // NSA_Q, gate_blk=(2 * d + NSA_Q + 2 * NSA_KV) // LANES,
                 ks_blk=NSA_Q // LANES, kw_blk=NSA_Q // LANES + 1)

    outs, lses = [], []
    for gidx, (window, dil) in enumerate(DIL_PAIRS):
        o, lse = _dilated(qk_d[gidx], qk_d[DIL_GROUPS + gidx], v_d[gidx], b, dil, window // dil)
        outs.append(o)
        lses.append(lse)

    return _out_proj(x2, y_nsa.reshape(b * t, NSA_Q), outs, lses, dils, pa,
                     w_branch_nsa.astype(BF16), w_branch_dil.astype(BF16), w_out.astype(BF16))


def kernel(x, positions, norm_ffn1, ffn1_w_in, ffn1_w_out, norm_mix, w_in, cmp_pos, cmp_w1, cmp_w2,
           w_branch_nsa, w_branch_dil, w_out, norm_ffn2, ffn2_w_in, ffn2_w_out, norm_final):
    b, t, d = x.shape
    depth = norm_ffn1.shape[0]
    x2 = x.reshape(b * t, d)
    pos_col = positions.reshape(b * t, 1).astype(jnp.int32)
    inv = jnp.power(ROPE_THETA, -jnp.arange(0, ROPE_DIM, 2, dtype=F32) / ROPE_DIM)
    inv_head = jnp.concatenate([inv, inv, jnp.zeros((HEAD_DIM - ROPE_DIM,), F32)])
    inv_lane = jnp.tile(inv_head, LANES // HEAD_DIM).reshape(1, LANES)
    for l in range(depth):
        last = l == depth - 1
        x2 = _ffn(x2, norm_ffn1[l], ffn1_w_in[l].astype(BF16), ffn1_w_out[l].astype(BF16))
        x2 = _mixer(x2, pos_col, inv_lane, b, t, norm_mix[l], w_in[l], cmp_pos[l], cmp_w1[l], cmp_w2[l],
                    w_branch_nsa[l], w_branch_dil[l], w_out[l])
        x2 = _ffn(x2, norm_ffn2[l], ffn2_w_in[l].astype(BF16), ffn2_w_out[l].astype(BF16),
                  final_gain=norm_final if last else None)
    if depth == 0:
        raise ValueError("depth must be positive")
    return x2.reshape(b, t, d)
```

```python
import functools

import numpy as np
import jax
import jax.numpy as jnp
from jax import lax
from jax.experimental import pallas as pl
from jax.experimental.pallas import tpu as pltpu

HEAD_DIM = 64
ROPE_DIM = HEAD_DIM // 4
ROPE_THETA = 500000.0
NORM_EPS = 1e-6

NSA_HEADS = 8
NSA_KV_GROUPS = 2
NSA_HPG = NSA_HEADS // NSA_KV_GROUPS
CMP_BLOCK = 32
CMP_STRIDE = 16
SLC_BLOCK = 64
SLC_TOPN = 16
WIN = 512
Q_BLOCK = 128
FORCE = 1e9
NEG = -1e30

DIL_PAIRS = ((128, 1), (512, 4), (2048, 16))
DIL_GROUPS = len(DIL_PAIRS)
DIL_HEADS = 4
DIL_BLOCK = 128

NSA_Q = NSA_HEADS * HEAD_DIM
NSA_KV = NSA_KV_GROUPS * HEAD_DIM
DIL_W = DIL_HEADS * HEAD_DIM

LANES = 128
SEL_KEY_TILE = 512
VAL_ROWS = HEAD_DIM + 16
VMEM_LIMIT = 56 * 1024 * 1024

F32 = jnp.float32
BF16 = jnp.bfloat16
SCALE = HEAD_DIM ** -0.5
LOG2E = 1.4426950408889634


def _dot(a, b):
    return jnp.dot(a, b, preferred_element_type=F32)


def _dot_nt(a, b):
    return lax.dot_general(a, b, (((1,), (1,)), ((), ())), preferred_element_type=F32)


def _rmsnorm(x, g):
    return x * lax.rsqrt(jnp.mean(x * x, axis=-1, keepdims=True) + NORM_EPS) * g


def _params(sem):
    return pltpu.CompilerParams(dimension_semantics=sem, vmem_limit_bytes=VMEM_LIMIT)


def _ffn_kernel(*refs, final):
    x_ref, g_ref, wi_ref, wo_ref = refs[:4]
    o_ref = refs[-1]
    dff = wo_ref.shape[0]
    x = x_ref[...]
    h = _rmsnorm(x, g_ref[...]).astype(BF16)
    u = _dot(h, wi_ref[:, :dff])
    v = _dot(h, wi_ref[:, dff:])
    a = (u * jax.nn.sigmoid(u)) * v
    y = x + 0.5 * _dot(a.astype(BF16), wo_ref[...])
    if final:
        y = _rmsnorm(y, refs[4][...])
    o_ref[...] = y


def _ffn(x2, gain, w_in, w_out, final_gain=None, tm=512):
    m, d = x2.shape
    final = final_gain is not None
    whole = lambda a: pl.BlockSpec(a.shape, lambda i: (0, 0))
    in_specs = [pl.BlockSpec((tm, d), lambda i: (i, 0)), pl.BlockSpec((1, d), lambda i: (0, 0)), whole(w_in), whole(w_out)]
    args = [x2, gain.reshape(1, d), w_in, w_out]
    if final:
        in_specs.append(pl.BlockSpec((1, d), lambda i: (0, 0)))
        args.append(final_gain.reshape(1, d))
    return pl.pallas_call(
        functools.partial(_ffn_kernel, final=final),
        grid=(m // tm,),
        in_specs=in_specs,
        out_specs=pl.BlockSpec((tm, d), lambda i: (i, 0)),
        out_shape=jax.ShapeDtypeStruct((m, d), F32),
        compiler_params=_params(("parallel",)),
        name="ffn_final" if final else "ffn",
    )(*args)


def _rope_coeffs(pos_col, inv_lane):
    ang = pos_col.astype(F32) * inv_lane
    c, s = jnp.cos(ang), jnp.sin(ang)
    d = lax.broadcasted_iota(jnp.int32, ang.shape, 1) & (HEAD_DIM - 1)
    half = ROPE_DIM // 2
    coef_c = jnp.where(d < ROPE_DIM, c, 1.0)
    coef_hi = jnp.where(d < half, -s, 0.0)
    coef_lo = jnp.where((d >= half) & (d < ROPE_DIM), s, 0.0)
    return coef_c, coef_hi, coef_lo


def _proj_kernel(*refs, rope, segments):
    n_in = 5 if rope else 3
    x_ref, g_ref, w_ref = refs[:3]
    o_refs = refs[n_in:n_in + len(segments)]
    h = _rmsnorm(x_ref[...], g_ref[...]).astype(BF16)
    y = _dot(h, w_ref[...])
    if len(refs) == n_in + len(segments):
        o_refs[0][...] = y.astype(o_refs[0].dtype)
        return
    y_scr = refs[-1]
    if rope:
        coef_c, coef_hi, coef_lo = _rope_coeffs(refs[3][...], refs[4][...])
        half = ROPE_DIM // 2
        for t in range(y.shape[1] // LANES):
            yt = y[:, t * LANES:(t + 1) * LANES]
            hi = pltpu.roll(yt, LANES - half, axis=1)
            lo = pltpu.roll(yt, half, axis=1)
            y_scr[t] = yt * coef_c + hi * coef_hi + lo * coef_lo
    else:
        for t in range(y.shape[1] // LANES):
            y_scr[t] = y[:, t * LANES:(t + 1) * LANES]
    tm = y.shape[0]
    for o_ref, (c0, w, dil) in zip(o_refs, segments):
        if dil == 0:
            yt = y_scr[c0 // LANES].T
            pad = jnp.where(lax.broadcasted_iota(jnp.int32, (VAL_ROWS - HEAD_DIM, tm), 0) == 0, 1.0, 0.0)
            for g in range(NSA_KV_GROUPS):
                rows = jnp.concatenate([yt[g * HEAD_DIM:(g + 1) * HEAD_DIM], pad], axis=0)
                o_ref[g] = rows.astype(o_ref.dtype)
            continue
        for k in range(w // LANES):
            src = c0 // LANES + k
            for res in range(dil):
                rows = y_scr[src] if dil == 1 else y_scr[src, pl.ds(res, tm // dil, stride=dil), :]
                o_ref[:, res * w + k * LANES:res * w + (k + 1) * LANES] = rows.astype(o_ref.dtype)


def _proj(x2, gain, w, out_dtype, segments=None, pos=None, inv_lane=None, tm=512, name="proj"):
    m, d = x2.shape
    n = w.shape[1]
    rope = pos is not None
    segments = segments or ((0, n, 1),)
    staged = rope or len(segments) > 1 or segments[0][2] != 1
    in_specs = [
        pl.BlockSpec((tm, d), lambda i: (i, 0)),
        pl.BlockSpec((1, d), lambda i: (0, 0)),
        pl.BlockSpec((d, n), lambda i: (0, 0)),
    ]
    args = [x2, gain.reshape(1, d), w]
    if rope:
        in_specs += [pl.BlockSpec((tm, 1), lambda i: (i, 0)), pl.BlockSpec((1, LANES), lambda i: (0, 0))]
        args += [pos, inv_lane]
    outs = pl.pallas_call(
        functools.partial(_proj_kernel, rope=rope, segments=segments),
        grid=(m // tm,),
        in_specs=in_specs,
        out_specs=[pl.BlockSpec((tm // dil, dil * wd), lambda i: (i, 0)) if dil else
                   pl.BlockSpec((NSA_KV_GROUPS, VAL_ROWS, tm), lambda i: (0, 0, i)) for _, wd, dil in segments],
        out_shape=[jax.ShapeDtypeStruct((m // dil, dil * wd) if dil else (NSA_KV_GROUPS, VAL_ROWS, m), out_dtype)
                   for _, wd, dil in segments],
        scratch_shapes=[pltpu.VMEM((n // LANES, tm, LANES), F32)] if staged else [],
        compiler_params=_params(("parallel",)),
        name=name,
    )(*args)
    return outs


def _compress_kernel(ksrc_ref, vsrc_ref, pos_ref, w1_ref, w2_ref, kc_ref, vct_ref):
    n_seg = kc_ref.shape[1]
    outs = []
    for c, src in enumerate((ksrc_ref, vsrc_ref)):
        first = second = None
        for p in range(CMP_STRIDE):
            tok = src[pl.ds(p, n_seg, stride=CMP_STRIDE), :]
            a = _dot((tok + pos_ref[c, p:p + 1, :]).astype(BF16), w1_ref[c, p])
            b = _dot((tok + pos_ref[c, CMP_STRIDE + p:CMP_STRIDE + p + 1, :]).astype(BF16), w1_ref[c, CMP_STRIDE + p])
            first = a if first is None else first + a
            second = b if second is None else second + b
        pre = first + pltpu.roll(second, n_seg - 1, axis=0)
        hid = pre * jax.nn.sigmoid(pre)
        outs.append(_dot(hid.astype(BF16), w2_ref[c]))
    kc_ref[0] = outs[0].astype(kc_ref.dtype)
    vct_ref[0] = outs[1].T.astype(vct_ref.dtype)


def _compress(pa, b, ksrc_blk, pos, w1, w2):
    t = pa.shape[0] // b
    n_seg = t // CMP_STRIDE
    whole = lambda a: pl.BlockSpec(a.shape, lambda i: (0,) * a.ndim)
    return pl.pallas_call(
        _compress_kernel,
        grid=(b,),
        in_specs=[pl.BlockSpec((t, LANES), lambda i: (i, ksrc_blk)), pl.BlockSpec((t, LANES), lambda i: (i, ksrc_blk + 1)),
                  whole(pos), whole(w1), whole(w2)],
        out_specs=[pl.BlockSpec((1, n_seg, LANES), lambda i: (i, 0, 0)), pl.BlockSpec((1, LANES, n_seg), lambda i: (i, 0, 0))],
        out_shape=[jax.ShapeDtypeStruct((b, n_seg, LANES), BF16), jax.ShapeDtypeStruct((b, LANES, n_seg), BF16)],
        compiler_params=_params(("parallel",)),
        name="compress",
    )(pa, pa, pos, w1, w2)


def _both_groups(w):
    z = jnp.zeros_like(w)
    return jnp.concatenate([jnp.concatenate([w, z], axis=-1), jnp.concatenate([z, w], axis=-1)], axis=-2)


MASK_BIAS = NEG
M_INIT = -1e29


def _stack_heads_t(q, g):
    zeros = jnp.zeros((HEAD_DIM, q.shape[0]), F32)
    parts = []
    for h in range(NSA_HPG):
        hg = g * NSA_HPG + h
        tile_t = (q[:, (hg // 2) * LANES:(hg // 2 + 1) * LANES] * (SCALE * LOG2E)).T
        feat = tile_t[(hg % 2) * HEAD_DIM:(hg % 2 + 1) * HEAD_DIM]
        parts.append(jnp.concatenate([feat, zeros] if g == 0 else [zeros, feat], axis=0))
    return jnp.concatenate(parts, axis=1).astype(BF16)


def _softmax_t(s, bias, hg):
    sb = s + jnp.concatenate([bias] * hg, axis=1)
    m = jnp.maximum(jnp.max(sb, axis=0, keepdims=True), M_INIT)
    e = jnp.exp2(sb - m)
    d = jnp.sum(e, axis=0, keepdims=True)
    return e, 1.0 / jnp.where(d > 0, d, 1.0)


def _nsa_kernel(qp_ref, qr_ref, gate_ref, kc_ref, vct_ref, ks_ref, vst_ref, kw_ref, vwt_ref, et_ref,
                o_ref, cnt_scr, sa_scr, sb_scr, oc_scr, psum_scr, *, n_sel, win_keys):
    qb = pl.program_id(1)
    t0 = qb * Q_BLOCK
    nq = Q_BLOCK
    hg = NSA_HPG
    tk = SEL_KEY_TILE
    n_cmp_pad = kc_ref.shape[1]
    t_row = t0 + lax.broadcasted_iota(jnp.int32, (1, nq), 1)
    blk = lax.broadcasted_iota(jnp.int32, (LANES, nq), 0)
    sub8 = lax.broadcasted_iota(jnp.int32, (8, nq), 0)

    q_plain = qp_ref[0]
    q_rope = qr_ref[0].astype(F32)
    gates_t = jax.nn.sigmoid(gate_ref[0]).T

    cmp_chunk = min(LANES, n_cmp_pad)
    n_cmp_rows = (qb + 1) * (Q_BLOCK // CMP_STRIDE)
    n_cmp_chunks = jnp.minimum((n_cmp_rows + cmp_chunk - 1) // cmp_chunk, n_cmp_pad // cmp_chunk)
    cur = t_row >> 6
    forced = (blk == 0) | (blk == cur) | (blk == cur - 1)
    valid = blk <= cur
    n_blocks = (qb + 1) * (Q_BLOCK // SLC_BLOCK)

    win_start = pl.multiple_of(jnp.maximum(t0 - WIN, 0), Q_BLOCK)
    diff_w = t_row - (win_start + lax.broadcasted_iota(jnp.int32, (win_keys, nq), 0))
    bias_w = jnp.where((diff_w >= 0) & (diff_w < WIN), 0.0, MASK_BIAS)

    first_blk = qb * (Q_BLOCK // SLC_BLOCK)
    n_sweep = (t0 + tk - 1) // tk
    n_key_tiles = ks_ref.shape[1] // tk
    bias_own = jnp.where(lax.broadcasted_iota(jnp.int32, (nq, nq), 0) <= lax.broadcasted_iota(jnp.int32, (nq, nq), 1),
                         0.0, MASK_BIAS)

    for g in range(NSA_KV_GROUPS):
        qt_plain = _stack_heads_t(q_plain, g)
        qt_rope = _stack_heads_t(q_rope, g)
        feat = slice(g * HEAD_DIM, (g + 1) * HEAD_DIM)

        for c in range(1, n_cmp_pad // cmp_chunk + 1):
            @pl.when(n_cmp_chunks == c)
            def _(c=c):
                rows = c * cmp_chunk
                cmp_end = lax.broadcasted_iota(jnp.int32, (rows, nq), 0) * CMP_STRIDE + (CMP_BLOCK - 1)
                bias_c = jnp.where(cmp_end <= t_row, 0.0, MASK_BIAS)
                e_c, r_c = _softmax_t(_dot(kc_ref[0, 0:rows, :], qt_plain), bias_c, hg)
                p_c = e_c * r_c
                oc_scr[...] = _dot(vct_ref[0, :, 0:rows], p_c.astype(BF16))
                p_sum = p_c[:, 0:nq]
                for h in range(1, hg):
                    p_sum = p_sum + p_c[:, h * nq:(h + 1) * nq]
                psum_scr[0:rows, :] = p_sum
                if rows < psum_scr.shape[0]:
                    psum_scr[rows:, :] = jnp.zeros((psum_scr.shape[0] - rows, nq), F32)
        o_c = oc_scr[...]
        per = SLC_BLOCK // CMP_STRIDE
        parts = [psum_scr[pl.ds(k, LANES, stride=per), :] for k in range(per)]
        tail_prev = jnp.where(blk == 0, 0.0, pltpu.roll(parts[per - 1], 1, axis=0))
        imp = parts[0] + parts[1] + parts[2] + 0.5 * parts[3] + 0.5 * tail_prev
        imp = jnp.where(forced, FORCE, jnp.where(valid, imp, -FORCE))

        cnt_scr[...] = jnp.zeros_like(cnt_scr)
        quarter = LANES // 4
        for ig in range(LANES // 8):
            @pl.when((ig * 8 < n_blocks) & (n_blocks > n_sel))
            def _(ig=ig):
                for jq in range(4):
                    @pl.when(jq * quarter < n_blocks)
                    def _(jq=jq):
                        parts = []
                        for jv in range(jq * quarter // 8, (jq + 1) * quarter // 8):
                            x = imp[jv * 8:(jv + 1) * 8]
                            c = jnp.zeros((8, nq), jnp.int32)
                            for r in range(8):
                                row = imp[ig * 8 + r:ig * 8 + r + 1]
                                if jv > ig:
                                    ahead = row >= x
                                elif jv < ig:
                                    ahead = row > x
                                else:
                                    ahead = (row > x) | ((row == x) & (sub8 > r))
                                c = c + ahead.astype(jnp.int32)
                            parts.append(c)
                        cnt_scr[jq * quarter:(jq + 1) * quarter, :] += jnp.concatenate(parts, axis=0)
        picked = (cnt_scr[...] < n_sel) & (imp > -0.5 * FORCE)
        bias_blk = jnp.where(picked & (blk < first_blk), 0.0, MASK_BIAS).astype(BF16)
        q_aug = jnp.concatenate([qt_rope, jnp.concatenate([bias_blk] * hg, axis=1)], axis=0)

        def absorb(s, vt, carry):
            m, acc = carry
            m_new = jnp.maximum(m, jnp.max(s, axis=0, keepdims=True))
            e = jnp.exp2(s - m_new)
            return m_new, jnp.exp2(m - m_new) * acc + _dot(vt, e.astype(BF16))

        def key_tile(kt):
            return pl.multiple_of(jnp.minimum(kt, n_key_tiles - 1) * tk, tk)

        def scores_into(scr, kt):
            k0 = key_tile(kt)
            scr[...] = _dot(jnp.concatenate([ks_ref[0, pl.ds(k0, tk), :], et_ref[pl.ds(k0, tk), :]], axis=1), q_aug)

        def sweep_pair(i, carry):
            scores_into(sb_scr, 2 * i + 1)
            carry = absorb(sa_scr[...], vst_ref[g, :, pl.ds(key_tile(2 * i), tk)], carry)
            scores_into(sa_scr, 2 * i + 2)
            return absorb(sb_scr[...], vst_ref[g, :, pl.ds(key_tile(2 * i + 1), tk)], carry)

        init = (jnp.full((1, hg * nq), M_INIT, F32), jnp.zeros((VAL_ROWS, hg * nq), F32))
        scores_into(sa_scr, 0)
        carry = lax.fori_loop(0, (n_sweep + 1) // 2, sweep_pair, init)
        own = pl.ds(pl.multiple_of(t0, nq), nq)
        s_own = _dot(ks_ref[0, own, :], qt_rope) + jnp.concatenate([bias_own] * hg, axis=1)
        _, acc_s = absorb(s_own, vst_ref[g, :, own], carry)
        o_s = acc_s[0:HEAD_DIM] * (1.0 / acc_s[HEAD_DIM:HEAD_DIM + 1])

        s_w = _dot(kw_ref[0, pl.ds(win_start, win_keys), :], qt_rope) + jnp.concatenate([bias_w] * hg, axis=1)
        e_w = jnp.exp2(s_w - jnp.max(s_w, axis=0, keepdims=True))
        acc_w = _dot(vwt_ref[g, :, pl.ds(win_start, win_keys)], e_w.astype(BF16))
        o_w = acc_w[0:HEAD_DIM] * (1.0 / acc_w[HEAD_DIM:HEAD_DIM + 1])

        mixes = []
        for h in range(hg):
            row = g * hg * 3 + h * 3
            cols = slice(h * nq, (h + 1) * nq)
            mixes.append(gates_t[row:row + 1] * o_c[feat, cols] + gates_t[row + 1:row + 2] * o_s[:, cols]
                         + gates_t[row + 2:row + 3] * o_w[:, cols])
        for hp in range(hg // 2):
            c = (g * hg) // 2 + hp
            pair = jnp.concatenate([mixes[2 * hp], mixes[2 * hp + 1]], axis=0)
            o_ref[0, :, c * LANES:(c + 1) * LANES] = pair.T.astype(o_ref.dtype)


def _nsa(pa3, r3, kc, vct, vst, vwt, expand, *, qp_blk, gate_blk, ks_blk, kw_blk):
    b, t, _ = pa3.shape
    n_cmp_pad = kc.shape[1]
    n_sel = min(SLC_TOPN, t // SLC_BLOCK)
    win_keys = min(WIN + Q_BLOCK, t)
    keys = lambda blk: pl.BlockSpec((1, t, LANES), lambda i, j, blk=blk: (i, 0, blk))
    vals_t = pl.BlockSpec((NSA_KV_GROUPS, VAL_ROWS, t), lambda i, j: (0, 0, i))
    return pl.pallas_call(
        functools.partial(_nsa_kernel, n_sel=n_sel, win_keys=win_keys),
        grid=(b, t // Q_BLOCK),
        in_specs=[
            pl.BlockSpec((1, Q_BLOCK, NSA_Q), lambda i, j: (i, j, qp_blk)),
            pl.BlockSpec((1, Q_BLOCK, NSA_Q), lambda i, j: (i, j, 0)),
            pl.BlockSpec((1, Q_BLOCK, LANES), lambda i, j: (i, j, gate_blk)),
            pl.BlockSpec((1, n_cmp_pad, LANES), lambda i, j: (i, 0, 0)),
            pl.BlockSpec((1, LANES, n_cmp_pad), lambda i, j: (i, 0, 0)),
            keys(ks_blk), vals_t, keys(kw_blk), vals_t,
            pl.BlockSpec((t, LANES), lambda i, j: (0, 0)),
        ],
        out_specs=pl.BlockSpec((1, Q_BLOCK, NSA_Q), lambda i, j: (i, j, 0)),
        out_shape=jax.ShapeDtypeStruct((b, t, NSA_Q), BF16),
        scratch_shapes=[pltpu.VMEM((LANES, Q_BLOCK), jnp.int32),
                        pltpu.VMEM((SEL_KEY_TILE, NSA_HPG * Q_BLOCK), F32), pltpu.VMEM((SEL_KEY_TILE, NSA_HPG * Q_BLOCK), F32),
                        pltpu.VMEM((LANES, NSA_HPG * Q_BLOCK), F32),
                        pltpu.VMEM((max(n_cmp_pad, LANES * SLC_BLOCK // CMP_STRIDE), Q_BLOCK), F32)],
        compiler_params=_params(("parallel", "arbitrary")),
        name="nsa",
    )(pa3, r3, pa3, kc, vct, r3, vst, r3, vwt, expand)


def _dil_kernel(q_ref, kp_ref, kc_ref, vp_ref, vc_ref, o_ref, lse_ref, *, span):
    ub = pl.program_id(2)
    nq = DIL_BLOCK
    lane = lax.broadcasted_iota(jnp.int32, (nq, LANES), 1)
    row = lax.broadcasted_iota(jnp.int32, (2 * nq, 2 * nq), 0) & (nq - 1)
    col = lax.broadcasted_iota(jnp.int32, (2 * nq, 2 * nq), 1)
    diff = row + nq - col
    band = (diff >= 0) & (diff <= span)
    bias = jnp.where(band, 0.0, MASK_BIAS)
    bias_first = jnp.where(band & ((col >= nq) | (ub > 0)), 0.0, MASK_BIAS)
    for j in range(q_ref.shape[0] // nq):
        rows = slice(j * nq, (j + 1) * nq)
        before = slice((j - 1) * nq, j * nq)
        for hp in range(DIL_HEADS // 2):
            cols = slice(hp * LANES, (hp + 1) * LANES)
            qt = (q_ref[rows, cols].astype(F32) * (SCALE * LOG2E)).astype(BF16)
            zero = jnp.zeros_like(qt)
            qs = jnp.concatenate([jnp.where(lane < HEAD_DIM, qt, zero), jnp.where(lane >= HEAD_DIM, qt, zero)], axis=0)
            k = jnp.concatenate([kp_ref[:, cols] if j == 0 else kc_ref[before, cols], kc_ref[rows, cols]], axis=0)
            v = jnp.concatenate([vp_ref[:, cols] if j == 0 else vc_ref[before, cols], vc_ref[rows, cols]], axis=0)
            s = _dot_nt(qs, k) + (bias_first if j == 0 else bias)
            m = jnp.max(s, axis=-1, keepdims=True)
            e = jnp.exp2(s - m)
            d = jnp.sum(e, axis=-1, keepdims=True)
            o = _dot(e.astype(BF16), v) * (1.0 / d)
            lse = (m + jnp.log2(d)) * (1.0 / LOG2E)
            o_ref[rows, cols] = jnp.where(lane < HEAD_DIM, o[:nq], o[nq:]).astype(o_ref.dtype)
            lse_ref[rows, cols] = jnp.where(lane < HEAD_DIM, lse[:nq], lse[nq:])


def _dilated(qd, kd, vd, b, dil, span):
    rows = qd.shape[0]
    ln = rows // b
    ub = min(ln, 4 * DIL_BLOCK)
    nub = ln // ub
    per = ub // DIL_BLOCK
    cur = pl.BlockSpec((ub, DIL_W), lambda i, r, u: (i * nub + u, r))
    prev = pl.BlockSpec((DIL_BLOCK, DIL_W), lambda i, r, u: (jnp.maximum((i * nub + u) * per - 1, 0), r))
    o, lse = pl.pallas_call(
        functools.partial(_dil_kernel, span=span),
        grid=(b, dil, nub),
        in_specs=[cur, prev, cur, prev, cur],
        out_specs=[cur, cur],
        out_shape=[jax.ShapeDtypeStruct(qd.shape, F32), jax.ShapeDtypeStruct(qd.shape, F32)],
        compiler_params=_params(("parallel", "parallel", "arbitrary")),
        name=f"dilated_{dil}",
    )(qd, kd, kd, vd, vd)
    return o, lse


def _out_kernel(*refs, dils):
    ng = len(dils)
    x_ref, yn_ref = refs[:2]
    o_refs, l_refs = refs[2:2 + ng], refs[2 + ng:2 + 2 * ng]
    g0_ref, g1_ref, wn_ref, wd_ref, wo_ref, o_ref = refs[2 + 2 * ng:8 + 2 * ng]
    scrs = list(refs[8 + 2 * ng:])

    def natural(ref, dil):
        if dil == 1:
            return ref[...]
        scr = scrs.pop(0)
        n = scr.shape[1] // dil
        for res in range(dil):
            for k in range(DIL_W // LANES):
                scr[k, pl.ds(res, n, stride=dil), :] = ref[:, res * DIL_W + k * LANES:res * DIL_W + (k + 1) * LANES]
        return jnp.concatenate([scr[k] for k in range(DIL_W // LANES)], axis=1)

    outs = [natural(r, d) for r, d in zip(o_refs, dils)]
    lses = [natural(r, d) for r, d in zip(l_refs, dils)]
    m = functools.reduce(jnp.maximum, lses)
    ws = [jnp.exp(l - m) for l in lses]
    inv_den = 1.0 / functools.reduce(lambda a, c: a + c, ws)
    y_dil = functools.reduce(lambda a, c: a + c, [w * o for w, o in zip(ws, outs)]) * inv_den
    merged = (jax.nn.sigmoid(g0_ref[...]) * _dot(yn_ref[...], wn_ref[...])
              + jax.nn.sigmoid(g1_ref[...]) * _dot(y_dil.astype(BF16), wd_ref[...]))
    o_ref[...] = x_ref[...] + _dot(merged.astype(BF16), wo_ref[...])


def _out_proj(x2, y_nsa, outs, lses, dils, pa, w_bn, w_bd, w_o, tm=512):
    m, d = x2.shape
    row = lambda w: pl.BlockSpec((tm, w), lambda i: (i, 0))
    dil_row = lambda dil: pl.BlockSpec((tm // dil, dil * DIL_W), lambda i: (i, 0))
    whole = lambda a: pl.BlockSpec(a.shape, lambda i: (0, 0))
    n_staged = 2 * sum(dil != 1 for dil in dils)
    return pl.pallas_call(
        functools.partial(_out_kernel, dils=tuple(dils)),
        grid=(m // tm,),
        in_specs=[row(d), row(NSA_Q)] + [dil_row(dil) for dil in dils] * 2
        + [pl.BlockSpec((tm, d), lambda i: (i, 0)), pl.BlockSpec((tm, d), lambda i: (i, 1))]
        + [whole(w_bn), whole(w_bd), whole(w_o)],
        out_specs=row(d),
        out_shape=jax.ShapeDtypeStruct((m, d), F32),
        scratch_shapes=[pltpu.VMEM((DIL_W // LANES, tm, LANES), F32)] * n_staged,
        compiler_params=_params(("parallel",)),
        name="out_proj",
    )(x2, y_nsa, *outs, *lses, pa, pa, w_bn, w_bd, w_o)


def _mixer(x2, pos_col, inv_lane, b, t, norm_mix, w_in, cmp_pos, cmp_w1, cmp_w2, w_branch_nsa, w_branch_dil, w_out):
    d = x2.shape[1]
    dh = HEAD_DIM
    o_q, o_kv, o_gn = 0, NSA_Q, NSA_Q + 6 * NSA_KV
    o_d = o_gn + 3 * NSA_HEADS
    o_gm = o_d + 3 * DIL_GROUPS * DIL_W
    kv = lambda i: w_in[:, o_kv + i * NSA_KV:o_kv + (i + 1) * NSA_KV]
    dd = lambda i: w_in[:, o_d + i * DIL_GROUPS * DIL_W:o_d + (i + 1) * DIL_GROUPS * DIL_W]
    w_q = w_in[:, o_q:o_q + NSA_Q]
    gn_pad = jnp.zeros((d, 2 * LANES - 3 * NSA_HEADS), w_in.dtype)
    w_a = jnp.concatenate([w_in[:, o_gm:o_gm + 2 * d], w_q, kv(0), kv(1), w_in[:, o_gn:o_gn + 3 * NSA_HEADS], gn_pad],
                          axis=1).astype(BF16)
    w_b = jnp.concatenate([w_q, kv(2), kv(4), dd(0), dd(1)], axis=1).astype(BF16)
    w_c = jnp.concatenate([kv(3), kv(5), dd(2)], axis=1).astype(BF16)
    dils = [dil for _, dil in DIL_PAIRS]
    n_r = NSA_Q + 2 * NSA_KV
    seg_b = ((0, n_r, 1),) + tuple((n_r + i * DIL_W, DIL_W, dil) for i, dil in enumerate(dils)) \
        + tuple((n_r + (DIL_GROUPS + i) * DIL_W, DIL_W, dil) for i, dil in enumerate(dils))
    seg_c = ((0, NSA_KV, 0), (NSA_KV, NSA_KV, 0)) + tuple((2 * NSA_KV + i * DIL_W, DIL_W, dil) for i, dil in enumerate(dils))

    pa, = _proj(x2, norm_mix, w_a, F32, name="proj_f32")
    rb, *qk_d = _proj(x2, norm_mix, w_b, BF16, segments=seg_b, pos=pos_col, inv_lane=inv_lane, name="proj_rope")
    vst, vwt, *v_d = _proj(x2, norm_mix, w_c, BF16, segments=seg_c, name="proj_bf16")

    pos = jnp.concatenate([cmp_pos] * NSA_KV_GROUPS, axis=-1)
    w1 = _both_groups(cmp_w1.reshape(2, CMP_BLOCK, dh, -1)).astype(BF16)
    kc, vct = _compress(pa, b, (2 * d + NSA_Q) // LANES, pos, w1, _both_groups(cmp_w2).astype(BF16))

    pa3 = pa.reshape(b, t, -1)
    r3 = rb.reshape(b, t, -1)
    expand = jnp.asarray(np.arange(t)[:, None] // SLC_BLOCK == np.arange(LANES)[None, :], BF16)
    y_nsa = _nsa(pa3, r3, kc, vct, vst, vwt, expand,
                 qp_blk=2 * d // NSA_Q, gate_blk=(2 * d + NSA_Q + 2 * NSA_KV) // LANES,
                 ks_blk=NSA_Q // LANES, kw_blk=NSA_Q // LANES + 1)

    outs, lses = [], []
    for gidx, (window, dil) in enumerate(DIL_PAIRS):
        o, lse = _dilated(qk_d[gidx], qk_d[DIL_GROUPS + gidx], v_d[gidx], b, dil, window // dil)
        outs.append(o)
        lses.append(lse)

    return _out_proj(x2, y_nsa.reshape(b * t, NSA_Q), outs, lses, dils, pa,
                     w_branch_nsa.astype(BF16), w_branch_dil.astype(BF16), w_out.astype(BF16))


def kernel(x, positions, norm_ffn1, ffn1_w_in, ffn1_w_out, norm_mix, w_in, cmp_pos, cmp_w1, cmp_w2,
           w_branch_nsa, w_branch_dil, w_out, norm_ffn2, ffn2_w_in, ffn2_w_out, norm_final):
    b, t, d = x.shape
    depth = norm_ffn1.shape[0]
    x2 = x.reshape(b * t, d)
    pos_col = positions.reshape(b * t, 1).astype(jnp.int32)
    inv = jnp.power(ROPE_THETA, -jnp.arange(0, ROPE_DIM, 2, dtype=F32) / ROPE_DIM)
    inv_head = jnp.concatenate([inv, inv, jnp.zeros((HEAD_DIM - ROPE_DIM,), F32)])
    inv_lane = jnp.tile(inv_head, LANES // HEAD_DIM).reshape(1, LANES)
    for l in range(depth):
        last = l == depth - 1
        x2 = _ffn(x2, norm_ffn1[l], ffn1_w_in[l].astype(BF16), ffn1_w_out[l].astype(BF16))
        x2 = _mixer(x2, pos_col, inv_lane, b, t, norm_mix[l], w_in[l], cmp_pos[l], cmp_w1[l], cmp_w2[l],
                    w_branch_nsa[l], w_branch_dil[l], w_out[l])
        x2 = _ffn(x2, norm_ffn2[l], ffn2_w_in[l].astype(BF16), ffn2_w_out[l].astype(BF16),
                  final_gain=norm_final if last else None)
    if depth == 0:
        raise ValueError("depth must be positive")
    return x2.reshape(b, t, d)
```

```python
import functools

import numpy as np
import jax
import jax.numpy as jnp
from jax import lax
from jax.experimental import pallas as pl
from jax.experimental.pallas import tpu as pltpu

HEAD_DIM = 64
ROPE_DIM = HEAD_DIM // 4
ROPE_THETA = 500000.0
NORM_EPS = 1e-6

NSA_HEADS = 8
NSA_KV_GROUPS = 2
NSA_HPG = NSA_HEADS // NSA_KV_GROUPS
CMP_BLOCK = 32
CMP_STRIDE = 16
SLC_BLOCK = 64
SLC_TOPN = 16
WIN = 512
Q_BLOCK = 128
FORCE = 1e9
NEG = -1e30

DIL_PAIRS = ((128, 1), (512, 4), (2048, 16))
DIL_GROUPS = len(DIL_PAIRS)
DIL_HEADS = 4
DIL_BLOCK = 128

NSA_Q = NSA_HEADS * HEAD_DIM
NSA_KV = NSA_KV_GROUPS * HEAD_DIM
DIL_W = DIL_HEADS * HEAD_DIM

LANES = 128
SEL_KEY_TILE = 1024
VAL_ROWS = HEAD_DIM + 16
VMEM_LIMIT = 56 * 1024 * 1024

F32 = jnp.float32
BF16 = jnp.bfloat16
SCALE = HEAD_DIM ** -0.5
LOG2E = 1.4426950408889634


def _dot(a, b):
    return jnp.dot(a, b, preferred_element_type=F32)


def _dot_nt(a, b):
    return lax.dot_general(a, b, (((1,), (1,)), ((), ())), preferred_element_type=F32)


def _rmsnorm(x, g):
    return x * lax.rsqrt(jnp.mean(x * x, axis=-1, keepdims=True) + NORM_EPS) * g


def _params(sem):
    return pltpu.CompilerParams(dimension_semantics=sem, vmem_limit_bytes=VMEM_LIMIT)


def _ffn_kernel(*refs, final):
    x_ref, g_ref, wi_ref, wo_ref = refs[:4]
    o_ref = refs[-1]
    dff = wo_ref.shape[0]
    x = x_ref[...]
    h = _rmsnorm(x, g_ref[...]).astype(BF16)
    u = _dot(h, wi_ref[:, :dff])
    v = _dot(h, wi_ref[:, dff:])
    a = (u * jax.nn.sigmoid(u)) * v
    y = x + 0.5 * _dot(a.astype(BF16), wo_ref[...])
    if final:
        y = _rmsnorm(y, refs[4][...])
    o_ref[...] = y


def _ffn(x2, gain, w_in, w_out, final_gain=None, tm=512):
    m, d = x2.shape
    final = final_gain is not None
    whole = lambda a: pl.BlockSpec(a.shape, lambda i: (0, 0))
    in_specs = [pl.BlockSpec((tm, d), lambda i: (i, 0)), pl.BlockSpec((1, d), lambda i: (0, 0)), whole(w_in), whole(w_out)]
    args = [x2, gain.reshape(1, d), w_in, w_out]
    if final:
        in_specs.append(pl.BlockSpec((1, d), lambda i: (0, 0)))
        args.append(final_gain.reshape(1, d))
    return pl.pallas_call(
        functools.partial(_ffn_kernel, final=final),
        grid=(m // tm,),
        in_specs=in_specs,
        out_specs=pl.BlockSpec((tm, d), lambda i: (i, 0)),
        out_shape=jax.ShapeDtypeStruct((m, d), F32),
        compiler_params=_params(("parallel",)),
        name="ffn_final" if final else "ffn",
    )(*args)


def _rope_coeffs(pos_col, inv_lane):
    ang = pos_col.astype(F32) * inv_lane
    c, s = jnp.cos(ang), jnp.sin(ang)
    d = lax.broadcasted_iota(jnp.int32, ang.shape, 1) & (HEAD_DIM - 1)
    half = ROPE_DIM // 2
    coef_c = jnp.where(d < ROPE_DIM, c, 1.0)
    coef_hi = jnp.where(d < half, -s, 0.0)
    coef_lo = jnp.where((d >= half) & (d < ROPE_DIM), s, 0.0)
    return coef_c, coef_hi, coef_lo


def _proj_kernel(*refs, rope, segments):
    n_in = 5 if rope else 3
    x_ref, g_ref, w_ref = refs[:3]
    o_refs = refs[n_in:n_in + len(segments)]
    h = _rmsnorm(x_ref[...], g_ref[...]).astype(BF16)
    y = _dot(h, w_ref[...])
    if len(refs) == n_in + len(segments):
        o_refs[0][...] = y.astype(o_refs[0].dtype)
        return
    y_scr = refs[-1]
    if rope:
        coef_c, coef_hi, coef_lo = _rope_coeffs(refs[3][...], refs[4][...])
        half = ROPE_DIM // 2
        for t in range(y.shape[1] // LANES):
            yt = y[:, t * LANES:(t + 1) * LANES]
            hi = pltpu.roll(yt, LANES - half, axis=1)
            lo = pltpu.roll(yt, half, axis=1)
            y_scr[t] = yt * coef_c + hi * coef_hi + lo * coef_lo
    else:
        for t in range(y.shape[1] // LANES):
            y_scr[t] = y[:, t * LANES:(t + 1) * LANES]
    tm = y.shape[0]
    for o_ref, (c0, w, dil) in zip(o_refs, segments):
        if dil == 0:
            yt = y_scr[c0 // LANES].T
            pad = jnp.where(lax.broadcasted_iota(jnp.int32, (VAL_ROWS - HEAD_DIM, tm), 0) == 0, 1.0, 0.0)
            for g in range(NSA_KV_GROUPS):
                rows = jnp.concatenate([yt[g * HEAD_DIM:(g + 1) * HEAD_DIM], pad], axis=0)
                o_ref[g] = rows.astype(o_ref.dtype)
            continue
        for k in range(w // LANES):
            src = c0 // LANES + k
            for res in range(dil):
                rows = y_scr[src] if dil == 1 else y_scr[src, pl.ds(res, tm // dil, stride=dil), :]
                o_ref[:, res * w + k * LANES:res * w + (k + 1) * LANES] = rows.astype(o_ref.dtype)


def _proj(x2, gain, w, out_dtype, segments=None, pos=None, inv_lane=None, tm=512, name="proj"):
    m, d = x2.shape
    n = w.shape[1]
    rope = pos is not None
    segments = segments or ((0, n, 1),)
    staged = rope or len(segments) > 1 or segments[0][2] != 1
    in_specs = [
        pl.BlockSpec((tm, d), lambda i: (i, 0)),
        pl.BlockSpec((1, d), lambda i: (0, 0)),
        pl.BlockSpec((d, n), lambda i: (0, 0)),
    ]
    args = [x2, gain.reshape(1, d), w]
    if rope:
        in_specs += [pl.BlockSpec((tm, 1), lambda i: (i, 0)), pl.BlockSpec((1, LANES), lambda i: (0, 0))]
        args += [pos, inv_lane]
    outs = pl.pallas_call(
        functools.partial(_proj_kernel, rope=rope, segments=segments),
        grid=(m // tm,),
        in_specs=in_specs,
        out_specs=[pl.BlockSpec((tm // dil, dil * wd), lambda i: (i, 0)) if dil else
                   pl.BlockSpec((NSA_KV_GROUPS, VAL_ROWS, tm), lambda i: (0, 0, i)) for _, wd, dil in segments],
        out_shape=[jax.ShapeDtypeStruct((m // dil, dil * wd) if dil else (NSA_KV_GROUPS, VAL_ROWS, m), out_dtype)
                   for _, wd, dil in segments],
        scratch_shapes=[pltpu.VMEM((n // LANES, tm, LANES), F32)] if staged else [],
        compiler_params=_params(("parallel",)),
        name=name,
    )(*args)
    return outs


def _compress_kernel(ksrc_ref, vsrc_ref, pos_ref, w1_ref, w2_ref, kc_ref, vct_ref):
    n_seg = kc_ref.shape[1]
    outs = []
    for c, src in enumerate((ksrc_ref, vsrc_ref)):
        first = second = None
        for p in range(CMP_STRIDE):
            tok = src[pl.ds(p, n_seg, stride=CMP_STRIDE), :]
            a = _dot((tok + pos_ref[c, p:p + 1, :]).astype(BF16), w1_ref[c, p])
            b = _dot((tok + pos_ref[c, CMP_STRIDE + p:CMP_STRIDE + p + 1, :]).astype(BF16), w1_ref[c, CMP_STRIDE + p])
            first = a if first is None else first + a
            second = b if second is None else second + b
        pre = first + pltpu.roll(second, n_seg - 1, axis=0)
        hid = pre * jax.nn.sigmoid(pre)
        outs.append(_dot(hid.astype(BF16), w2_ref[c]))
    kc_ref[0] = outs[0].astype(kc_ref.dtype)
    vct_ref[0] = outs[1].T.astype(vct_ref.dtype)


def _compress(pa, b, ksrc_blk, pos, w1, w2):
    t = pa.shape[0] // b
    n_seg = t // CMP_STRIDE
    whole = lambda a: pl.BlockSpec(a.shape, lambda i: (0,) * a.ndim)
    return pl.pallas_call(
        _compress_kernel,
        grid=(b,),
        in_specs=[pl.BlockSpec((t, LANES), lambda i: (i, ksrc_blk)), pl.BlockSpec((t, LANES), lambda i: (i, ksrc_blk + 1)),
                  whole(pos), whole(w1), whole(w2)],
        out_specs=[pl.BlockSpec((1, n_seg, LANES), lambda i: (i, 0, 0)), pl.BlockSpec((1, LANES, n_seg), lambda i: (i, 0, 0))],
        out_shape=[jax.ShapeDtypeStruct((b, n_seg, LANES), BF16), jax.ShapeDtypeStruct((b, LANES, n_seg), BF16)],
        compiler_params=_params(("parallel",)),
        name="compress",
    )(pa, pa, pos, w1, w2)


def _both_groups(w):
    z = jnp.zeros_like(w)
    return jnp.concatenate([jnp.concatenate([w, z], axis=-1), jnp.concatenate([z, w], axis=-1)], axis=-2)


MASK_BIAS = NEG
M_INIT = -1e29


def _stack_heads_t(q, g):
    zeros = jnp.zeros((HEAD_DIM, q.shape[0]), F32)
    parts = []
    for h in range(NSA_HPG):
        hg = g * NSA_HPG + h
        tile_t = (q[:, (hg // 2) * LANES:(hg // 2 + 1) * LANES] * (SCALE * LOG2E)).T
        feat = tile_t[(hg % 2) * HEAD_DIM:(hg % 2 + 1) * HEAD_DIM]
        parts.append(jnp.concatenate([feat, zeros] if g == 0 else [zeros, feat], axis=0))
    return jnp.concatenate(parts, axis=1).astype(BF16)


def _softmax_t(s, bias, hg):
    sb = s + jnp.concatenate([bias] * hg, axis=1)
    m = jnp.maximum(jnp.max(sb, axis=0, keepdims=True), M_INIT)
    e = jnp.exp2(sb - m)
    d = jnp.sum(e, axis=0, keepdims=True)
    return e, 1.0 / jnp.where(d > 0, d, 1.0)


def _nsa_kernel(qp_ref, qr_ref, gate_ref, kc_ref, vct_ref, ks_ref, vst_ref, kw_ref, vwt_ref, et_ref,
                o_ref, cnt_scr, sa_scr, sb_scr, oc_scr, psum_scr, *, n_sel, win_keys):
    qb = pl.program_id(1)
    t0 = qb * Q_BLOCK
    nq = Q_BLOCK
    hg = NSA_HPG
    tk = SEL_KEY_TILE
    n_cmp_pad = kc_ref.shape[1]
    t_row = t0 + lax.broadcasted_iota(jnp.int32, (1, nq), 1)
    blk = lax.broadcasted_iota(jnp.int32, (LANES, nq), 0)
    sub8 = lax.broadcasted_iota(jnp.int32, (8, nq), 0)

    q_plain = qp_ref[0]
    q_rope = qr_ref[0].astype(F32)
    gates_t = jax.nn.sigmoid(gate_ref[0]).T

    cmp_chunk = min(LANES, n_cmp_pad)
    n_cmp_rows = (qb + 1) * (Q_BLOCK // CMP_STRIDE)
    n_cmp_chunks = jnp.minimum((n_cmp_rows + cmp_chunk - 1) // cmp_chunk, n_cmp_pad // cmp_chunk)
    cur = t_row >> 6
    forced = (blk == 0) | (blk == cur) | (blk == cur - 1)
    valid = blk <= cur
    n_blocks = (qb + 1) * (Q_BLOCK // SLC_BLOCK)

    win_start = pl.multiple_of(jnp.maximum(t0 - WIN, 0), Q_BLOCK)
    diff_w = t_row - (win_start + lax.broadcasted_iota(jnp.int32, (win_keys, nq), 0))
    bias_w = jnp.where((diff_w >= 0) & (diff_w < WIN), 0.0, MASK_BIAS)

    first_blk = qb * (Q_BLOCK // SLC_BLOCK)
    n_sweep = (t0 + tk - 1) // tk
    n_key_tiles = ks_ref.shape[1] // tk
    bias_own = jnp.where(lax.broadcasted_iota(jnp.int32, (nq, nq), 0) <= lax.broadcasted_iota(jnp.int32, (nq, nq), 1),
                         0.0, MASK_BIAS)

    for g in range(NSA_KV_GROUPS):
        qt_plain = _stack_heads_t(q_plain, g)
        qt_rope = _stack_heads_t(q_rope, g)
        feat = slice(g * HEAD_DIM, (g + 1) * HEAD_DIM)

        for c in range(1, n_cmp_pad // cmp_chunk + 1):
            @pl.when(n_cmp_chunks == c)
            def _(c=c):
                rows = c * cmp_chunk
                cmp_end = lax.broadcasted_iota(jnp.int32, (rows, nq), 0) * CMP_STRIDE + (CMP_BLOCK - 1)
                bias_c = jnp.where(cmp_end <= t_row, 0.0, MASK_BIAS)
                e_c, r_c = _softmax_t(_dot(kc_ref[0, 0:rows, :], qt_plain), bias_c, hg)
                p_c = e_c * r_c
                oc_scr[...] = _dot(vct_ref[0, :, 0:rows], p_c.astype(BF16))
                p_sum = p_c[:, 0:nq]
                for h in range(1, hg):
                    p_sum = p_sum + p_c[:, h * nq:(h + 1) * nq]
                psum_scr[0:rows, :] = p_sum
                if rows < psum_scr.shape[0]:
                    psum_scr[rows:, :] = jnp.zeros((psum_scr.shape[0] - rows, nq), F32)
        o_c = oc_scr[...]
        per = SLC_BLOCK // CMP_STRIDE
        parts = [psum_scr[pl.ds(k, LANES, stride=per), :] for k in range(per)]
        tail_prev = jnp.where(blk == 0, 0.0, pltpu.roll(parts[per - 1], 1, axis=0))
        imp = parts[0] + parts[1] + parts[2] + 0.5 * parts[3] + 0.5 * tail_prev
        imp = jnp.where(forced, FORCE, jnp.where(valid, imp, -FORCE))

        cnt_scr[...] = jnp.zeros_like(cnt_scr)
        quarter = LANES // 4
        for ig in range(LANES // 8):
            @pl.when((ig * 8 < n_blocks) & (n_blocks > n_sel))
            def _(ig=ig):
                for jq in range(4):
                    @pl.when(jq * quarter < n_blocks)
                    def _(jq=jq):
                        parts = []
                        for jv in range(jq * quarter // 8, (jq + 1) * quarter // 8):
                            x = imp[jv * 8:(jv + 1) * 8]
                            c = jnp.zeros((8, nq), jnp.int32)
                            for r in range(8):
                                row = imp[ig * 8 + r:ig * 8 + r + 1]
                                if jv > ig:
                                    ahead = row >= x
                                elif jv < ig:
                                    ahead = row > x
                                else:
                                    ahead = (row > x) | ((row == x) & (sub8 > r))
                                c = c + ahead.astype(jnp.int32)
                            parts.append(c)
                        cnt_scr[jq * quarter:(jq + 1) * quarter, :] += jnp.concatenate(parts, axis=0)
        picked = (cnt_scr[...] < n_sel) & (imp > -0.5 * FORCE)
        bias_blk = jnp.where(picked & (blk < first_blk), 0.0, MASK_BIAS).astype(BF16)
        q_aug = jnp.concatenate([qt_rope, jnp.concatenate([bias_blk] * hg, axis=1)], axis=0)

        def absorb(s, vt, carry):
            m, acc = carry
            m_new = jnp.maximum(m, jnp.max(s, axis=0, keepdims=True))
            e = jnp.exp2(s - m_new)
            return m_new, jnp.exp2(m - m_new) * acc + _dot(vt, e.astype(BF16))

        def key_tile(kt):
            return pl.multiple_of(jnp.minimum(kt, n_key_tiles - 1) * tk, tk)

        def scores_into(scr, kt):
            k0 = key_tile(kt)
            scr[...] = _dot(jnp.concatenate([ks_ref[0, pl.ds(k0, tk), :], et_ref[pl.ds(k0, tk), :]], axis=1), q_aug)

        def sweep_pair(i, carry):
            scores_into(sb_scr, 2 * i + 1)
            carry = absorb(sa_scr[...], vst_ref[g, :, pl.ds(key_tile(2 * i), tk)], carry)
            scores_into(sa_scr, 2 * i + 2)
            return absorb(sb_scr[...], vst_ref[g, :, pl.ds(key_tile(2 * i + 1), tk)], carry)

        init = (jnp.full((1, hg * nq), M_INIT, F32), jnp.zeros((VAL_ROWS, hg * nq), F32))
        scores_into(sa_scr, 0)
        carry = lax.fori_loop(0, n_sweep // 2, sweep_pair, init)
        last = n_sweep - 1
        carry = lax.cond(n_sweep % 2 == 1,
                         lambda c: absorb(sa_scr[...], vst_ref[g, :, pl.ds(key_tile(last), tk)], c),
                         lambda c: c, carry)
        own = pl.ds(pl.multiple_of(t0, nq), nq)
        s_own = _dot(ks_ref[0, own, :], qt_rope) + jnp.concatenate([bias_own] * hg, axis=1)
        _, acc_s = absorb(s_own, vst_ref[g, :, own], carry)
        o_s = acc_s[0:HEAD_DIM] * (1.0 / acc_s[HEAD_DIM:HEAD_DIM + 1])

        s_w = _dot(kw_ref[0, pl.ds(win_start, win_keys), :], qt_rope) + jnp.concatenate([bias_w] * hg, axis=1)
        e_w = jnp.exp2(s_w - jnp.max(s_w, axis=0, keepdims=True))
        acc_w = _dot(vwt_ref[g, :, pl.ds(win_start, win_keys)], e_w.astype(BF16))
        o_w = acc_w[0:HEAD_DIM] * (1.0 / acc_w[HEAD_DIM:HEAD_DIM + 1])

        mixes = []
        for h in range(hg):
            row = g * hg * 3 + h * 3
            cols = slice(h * nq, (h + 1) * nq)
            mixes.append(gates_t[row:row + 1] * o_c[feat, cols] + gates_t[row + 1:row + 2] * o_s[:, cols]
                         + gates_t[row + 2:row + 3] * o_w[:, cols])
        for hp in range(hg // 2):
            c = (g * hg) // 2 + hp
            pair = jnp.concatenate([mixes[2 * hp], mixes[2 * hp + 1]], axis=0)
            o_ref[0, :, c * LANES:(c + 1) * LANES] = pair.T.astype(o_ref.dtype)


def _nsa(pa3, r3, kc, vct, vst, vwt, expand, *, qp_blk, gate_blk, ks_blk, kw_blk):
    b, t, _ = pa3.shape
    n_cmp_pad = kc.shape[1]
    n_sel = min(SLC_TOPN, t // SLC_BLOCK)
    win_keys = min(WIN + Q_BLOCK, t)
    keys = lambda blk: pl.BlockSpec((1, t, LANES), lambda i, j, blk=blk: (i, 0, blk))
    vals_t = pl.BlockSpec((NSA_KV_GROUPS, VAL_ROWS, t), lambda i, j: (0, 0, i))
    return pl.pallas_call(
        functools.partial(_nsa_kernel, n_sel=n_sel, win_keys=win_keys),
        grid=(b, t // Q_BLOCK),
        in_specs=[
            pl.BlockSpec((1, Q_BLOCK, NSA_Q), lambda i, j: (i, j, qp_blk)),
            pl.BlockSpec((1, Q_BLOCK, NSA_Q), lambda i, j: (i, j, 0)),
            pl.BlockSpec((1, Q_BLOCK, LANES), lambda i, j: (i, j, gate_blk)),
            pl.BlockSpec((1, n_cmp_pad, LANES), lambda i, j: (i, 0, 0)),
            pl.BlockSpec((1, LANES, n_cmp_pad), lambda i, j: (i, 0, 0)),
            keys(ks_blk), vals_t, keys(kw_blk), vals_t,
            pl.BlockSpec((t, LANES), lambda i, j: (0, 0)),
        ],
        out_specs=pl.BlockSpec((1, Q_BLOCK, NSA_Q), lambda i, j: (i, j, 0)),
        out_shape=jax.ShapeDtypeStruct((b, t, NSA_Q), BF16),
        scratch_shapes=[pltpu.VMEM((LANES, Q_BLOCK), jnp.int32),
                        pltpu.VMEM((SEL_KEY_TILE, NSA_HPG * Q_BLOCK), F32), pltpu.VMEM((SEL_KEY_TILE, NSA_HPG * Q_BLOCK), F32),
                        pltpu.VMEM((LANES, NSA_HPG * Q_BLOCK), F32),
                        pltpu.VMEM((max(n_cmp_pad, LANES * SLC_BLOCK // CMP_STRIDE), Q_BLOCK), F32)],
        compiler_params=_params(("parallel", "arbitrary")),
        name="nsa",
    )(pa3, r3, pa3, kc, vct, r3, vst, r3, vwt, expand)


def _dil_kernel(q_ref, kp_ref, kc_ref, vp_ref, vc_ref, o_ref, lse_ref, *, span):
    ub = pl.program_id(2)
    nq = DIL_BLOCK
    lane = lax.broadcasted_iota(jnp.int32, (nq, LANES), 1)
    row = lax.broadcasted_iota(jnp.int32, (2 * nq, 2 * nq), 0) & (nq - 1)
    col = lax.broadcasted_iota(jnp.int32, (2 * nq, 2 * nq), 1)
    diff = row + nq - col
    band = (diff >= 0) & (diff <= span)
    bias = jnp.where(band, 0.0, MASK_BIAS)
    bias_first = jnp.where(band & ((col >= nq) | (ub > 0)), 0.0, MASK_BIAS)
    for j in range(q_ref.shape[0] // nq):
        rows = slice(j * nq, (j + 1) * nq)
        before = slice((j - 1) * nq, j * nq)
        for hp in range(DIL_HEADS // 2):
            cols = slice(hp * LANES, (hp + 1) * LANES)
            qt = (q_ref[rows, cols].astype(F32) * (SCALE * LOG2E)).astype(BF16)
            zero = jnp.zeros_like(qt)
            qs = jnp.concatenate([jnp.where(lane < HEAD_DIM, qt, zero), jnp.where(lane >= HEAD_DIM, qt, zero)], axis=0)
            k = jnp.concatenate([kp_ref[:, cols] if j == 0 else kc_ref[before, cols], kc_ref[rows, cols]], axis=0)
            v = jnp.concatenate([vp_ref[:, cols] if j == 0 else vc_ref[before, cols], vc_ref[rows, cols]], axis=0)
            s = _dot_nt(qs, k) + (bias_first if j == 0 else bias)
            m = jnp.max(s, axis=-1, keepdims=True)
            e = jnp.exp2(s - m)
            d = jnp.sum(e, axis=-1, keepdims=True)
            o = _dot(e.astype(BF16), v) * (1.0 / d)
            lse = (m + jnp.log2(d)) * (1.0 / LOG2E)
            o_ref[rows, cols] = jnp.where(lane < HEAD_DIM, o[:nq], o[nq:]).astype(o_ref.dtype)
            lse_ref[rows, cols] = jnp.where(lane < HEAD_DIM, lse[:nq], lse[nq:])


def _dilated(qd, kd, vd, b, dil, span):
    rows = qd.shape[0]
    ln = rows // b
    ub = min(ln, 4 * DIL_BLOCK)
    nub = ln // ub
    per = ub // DIL_BLOCK
    cur = pl.BlockSpec((ub, DIL_W), lambda i, r, u: (i * nub + u, r))
    prev = pl.BlockSpec((DIL_BLOCK, DIL_W), lambda i, r, u: (jnp.maximum((i * nub + u) * per - 1, 0), r))
    o, lse = pl.pallas_call(
        functools.partial(_dil_kernel, span=span),
        grid=(b, dil, nub),
        in_specs=[cur, prev, cur, prev, cur],
        out_specs=[cur, cur],
        out_shape=[jax.ShapeDtypeStruct(qd.shape, F32), jax.ShapeDtypeStruct(qd.shape, F32)],
        compiler_params=_params(("parallel", "parallel", "arbitrary")),
        name=f"dilated_{dil}",
    )(qd, kd, kd, vd, vd)
    return o, lse


def _out_kernel(*refs, dils):
    ng = len(dils)
    x_ref, yn_ref = refs[:2]
    o_refs, l_refs = refs[2:2 + ng], refs[2 + ng:2 + 2 * ng]
    g0_ref, g1_ref, wn_ref, wd_ref, wo_ref, o_ref = refs[2 + 2 * ng:8 + 2 * ng]
    scrs = list(refs[8 + 2 * ng:])

    def natural(ref, dil):
        if dil == 1:
            return ref[...]
        scr = scrs.pop(0)
        n = scr.shape[1] // dil
        for res in range(dil):
            for k in range(DIL_W // LANES):
                scr[k, pl.ds(res, n, stride=dil), :] = ref[:, res * DIL_W + k * LANES:res * DIL_W + (k + 1) * LANES]
        return jnp.concatenate([scr[k] for k in range(DIL_W // LANES)], axis=1)

    outs = [natural(r, d) for r, d in zip(o_refs, dils)]
    lses = [natural(r, d) for r, d in zip(l_refs, dils)]
    m = functools.reduce(jnp.maximum, lses)
    ws = [jnp.exp(l - m) for l in lses]
    inv_den = 1.0 / functools.reduce(lambda a, c: a + c, ws)
    y_dil = functools.reduce(lambda a, c: a + c, [w * o for w, o in zip(ws, outs)]) * inv_den
    merged = (jax.nn.sigmoid(g0_ref[...]) * _dot(yn_ref[...], wn_ref[...])
              + jax.nn.sigmoid(g1_ref[...]) * _dot(y_dil.astype(BF16), wd_ref[...]))
    o_ref[...] = x_ref[...] + _dot(merged.astype(BF16), wo_ref[...])


def _out_proj(x2, y_nsa, outs, lses, dils, pa, w_bn, w_bd, w_o, tm=512):
    m, d = x2.shape
    row = lambda w: pl.BlockSpec((tm, w), lambda i: (i, 0))
    dil_row = lambda dil: pl.BlockSpec((tm // dil, dil * DIL_W), lambda i: (i, 0))
    whole = lambda a: pl.BlockSpec(a.shape, lambda i: (0, 0))
    n_staged = 2 * sum(dil != 1 for dil in dils)
    return pl.pallas_call(
        functools.partial(_out_kernel, dils=tuple(dils)),
        grid=(m // tm,),
        in_specs=[row(d), row(NSA_Q)] + [dil_row(dil) for dil in dils] * 2
        + [pl.BlockSpec((tm, d), lambda i: (i, 0)), pl.BlockSpec((tm, d), lambda i: (i, 1))]
        + [whole(w_bn), whole(w_bd), whole(w_o)],
        out_specs=row(d),
        out_shape=jax.ShapeDtypeStruct((m, d), F32),
        scratch_shapes=[pltpu.VMEM((DIL_W // LANES, tm, LANES), F32)] * n_staged,
        compiler_params=_params(("parallel",)),
        name="out_proj",
    )(x2, y_nsa, *outs, *lses, pa, pa, w_bn, w_bd, w_o)


def _mixer(x2, pos_col, inv_lane, b, t, norm_mix, w_in, cmp_pos, cmp_w1, cmp_w2, w_branch_nsa, w_branch_dil, w_out):
    d = x2.shape[1]
    dh = HEAD_DIM
    o_q, o_kv, o_gn = 0, NSA_Q, NSA_Q + 6 * NSA_KV
    o_d = o_gn + 3 * NSA_HEADS
    o_gm = o_d + 3 * DIL_GROUPS * DIL_W
    kv = lambda i: w_in[:, o_kv + i * NSA_KV:o_kv + (i + 1) * NSA_KV]
    dd = lambda i: w_in[:, o_d + i * DIL_GROUPS * DIL_W:o_d + (i + 1) * DIL_GROUPS * DIL_W]
    w_q = w_in[:, o_q:o_q + NSA_Q]
    gn_pad = jnp.zeros((d, 2 * LANES - 3 * NSA_HEADS), w_in.dtype)
    w_a = jnp.concatenate([w_in[:, o_gm:o_gm + 2 * d], w_q, kv(0), kv(1), w_in[:, o_gn:o_gn + 3 * NSA_HEADS], gn_pad],
                          axis=1).astype(BF16)
    w_b = jnp.concatenate([w_q, kv(2), kv(4), dd(0), dd(1)], axis=1).astype(BF16)
    w_c = jnp.concatenate([kv(3), kv(5), dd(2)], axis=1).astype(BF16)
    dils = [dil for _, dil in DIL_PAIRS]
    n_r = NSA_Q + 2 * NSA_KV
    seg_b = ((0, n_r, 1),) + tuple((n_r + i * DIL_W, DIL_W, dil) for i, dil in enumerate(dils)) \
        + tuple((n_r + (DIL_GROUPS + i) * DIL_W, DIL_W, dil) for i, dil in enumerate(dils))
    seg_c = ((0, NSA_KV, 0), (NSA_KV, NSA_KV, 0)) + tuple((2 * NSA_KV + i * DIL_W, DIL_W, dil) for i, dil in enumerate(dils))

    pa, = _proj(x2, norm_mix, w_a, F32, name="proj_f32")
    rb, *qk_d = _proj(x2, norm_mix, w_b, BF16, segments=seg_b, pos=pos_col, inv_lane=inv_lane, name="proj_rope")
    vst, vwt, *v_d = _proj(x2, norm_mix, w_c, BF16, segments=seg_c, name="proj_bf16")

    pos = jnp.concatenate([cmp_pos] * NSA_KV_GROUPS, axis=-1)
    w1 = _both_groups(cmp_w1.reshape(2, CMP_BLOCK, dh, -1)).astype(BF16)
    kc, vct = _compress(pa, b, (2 * d + NSA_Q) // LANES, pos, w1, _both_groups(cmp_w2).astype(BF16))

    pa3 = pa.reshape(b, t, -1)
    r3 = rb.reshape(b, t, -1)
    expand = jnp.asarray(np.arange(t)[:, None] // SLC_BLOCK == np.arange(LANES)[None, :], BF16)
    y_nsa = _nsa(pa3, r3, kc, vct, vst, vwt, expand,
                 qp_blk=2 * d // NSA_Q, gate_blk=(2 * d + NSA_Q + 2 * NSA_KV) // LANES,
                 ks_blk=NSA_Q // LANES, kw_blk=NSA_Q // LANES + 1)

    outs, lses = [], []
    for gidx, (window, dil) in enumerate(DIL_PAIRS):
        o, lse = _dilated(qk_d[gidx], qk_d[DIL_GROUPS + gidx], v_d[gidx], b, dil, window // dil)
        outs.append(o)
        lses.append(lse)

    return _out_proj(x2, y_nsa.reshape(b * t, NSA_Q), outs, lses, dils, pa,
                     w_branch_nsa.astype(BF16), w_branch_dil.astype(BF16), w_out.astype(BF16))


def kernel(x, positions, norm_ffn1, ffn1_w_in, ffn1_w_out, norm_mix, w_in, cmp_pos, cmp_w1, cmp_w2,
           w_branch_nsa, w_branch_dil, w_out, norm_ffn2, ffn2_w_in, ffn2_w_out, norm_final):
    b, t, d = x.shape
    depth = norm_ffn1.shape[0]
    x2 = x.reshape(b * t, d)
    pos_col = positions.reshape(b * t, 1).astype(jnp.int32)
    inv = jnp.power(ROPE_THETA, -jnp.arange(0, ROPE_DIM, 2, dtype=F32) / ROPE_DIM)
    inv_head = jnp.concatenate([inv, inv, jnp.zeros((HEAD_DIM - ROPE_DIM,), F32)])
    inv_lane = jnp.tile(inv_head, LANES // HEAD_DIM).reshape(1, LANES)
    for l in range(depth):
        last = l == depth - 1
        x2 = _ffn(x2, norm_ffn1[l], ffn1_w_in[l].astype(BF16), ffn1_w_out[l].astype(BF16))
        x2 = _mixer(x2, pos_col, inv_lane, b, t, norm_mix[l], w_in[l], cmp_pos[l], cmp_w1[l], cmp_w2[l],
                    w_branch_nsa[l], w_branch_dil[l], w_out[l])
        x2 = _ffn(x2, norm_ffn2[l], ffn2_w_in[l].astype(BF16), ffn2_w_out[l].astype(BF16),
                  final_gain=norm_final if last else None)
    if depth == 0:
        raise ValueError("depth must be positive")
    return x2.reshape(b, t, d)
```

```python
import functools

import numpy as np
import jax
import jax.numpy as jnp
from jax import lax
from jax.experimental import pallas as pl
from jax.experimental.pallas import tpu as pltpu

HEAD_DIM = 64
ROPE_DIM = HEAD_DIM // 4
ROPE_THETA = 500000.0
NORM_EPS = 1e-6

NSA_HEADS = 8
NSA_KV_GROUPS = 2
NSA_HPG = NSA_HEADS // NSA_KV_GROUPS
CMP_BLOCK = 32
CMP_STRIDE = 16
SLC_BLOCK = 64
SLC_TOPN = 16
WIN = 512
Q_BLOCK = 128
FORCE = 1e9
N_FORCED = 3
NEG = -1e30

DIL_PAIRS = ((128, 1), (512, 4), (2048, 16))
DIL_GROUPS = len(DIL_PAIRS)
DIL_HEADS = 4
DIL_BLOCK = 128

NSA_Q = NSA_HEADS * HEAD_DIM
NSA_KV = NSA_KV_GROUPS * HEAD_DIM
DIL_W = DIL_HEADS * HEAD_DIM

LANES = 128
SEL_KEY_TILE = 1024
VAL_ROWS = HEAD_DIM + 16
VMEM_LIMIT = 56 * 1024 * 1024

F32 = jnp.float32
BF16 = jnp.bfloat16
SCALE = HEAD_DIM ** -0.5
LOG2E = 1.4426950408889634


def _dot(a, b):
    return jnp.dot(a, b, preferred_element_type=F32)


def _dot_nt(a, b):
    return lax.dot_general(a, b, (((1,), (1,)), ((), ())), preferred_element_type=F32)


def _rmsnorm(x, g):
    return x * lax.rsqrt(jnp.mean(x * x, axis=-1, keepdims=True) + NORM_EPS) * g


def _params(sem):
    return pltpu.CompilerParams(dimension_semantics=sem, vmem_limit_bytes=VMEM_LIMIT)


def _ffn_kernel(*refs, final):
    x_ref, g_ref, wi_ref, wo_ref = refs[:4]
    o_ref = refs[-1]
    dff = wo_ref.shape[0]
    x = x_ref[...]
    h = _rmsnorm(x, g_ref[...]).astype(BF16)
    u = _dot(h, wi_ref[:, :dff])
    v = _dot(h, wi_ref[:, dff:])
    a = (u * jax.nn.sigmoid(u)) * v
    y = x + 0.5 * _dot(a.astype(BF16), wo_ref[...])
    if final:
        y = _rmsnorm(y, refs[4][...])
    o_ref[...] = y


def _ffn(x2, gain, w_in, w_out, final_gain=None, tm=512):
    m, d = x2.shape
    final = final_gain is not None
    whole = lambda a: pl.BlockSpec(a.shape, lambda i: (0, 0))
    in_specs = [pl.BlockSpec((tm, d), lambda i: (i, 0)), pl.BlockSpec((1, d), lambda i: (0, 0)), whole(w_in), whole(w_out)]
    args = [x2, gain.reshape(1, d), w_in, w_out]
    if final:
        in_specs.append(pl.BlockSpec((1, d), lambda i: (0, 0)))
        args.append(final_gain.reshape(1, d))
    return pl.pallas_call(
        functools.partial(_ffn_kernel, final=final),
        grid=(m // tm,),
        in_specs=in_specs,
        out_specs=pl.BlockSpec((tm, d), lambda i: (i, 0)),
        out_shape=jax.ShapeDtypeStruct((m, d), F32),
        compiler_params=_params(("parallel",)),
        name="ffn_final" if final else "ffn",
    )(*args)


def _rope_coeffs(pos_col, inv_lane):
    ang = pos_col.astype(F32) * inv_lane
    c, s = jnp.cos(ang), jnp.sin(ang)
    d = lax.broadcasted_iota(jnp.int32, ang.shape, 1) & (HEAD_DIM - 1)
    half = ROPE_DIM // 2
    coef_c = jnp.where(d < ROPE_DIM, c, 1.0)
    coef_hi = jnp.where(d < half, -s, 0.0)
    coef_lo = jnp.where((d >= half) & (d < ROPE_DIM), s, 0.0)
    return coef_c, coef_hi, coef_lo


def _proj_kernel(*refs, rope, segments):
    n_in = 5 if rope else 3
    x_ref, g_ref, w_ref = refs[:3]
    o_refs = refs[n_in:n_in + len(segments)]
    h = _rmsnorm(x_ref[...], g_ref[...]).astype(BF16)
    y = _dot(h, w_ref[...])
    if len(refs) == n_in + len(segments):
        o_refs[0][...] = y.astype(o_refs[0].dtype)
        return
    y_scr = refs[-1]
    if rope:
        coef_c, coef_hi, coef_lo = _rope_coeffs(refs[3][...], refs[4][...])
        half = ROPE_DIM // 2
        for t in range(y.shape[1] // LANES):
            yt = y[:, t * LANES:(t + 1) * LANES]
            hi = pltpu.roll(yt, LANES - half, axis=1)
            lo = pltpu.roll(yt, half, axis=1)
            y_scr[t] = yt * coef_c + hi * coef_hi + lo * coef_lo
    else:
        for t in range(y.shape[1] // LANES):
            y_scr[t] = y[:, t * LANES:(t + 1) * LANES]
    tm = y.shape[0]
    for o_ref, (c0, w, dil) in zip(o_refs, segments):
        if dil == 0:
            yt = y_scr[c0 // LANES].T
            pad = jnp.where(lax.broadcasted_iota(jnp.int32, (VAL_ROWS - HEAD_DIM, tm), 0) == 0, 1.0, 0.0)
            for g in range(NSA_KV_GROUPS):
                rows = jnp.concatenate([yt[g * HEAD_DIM:(g + 1) * HEAD_DIM], pad], axis=0)
                o_ref[g] = rows.astype(o_ref.dtype)
            continue
        for k in range(w // LANES):
            src = c0 // LANES + k
            for res in range(dil):
                rows = y_scr[src] if dil == 1 else y_scr[src, pl.ds(res, tm // dil, stride=dil), :]
                o_ref[:, res * w + k * LANES:res * w + (k + 1) * LANES] = rows.astype(o_ref.dtype)


def _proj(x2, gain, w, out_dtype, segments=None, pos=None, inv_lane=None, tm=512, name="proj"):
    m, d = x2.shape
    n = w.shape[1]
    rope = pos is not None
    segments = segments or ((0, n, 1),)
    staged = rope or len(segments) > 1 or segments[0][2] != 1
    in_specs = [
        pl.BlockSpec((tm, d), lambda i: (i, 0)),
        pl.BlockSpec((1, d), lambda i: (0, 0)),
        pl.BlockSpec((d, n), lambda i: (0, 0)),
    ]
    args = [x2, gain.reshape(1, d), w]
    if rope:
        in_specs += [pl.BlockSpec((tm, 1), lambda i: (i, 0)), pl.BlockSpec((1, LANES), lambda i: (0, 0))]
        args += [pos, inv_lane]
    outs = pl.pallas_call(
        functools.partial(_proj_kernel, rope=rope, segments=segments),
        grid=(m // tm,),
        in_specs=in_specs,
        out_specs=[pl.BlockSpec((tm // dil, dil * wd), lambda i: (i, 0)) if dil else
                   pl.BlockSpec((NSA_KV_GROUPS, VAL_ROWS, tm), lambda i: (0, 0, i)) for _, wd, dil in segments],
        out_shape=[jax.ShapeDtypeStruct((m // dil, dil * wd) if dil else (NSA_KV_GROUPS, VAL_ROWS, m), out_dtype)
                   for _, wd, dil in segments],
        scratch_shapes=[pltpu.VMEM((n // LANES, tm, LANES), F32)] if staged else [],
        compiler_params=_params(("parallel",)),
        name=name,
    )(*args)
    return outs


def _compress_kernel(ksrc_ref, vsrc_ref, pos_ref, w1_ref, w2_ref, kc_ref, vct_ref):
    n_seg = kc_ref.shape[1]
    outs = []
    for c, src in enumerate((ksrc_ref, vsrc_ref)):
        first = second = None
        for p in range(CMP_STRIDE):
            tok = src[pl.ds(p, n_seg, stride=CMP_STRIDE), :]
            a = _dot((tok + pos_ref[c, p:p + 1, :]).astype(BF16), w1_ref[c, p])
            b = _dot((tok + pos_ref[c, CMP_STRIDE + p:CMP_STRIDE + p + 1, :]).astype(BF16), w1_ref[c, CMP_STRIDE + p])
            first = a if first is None else first + a
            second = b if second is None else second + b
        pre = first + pltpu.roll(second, n_seg - 1, axis=0)
        hid = pre * jax.nn.sigmoid(pre)
        outs.append(_dot(hid.astype(BF16), w2_ref[c]))
    kc_ref[0] = outs[0].astype(kc_ref.dtype)
    vct_ref[0] = outs[1].T.astype(vct_ref.dtype)


def _compress(pa, b, ksrc_blk, pos, w1, w2):
    t = pa.shape[0] // b
    n_seg = t // CMP_STRIDE
    whole = lambda a: pl.BlockSpec(a.shape, lambda i: (0,) * a.ndim)
    return pl.pallas_call(
        _compress_kernel,
        grid=(b,),
        in_specs=[pl.BlockSpec((t, LANES), lambda i: (i, ksrc_blk)), pl.BlockSpec((t, LANES), lambda i: (i, ksrc_blk + 1)),
                  whole(pos), whole(w1), whole(w2)],
        out_specs=[pl.BlockSpec((1, n_seg, LANES), lambda i: (i, 0, 0)), pl.BlockSpec((1, LANES, n_seg), lambda i: (i, 0, 0))],
        out_shape=[jax.ShapeDtypeStruct((b, n_seg, LANES), BF16), jax.ShapeDtypeStruct((b, LANES, n_seg), BF16)],
        compiler_params=_params(("parallel",)),
        name="compress",
    )(pa, pa, pos, w1, w2)


def _both_groups(w):
    z = jnp.zeros_like(w)
    return jnp.concatenate([jnp.concatenate([w, z], axis=-1), jnp.concatenate([z, w], axis=-1)], axis=-2)


MASK_BIAS = NEG
M_INIT = -1e29


def _stack_heads_t(q, g):
    zeros = jnp.zeros((HEAD_DIM, q.shape[0]), F32)
    parts = []
    for h in range(NSA_HPG):
        hg = g * NSA_HPG + h
        tile_t = (q[:, (hg // 2) * LANES:(hg // 2 + 1) * LANES] * (SCALE * LOG2E)).T
        feat = tile_t[(hg % 2) * HEAD_DIM:(hg % 2 + 1) * HEAD_DIM]
        parts.append(jnp.concatenate([feat, zeros] if g == 0 else [zeros, feat], axis=0))
    return jnp.concatenate(parts, axis=1).astype(BF16)


def _softmax_t(s, bias, hg):
    sb = s + jnp.concatenate([bias] * hg, axis=1)
    m = jnp.maximum(jnp.max(sb, axis=0, keepdims=True), M_INIT)
    e = jnp.exp2(sb - m)
    d = jnp.sum(e, axis=0, keepdims=True)
    return e, 1.0 / jnp.where(d > 0, d, 1.0)


def _nsa_kernel(qp_ref, qr_ref, gate_ref, kc_ref, vct_ref, ks_ref, vst_ref, kw_ref, vwt_ref, et_ref,
                o_ref, sa_scr, sb_scr, oc_scr, psum_scr, *, n_sel, win_keys):
    qb = pl.program_id(1)
    t0 = qb * Q_BLOCK
    nq = Q_BLOCK
    hg = NSA_HPG
    tk = SEL_KEY_TILE
    n_cmp_pad = kc_ref.shape[1]
    t_row = t0 + lax.broadcasted_iota(jnp.int32, (1, nq), 1)
    blk = lax.broadcasted_iota(jnp.int32, (LANES, nq), 0)

    q_plain = qp_ref[0]
    q_rope = qr_ref[0].astype(F32)
    gates_t = jax.nn.sigmoid(gate_ref[0]).T

    cmp_chunk = min(LANES, n_cmp_pad)
    n_cmp_rows = (qb + 1) * (Q_BLOCK // CMP_STRIDE)
    n_cmp_chunks = jnp.minimum((n_cmp_rows + cmp_chunk - 1) // cmp_chunk, n_cmp_pad // cmp_chunk)
    cur = t_row >> 6
    forced = (blk == 0) | (blk == cur) | (blk == cur - 1)
    valid = blk <= cur

    win_start = pl.multiple_of(jnp.maximum(t0 - WIN, 0), Q_BLOCK)
    diff_w = t_row - (win_start + lax.broadcasted_iota(jnp.int32, (win_keys, nq), 0))
    bias_w = jnp.where((diff_w >= 0) & (diff_w < WIN), 0.0, MASK_BIAS)

    first_blk = qb * (Q_BLOCK // SLC_BLOCK)
    n_sweep = (t0 + tk - 1) // tk
    n_key_tiles = ks_ref.shape[1] // tk
    bias_own = jnp.where(lax.broadcasted_iota(jnp.int32, (nq, nq), 0) <= lax.broadcasted_iota(jnp.int32, (nq, nq), 1),
                         0.0, MASK_BIAS)

    for g in range(NSA_KV_GROUPS):
        qt_plain = _stack_heads_t(q_plain, g)
        qt_rope = _stack_heads_t(q_rope, g)
        feat = slice(g * HEAD_DIM, (g + 1) * HEAD_DIM)

        for c in range(1, n_cmp_pad // cmp_chunk + 1):
            @pl.when(n_cmp_chunks == c)
            def _(c=c):
                rows = c * cmp_chunk
                cmp_end = lax.broadcasted_iota(jnp.int32, (rows, nq), 0) * CMP_STRIDE + (CMP_BLOCK - 1)
                bias_c = jnp.where(cmp_end <= t_row, 0.0, MASK_BIAS)
                e_c, r_c = _softmax_t(_dot(kc_ref[0, 0:rows, :], qt_plain), bias_c, hg)
                p_c = e_c * r_c
                oc_scr[...] = _dot(vct_ref[0, :, 0:rows], p_c.astype(BF16))
                p_sum = p_c[:, 0:nq]
                for h in range(1, hg):
                    p_sum = p_sum + p_c[:, h * nq:(h + 1) * nq]
                psum_scr[0:rows, :] = p_sum
                if rows < psum_scr.shape[0]:
                    psum_scr[rows:, :] = jnp.zeros((psum_scr.shape[0] - rows, nq), F32)
        o_c = oc_scr[...]
        per = SLC_BLOCK // CMP_STRIDE
        parts = [psum_scr[pl.ds(k, LANES, stride=per), :] for k in range(per)]
        tail_prev = jnp.where(blk == 0, 0.0, pltpu.roll(parts[per - 1], 1, axis=0))
        imp = parts[0] + parts[1] + parts[2] + 0.5 * parts[3] + 0.5 * tail_prev
        pool = jnp.where(valid & ~forced, imp, -FORCE)
        picked = forced & valid
        for _ in range(n_sel - N_FORCED):
            best = jnp.max(pool, axis=0, keepdims=True)
            first = jnp.min(jnp.where(pool == best, blk, LANES), axis=0, keepdims=True)
            hit = (blk == first) & (best > -0.5 * FORCE)
            picked = picked | hit
            pool = jnp.where(hit, -FORCE, pool)
        bias_blk = jnp.where(picked & (blk < first_blk), 0.0, MASK_BIAS).astype(BF16)
        q_aug = jnp.concatenate([qt_rope, jnp.concatenate([bias_blk] * hg, axis=1)], axis=0)

        def absorb(s, vt, carry):
            m, acc = carry
            m_new = jnp.maximum(m, jnp.max(s, axis=0, keepdims=True))
            e = jnp.exp2(s - m_new)
            return m_new, jnp.exp2(m - m_new) * acc + _dot(vt, e.astype(BF16))

        def key_tile(kt):
            return pl.multiple_of(jnp.minimum(kt, n_key_tiles - 1) * tk, tk)

        def scores_into(scr, kt):
            k0 = key_tile(kt)
            scr[...] = _dot(jnp.concatenate([ks_ref[0, pl.ds(k0, tk), :], et_ref[pl.ds(k0, tk), :]], axis=1), q_aug)

        def sweep_pair(i, carry):
            scores_into(sb_scr, 2 * i + 1)
            carry = absorb(sa_scr[...], vst_ref[g, :, pl.ds(key_tile(2 * i), tk)], carry)
            scores_into(sa_scr, 2 * i + 2)
            return absorb(sb_scr[...], vst_ref[g, :, pl.ds(key_tile(2 * i + 1), tk)], carry)

        init = (jnp.full((1, hg * nq), M_INIT, F32), jnp.zeros((VAL_ROWS, hg * nq), F32))
        scores_into(sa_scr, 0)
        carry = lax.fori_loop(0, n_sweep // 2, sweep_pair, init)
        last = n_sweep - 1
        carry = lax.cond(n_sweep % 2 == 1,
                         lambda c: absorb(sa_scr[...], vst_ref[g, :, pl.ds(key_tile(last), tk)], c),
                         lambda c: c, carry)
        own = pl.ds(pl.multiple_of(t0, nq), nq)
        s_own = _dot(ks_ref[0, own, :], qt_rope) + jnp.concatenate([bias_own] * hg, axis=1)
        _, acc_s = absorb(s_own, vst_ref[g, :, own], carry)
        o_s = acc_s[0:HEAD_DIM] * (1.0 / acc_s[HEAD_DIM:HEAD_DIM + 1])

        s_w = _dot(kw_ref[0, pl.ds(win_start, win_keys), :], qt_rope) + jnp.concatenate([bias_w] * hg, axis=1)
        e_w = jnp.exp2(s_w - jnp.max(s_w, axis=0, keepdims=True))
        acc_w = _dot(vwt_ref[g, :, pl.ds(win_start, win_keys)], e_w.astype(BF16))
        o_w = acc_w[0:HEAD_DIM] * (1.0 / acc_w[HEAD_DIM:HEAD_DIM + 1])

        mixes = []
        for h in range(hg):
            row = g * hg * 3 + h * 3
            cols = slice(h * nq, (h + 1) * nq)
            mixes.append(gates_t[row:row + 1] * o_c[feat, cols] + gates_t[row + 1:row + 2] * o_s[:, cols]
                         + gates_t[row + 2:row + 3] * o_w[:, cols])
        for hp in range(hg // 2):
            c = (g * hg) // 2 + hp
            pair = jnp.concatenate([mixes[2 * hp], mixes[2 * hp + 1]], axis=0)
            o_ref[0, :, c * LANES:(c + 1) * LANES] = pair.T.astype(o_ref.dtype)


def _nsa(pa3, r3, kc, vct, vst, vwt, expand, *, qp_blk, gate_blk, ks_blk, kw_blk):
    b, t, _ = pa3.shape
    n_cmp_pad = kc.shape[1]
    n_sel = min(SLC_TOPN, t // SLC_BLOCK)
    win_keys = min(WIN + Q_BLOCK, t)
    keys = lambda blk: pl.BlockSpec((1, t, LANES), lambda i, j, blk=blk: (i, 0, blk))
    vals_t = pl.BlockSpec((NSA_KV_GROUPS, VAL_ROWS, t), lambda i, j: (0, 0, i))
    return pl.pallas_call(
        functools.partial(_nsa_kernel, n_sel=n_sel, win_keys=win_keys),
        grid=(b, t // Q_BLOCK),
        in_specs=[
            pl.BlockSpec((1, Q_BLOCK, NSA_Q), lambda i, j: (i, j, qp_blk)),
            pl.BlockSpec((1, Q_BLOCK, NSA_Q), lambda i, j: (i, j, 0)),
            pl.BlockSpec((1, Q_BLOCK, LANES), lambda i, j: (i, j, gate_blk)),
            pl.BlockSpec((1, n_cmp_pad, LANES), lambda i, j: (i, 0, 0)),
            pl.BlockSpec((1, LANES, n_cmp_pad), lambda i, j: (i, 0, 0)),
            keys(ks_blk), vals_t, keys(kw_blk), vals_t,
            pl.BlockSpec((t, LANES), lambda i, j: (0, 0)),
        ],
        out_specs=pl.BlockSpec((1, Q_BLOCK, NSA_Q), lambda i, j: (i, j, 0)),
        out_shape=jax.ShapeDtypeStruct((b, t, NSA_Q), BF16),
        scratch_shapes=[
                        pltpu.VMEM((SEL_KEY_TILE, NSA_HPG * Q_BLOCK), F32), pltpu.VMEM((SEL_KEY_TILE, NSA_HPG * Q_BLOCK), F32),
                        pltpu.VMEM((LANES, NSA_HPG * Q_BLOCK), F32),
                        pltpu.VMEM((max(n_cmp_pad, LANES * SLC_BLOCK // CMP_STRIDE), Q_BLOCK), F32)],
        compiler_params=_params(("parallel", "arbitrary")),
        name="nsa",
    )(pa3, r3, pa3, kc, vct, r3, vst, r3, vwt, expand)


def _dil_kernel(q_ref, kp_ref, kc_ref, vp_ref, vc_ref, o_ref, lse_ref, *, span):
    ub = pl.program_id(2)
    nq = DIL_BLOCK
    lane = lax.broadcasted_iota(jnp.int32, (nq, LANES), 1)
    row = lax.broadcasted_iota(jnp.int32, (2 * nq, 2 * nq), 0) & (nq - 1)
    col = lax.broadcasted_iota(jnp.int32, (2 * nq, 2 * nq), 1)
    diff = row + nq - col
    band = (diff >= 0) & (diff <= span)
    bias = jnp.where(band, 0.0, MASK_BIAS)
    bias_first = jnp.where(band & ((col >= nq) | (ub > 0)), 0.0, MASK_BIAS)
    for j in range(q_ref.shape[0] // nq):
        rows = slice(j * nq, (j + 1) * nq)
        before = slice((j - 1) * nq, j * nq)
        for hp in range(DIL_HEADS // 2):
            cols = slice(hp * LANES, (hp + 1) * LANES)
            qt = (q_ref[rows, cols].astype(F32) * (SCALE * LOG2E)).astype(BF16)
            zero = jnp.zeros_like(qt)
            qs = jnp.concatenate([jnp.where(lane < HEAD_DIM, qt, zero), jnp.where(lane >= HEAD_DIM, qt, zero)], axis=0)
            k = jnp.concatenate([kp_ref[:, cols] if j == 0 else kc_ref[before, cols], kc_ref[rows, cols]], axis=0)
            v = jnp.concatenate([vp_ref[:, cols] if j == 0 else vc_ref[before, cols], vc_ref[rows, cols]], axis=0)
            s = _dot_nt(qs, k) + (bias_first if j == 0 else bias)
            m = jnp.max(s, axis=-1, keepdims=True)
            e = jnp.exp2(s - m)
            d = jnp.sum(e, axis=-1, keepdims=True)
            o = _dot(e.astype(BF16), v) * (1.0 / d)
            lse = (m + jnp.log2(d)) * (1.0 / LOG2E)
            o_ref[rows, cols] = jnp.where(lane < HEAD_DIM, o[:nq], o[nq:]).astype(o_ref.dtype)
            lse_ref[rows, cols] = jnp.where(lane < HEAD_DIM, lse[:nq], lse[nq:])


def _dilated(qd, kd, vd, b, dil, span):
    rows = qd.shape[0]
    ln = rows // b
    ub = min(ln, 4 * DIL_BLOCK)
    nub = ln // ub
    per = ub // DIL_BLOCK
    cur = pl.BlockSpec((ub, DIL_W), lambda i, r, u: (i * nub + u, r))
    prev = pl.BlockSpec((DIL_BLOCK, DIL_W), lambda i, r, u: (jnp.maximum((i * nub + u) * per - 1, 0), r))
    o, lse = pl.pallas_call(
        functools.partial(_dil_kernel, span=span),
        grid=(b, dil, nub),
        in_specs=[cur, prev, cur, prev, cur],
        out_specs=[cur, cur],
        out_shape=[jax.ShapeDtypeStruct(qd.shape, F32), jax.ShapeDtypeStruct(qd.shape, F32)],
        compiler_params=_params(("parallel", "parallel", "arbitrary")),
        name=f"dilated_{dil}",
    )(qd, kd, kd, vd, vd)
    return o, lse


def _out_kernel(*refs, dils):
    ng = len(dils)
    x_ref, yn_ref = refs[:2]
    o_refs, l_refs = refs[2:2 + ng], refs[2 + ng:2 + 2 * ng]
    g0_ref, g1_ref, wn_ref, wd_ref, wo_ref, o_ref = refs[2 + 2 * ng:8 + 2 * ng]
    scrs = list(refs[8 + 2 * ng:])

    def natural(ref, dil):
        if dil == 1:
            return ref[...]
        scr = scrs.pop(0)
        n = scr.shape[1] // dil
        for res in range(dil):
            for k in range(DIL_W // LANES):
                scr[k, pl.ds(res, n, stride=dil), :] = ref[:, res * DIL_W + k * LANES:res * DIL_W + (k + 1) * LANES]
        return jnp.concatenate([scr[k] for k in range(DIL_W // LANES)], axis=1)

    outs = [natural(r, d) for r, d in zip(o_refs, dils)]
    lses = [natural(r, d) for r, d in zip(l_refs, dils)]
    m = functools.reduce(jnp.maximum, lses)
    ws = [jnp.exp(l - m) for l in lses]
    inv_den = 1.0 / functools.reduce(lambda a, c: a + c, ws)
    y_dil = functools.reduce(lambda a, c: a + c, [w * o for w, o in zip(ws, outs)]) * inv_den
    merged = (jax.nn.sigmoid(g0_ref[...]) * _dot(yn_ref[...], wn_ref[...])
              + jax.nn.sigmoid(g1_ref[...]) * _dot(y_dil.astype(BF16), wd_ref[...]))
    o_ref[...] = x_ref[...] + _dot(merged.astype(BF16), wo_ref[...])


def _out_proj(x2, y_nsa, outs, lses, dils, pa, w_bn, w_bd, w_o, tm=512):
    m, d = x2.shape
    row = lambda w: pl.BlockSpec((tm, w), lambda i: (i, 0))
    dil_row = lambda dil: pl.BlockSpec((tm // dil, dil * DIL_W), lambda i: (i, 0))
    whole = lambda a: pl.BlockSpec(a.shape, lambda i: (0, 0))
    n_staged = 2 * sum(dil != 1 for dil in dils)
    return pl.pallas_call(
        functools.partial(_out_kernel, dils=tuple(dils)),
        grid=(m // tm,),
        in_specs=[row(d), row(NSA_Q)] + [dil_row(dil) for dil in dils] * 2
        + [pl.BlockSpec((tm, d), lambda i: (i, 0)), pl.BlockSpec((tm, d), lambda i: (i, 1))]
        + [whole(w_bn), whole(w_bd), whole(w_o)],
        out_specs=row(d),
        out_shape=jax.ShapeDtypeStruct((m, d), F32),
        scratch_shapes=[pltpu.VMEM((DIL_W // LANES, tm, LANES), F32)] * n_staged,
        compiler_params=_params(("parallel",)),
        name="out_proj",
    )(x2, y_nsa, *outs, *lses, pa, pa, w_bn, w_bd, w_o)


def _mixer(x2, pos_col, inv_lane, b, t, norm_mix, w_in, cmp_pos, cmp_w1, cmp_w2, w_branch_nsa, w_branch_dil, w_out):
    d = x2.shape[1]
    dh = HEAD_DIM
    o_q, o_kv, o_gn = 0, NSA_Q, NSA_Q + 6 * NSA_KV
    o_d = o_gn + 3 * NSA_HEADS
    o_gm = o_d + 3 * DIL_GROUPS * DIL_W
    kv = lambda i: w_in[:, o_kv + i * NSA_KV:o_kv + (i + 1) * NSA_KV]
    dd = lambda i: w_in[:, o_d + i * DIL_GROUPS * DIL_W:o_d + (i + 1) * DIL_GROUPS * DIL_W]
    w_q = w_in[:, o_q:o_q + NSA_Q]
    gn_pad = jnp.zeros((d, 2 * LANES - 3 * NSA_HEADS), w_in.dtype)
    w_a = jnp.concatenate([w_in[:, o_gm:o_gm + 2 * d], w_q, kv(0), kv(1), w_in[:, o_gn:o_gn + 3 * NSA_HEADS], gn_pad],
                          axis=1).astype(BF16)
    w_b = jnp.concatenate([w_q, kv(2), kv(4), dd(0), dd(1)], axis=1).astype(BF16)
    w_c = jnp.concatenate([kv(3), kv(5), dd(2)], axis=1).astype(BF16)
    dils = [dil for _, dil in DIL_PAIRS]
    n_r = NSA_Q + 2 * NSA_KV
    seg_b = ((0, n_r, 1),) + tuple((n_r + i * DIL_W, DIL_W, dil) for i, dil in enumerate(dils)) \
        + tuple((n_r + (DIL_GROUPS + i) * DIL_W, DIL_W, dil) for i, dil in enumerate(dils))
    seg_c = ((0, NSA_KV, 0), (NSA_KV, NSA_KV, 0)) + tuple((2 * NSA_KV + i * DIL_W, DIL_W, dil) for i, dil in enumerate(dils))

    pa, = _proj(x2, norm_mix, w_a, F32, name="proj_f32")
    rb, *qk_d = _proj(x2, norm_mix, w_b, BF16, segments=seg_b, pos=pos_col, inv_lane=inv_lane, name="proj_rope")
    vst, vwt, *v_d = _proj(x2, norm_mix, w_c, BF16, segments=seg_c, name="proj_bf16")

    pos = jnp.concatenate([cmp_pos] * NSA_KV_GROUPS, axis=-1)
    w1 = _both_groups(cmp_w1.reshape(2, CMP_BLOCK, dh, -1)).astype(BF16)
    kc, vct = _compress(pa, b, (2 * d + NSA_Q) // LANES, pos, w1, _both_groups(cmp_w2).astype(BF16))

    pa3 = pa.reshape(b, t, -1)
    r3 = rb.reshape(b, t, -1)
    expand = jnp.asarray(np.arange(t)[:, None] // SLC_BLOCK == np.arange(LANES)[None, :], BF16)
    y_nsa = _nsa(pa3, r3, kc, vct, vst, vwt, expand,
                 qp_blk=2 * d // NSA_Q, gate_blk=(2 * d + NSA_Q + 2 * NSA_KV) // LANES,
                 ks_blk=NSA_Q // LANES, kw_blk=NSA_Q // LANES + 1)

    outs, lses = [], []
    for gidx, (window, dil) in enumerate(DIL_PAIRS):
        o, lse = _dilated(qk_d[gidx], qk_d[DIL_GROUPS + gidx], v_d[gidx], b, dil, window // dil)
        outs.append(o)
        lses.append(lse)

    return _out_proj(x2, y_nsa.reshape(b * t, NSA_Q), outs, lses, dils, pa,
                     w_branch_nsa.astype(BF16), w_branch_dil.astype(BF16), w_out.astype(BF16))


def kernel(x, positions, norm_ffn1, ffn1_w_in, ffn1_w_out, norm_mix, w_in, cmp_pos, cmp_w1, cmp_w2,
           w_branch_nsa, w_branch_dil, w_out, norm_ffn2, ffn2_w_in, ffn2_w_out, norm_final):
    b, t, d = x.shape
    depth = norm_ffn1.shape[0]
    x2 = x.reshape(b * t, d)
    pos_col = positions.reshape(b * t, 1).astype(jnp.int32)
    inv = jnp.power(ROPE_THETA, -jnp.arange(0, ROPE_DIM, 2, dtype=F32) / ROPE_DIM)
    inv_head = jnp.concatenate([inv, inv, jnp.zeros((HEAD_DIM - ROPE_DIM,), F32)])
    inv_lane = jnp.tile(inv_head, LANES // HEAD_DIM).reshape(1, LANES)
    for l in range(depth):
        last = l == depth - 1
        x2 = _ffn(x2, norm_ffn1[l], ffn1_w_in[l].astype(BF16), ffn1_w_out[l].astype(BF16))
        x2 = _mixer(x2, pos_col, inv_lane, b, t, norm_mix[l], w_in[l], cmp_pos[l], cmp_w1[l], cmp_w2[l],
                    w_branch_nsa[l], w_branch_dil[l], w_out[l])
        x2 = _ffn(x2, norm_ffn2[l], ffn2_w_in[l].astype(BF16), ffn2_w_out[l].astype(BF16),
                  final_gain=norm_final if last else None)
    if depth == 0:
        raise ValueError("depth must be positive")
    return x2.reshape(b, t, d)
```

```python
import functools

import numpy as np
import jax
import jax.numpy as jnp
from jax import lax
from jax.experimental import pallas as pl
from jax.experimental.pallas import tpu as pltpu

HEAD_DIM = 64
ROPE_DIM = HEAD_DIM // 4
ROPE_THETA = 500000.0
NORM_EPS = 1e-6

NSA_HEADS = 8
NSA_KV_GROUPS = 2
NSA_HPG = NSA_HEADS // NSA_KV_GROUPS
CMP_BLOCK = 32
CMP_STRIDE = 16
SLC_BLOCK = 64
SLC_TOPN = 16
WIN = 512
Q_BLOCK = 128
FORCE = 1e9
N_FORCED = 3
NEG = -1e30

DIL_PAIRS = ((128, 1), (512, 4), (2048, 16))
DIL_GROUPS = len(DIL_PAIRS)
DIL_HEADS = 4
DIL_BLOCK = 128

NSA_Q = NSA_HEADS * HEAD_DIM
NSA_KV = NSA_KV_GROUPS * HEAD_DIM
DIL_W = DIL_HEADS * HEAD_DIM

LANES = 128
SEL_KEY_TILE = 1024
VAL_ROWS = HEAD_DIM + 16
VMEM_LIMIT = 56 * 1024 * 1024

F32 = jnp.float32
BF16 = jnp.bfloat16
SCALE = HEAD_DIM ** -0.5
LOG2E = 1.4426950408889634


def _dot(a, b):
    return jnp.dot(a, b, preferred_element_type=F32)


def _dot_nt(a, b):
    return lax.dot_general(a, b, (((1,), (1,)), ((), ())), preferred_element_type=F32)


def _rmsnorm(x, g):
    return x * lax.rsqrt(jnp.mean(x * x, axis=-1, keepdims=True) + NORM_EPS) * g


def _params(sem):
    return pltpu.CompilerParams(dimension_semantics=sem, vmem_limit_bytes=VMEM_LIMIT)


def _ffn_kernel(*refs, final):
    x_ref, g_ref, wi_ref, wo_ref = refs[:4]
    o_ref = refs[-1]
    dff = wo_ref.shape[0]
    x = x_ref[...]
    h = _rmsnorm(x, g_ref[...]).astype(BF16)
    u = _dot(h, wi_ref[:, :dff])
    v = _dot(h, wi_ref[:, dff:])
    a = (u * jax.nn.sigmoid(u)) * v
    y = x + 0.5 * _dot(a.astype(BF16), wo_ref[...])
    if final:
        y = _rmsnorm(y, refs[4][...])
    o_ref[...] = y


def _ffn(x2, gain, w_in, w_out, final_gain=None, tm=512):
    m, d = x2.shape
    final = final_gain is not None
    whole = lambda a: pl.BlockSpec(a.shape, lambda i: (0, 0))
    in_specs = [pl.BlockSpec((tm, d), lambda i: (i, 0)), pl.BlockSpec((1, d), lambda i: (0, 0)), whole(w_in), whole(w_out)]
    args = [x2, gain.reshape(1, d), w_in, w_out]
    if final:
        in_specs.append(pl.BlockSpec((1, d), lambda i: (0, 0)))
        args.append(final_gain.reshape(1, d))
    return pl.pallas_call(
        functools.partial(_ffn_kernel, final=final),
        grid=(m // tm,),
        in_specs=in_specs,
        out_specs=pl.BlockSpec((tm, d), lambda i: (i, 0)),
        out_shape=jax.ShapeDtypeStruct((m, d), F32),
        compiler_params=_params(("parallel",)),
        name="ffn_final" if final else "ffn",
    )(*args)


def _rope_coeffs(pos_col, inv_lane):
    ang = pos_col.astype(F32) * inv_lane
    c, s = jnp.cos(ang), jnp.sin(ang)
    d = lax.broadcasted_iota(jnp.int32, ang.shape, 1) & (HEAD_DIM - 1)
    half = ROPE_DIM // 2
    coef_c = jnp.where(d < ROPE_DIM, c, 1.0)
    coef_hi = jnp.where(d < half, -s, 0.0)
    coef_lo = jnp.where((d >= half) & (d < ROPE_DIM), s, 0.0)
    return coef_c, coef_hi, coef_lo


def _proj_kernel(*refs, rope, segments):
    n_in = 5 if rope else 3
    x_ref, g_ref, w_ref = refs[:3]
    o_refs = refs[n_in:n_in + len(segments)]
    h = _rmsnorm(x_ref[...], g_ref[...]).astype(BF16)
    y = _dot(h, w_ref[...])
    if len(refs) == n_in + len(segments):
        o_refs[0][...] = y.astype(o_refs[0].dtype)
        return
    y_scr = refs[-1]
    if rope:
        coef_c, coef_hi, coef_lo = _rope_coeffs(refs[3][...], refs[4][...])
        half = ROPE_DIM // 2
        for t in range(y.shape[1] // LANES):
            yt = y[:, t * LANES:(t + 1) * LANES]
            hi = pltpu.roll(yt, LANES - half, axis=1)
            lo = pltpu.roll(yt, half, axis=1)
            y_scr[t] = yt * coef_c + hi * coef_hi + lo * coef_lo
    else:
        for t in range(y.shape[1] // LANES):
            y_scr[t] = y[:, t * LANES:(t + 1) * LANES]
    tm = y.shape[0]
    for o_ref, (c0, w, dil) in zip(o_refs, segments):
        if dil == 0:
            yt = y_scr[c0 // LANES].T
            pad = jnp.where(lax.broadcasted_iota(jnp.int32, (VAL_ROWS - HEAD_DIM, tm), 0) == 0, 1.0, 0.0)
            for g in range(NSA_KV_GROUPS):
                rows = jnp.concatenate([yt[g * HEAD_DIM:(g + 1) * HEAD_DIM], pad], axis=0)
                o_ref[g] = rows.astype(o_ref.dtype)
            continue
        for k in range(w // LANES):
            src = c0 // LANES + k
            for res in range(dil):
                rows = y_scr[src] if dil == 1 else y_scr[src, pl.ds(res, tm // dil, stride=dil), :]
                o_ref[:, res * w + k * LANES:res * w + (k + 1) * LANES] = rows.astype(o_ref.dtype)


def _proj(x2, gain, w, out_dtype, segments=None, pos=None, inv_lane=None, tm=512, name="proj"):
    m, d = x2.shape
    n = w.shape[1]
    rope = pos is not None
    segments = segments or ((0, n, 1),)
    staged = rope or len(segments) > 1 or segments[0][2] != 1
    in_specs = [
        pl.BlockSpec((tm, d), lambda i: (i, 0)),
        pl.BlockSpec((1, d), lambda i: (0, 0)),
        pl.BlockSpec((d, n), lambda i: (0, 0)),
    ]
    args = [x2, gain.reshape(1, d), w]
    if rope:
        in_specs += [pl.BlockSpec((tm, 1), lambda i: (i, 0)), pl.BlockSpec((1, LANES), lambda i: (0, 0))]
        args += [pos, inv_lane]
    outs = pl.pallas_call(
        functools.partial(_proj_kernel, rope=rope, segments=segments),
        grid=(m // tm,),
        in_specs=in_specs,
        out_specs=[pl.BlockSpec((tm // dil, dil * wd), lambda i: (i, 0)) if dil else
                   pl.BlockSpec((NSA_KV_GROUPS, VAL_ROWS, tm), lambda i: (0, 0, i)) for _, wd, dil in segments],
        out_shape=[jax.ShapeDtypeStruct((m // dil, dil * wd) if dil else (NSA_KV_GROUPS, VAL_ROWS, m), out_dtype)
                   for _, wd, dil in segments],
        scratch_shapes=[pltpu.VMEM((n // LANES, tm, LANES), F32)] if staged else [],
        compiler_params=_params(("parallel",)),
        name=name,
    )(*args)
    return outs


def _compress_kernel(ksrc_ref, vsrc_ref, pos_ref, w1_ref, w2_ref, kc_ref, vct_ref):
    n_seg = kc_ref.shape[1]
    outs = []
    for c, src in enumerate((ksrc_ref, vsrc_ref)):
        first = second = None
        for p in range(CMP_STRIDE):
            tok = src[pl.ds(p, n_seg, stride=CMP_STRIDE), :]
            a = _dot((tok + pos_ref[c, p:p + 1, :]).astype(BF16), w1_ref[c, p])
            b = _dot((tok + pos_ref[c, CMP_STRIDE + p:CMP_STRIDE + p + 1, :]).astype(BF16), w1_ref[c, CMP_STRIDE + p])
            first = a if first is None else first + a
            second = b if second is None else second + b
        pre = first + pltpu.roll(second, n_seg - 1, axis=0)
        hid = pre * jax.nn.sigmoid(pre)
        outs.append(_dot(hid.astype(BF16), w2_ref[c]))
    kc_ref[0] = outs[0].astype(kc_ref.dtype)
    vct_ref[0] = outs[1].T.astype(vct_ref.dtype)


def _compress(pa, b, ksrc_blk, pos, w1, w2):
    t = pa.shape[0] // b
    n_seg = t // CMP_STRIDE
    whole = lambda a: pl.BlockSpec(a.shape, lambda i: (0,) * a.ndim)
    return pl.pallas_call(
        _compress_kernel,
        grid=(b,),
        in_specs=[pl.BlockSpec((t, LANES), lambda i: (i, ksrc_blk)), pl.BlockSpec((t, LANES), lambda i: (i, ksrc_blk + 1)),
                  whole(pos), whole(w1), whole(w2)],
        out_specs=[pl.BlockSpec((1, n_seg, LANES), lambda i: (i, 0, 0)), pl.BlockSpec((1, LANES, n_seg), lambda i: (i, 0, 0))],
        out_shape=[jax.ShapeDtypeStruct((b, n_seg, LANES), BF16), jax.ShapeDtypeStruct((b, LANES, n_seg), BF16)],
        compiler_params=_params(("parallel",)),
        name="compress",
    )(pa, pa, pos, w1, w2)


def _both_groups(w):
    z = jnp.zeros_like(w)
    return jnp.concatenate([jnp.concatenate([w, z], axis=-1), jnp.concatenate([z, w], axis=-1)], axis=-2)


MASK_BIAS = NEG
M_INIT = -1e29


def _stack_heads_t(q, g):
    zeros = jnp.zeros((HEAD_DIM, q.shape[0]), F32)
    parts = []
    for h in range(NSA_HPG):
        hg = g * NSA_HPG + h
        tile_t = (q[:, (hg // 2) * LANES:(hg // 2 + 1) * LANES] * (SCALE * LOG2E)).T
        feat = tile_t[(hg % 2) * HEAD_DIM:(hg % 2 + 1) * HEAD_DIM]
        parts.append(jnp.concatenate([feat, zeros] if g == 0 else [zeros, feat], axis=0))
    return jnp.concatenate(parts, axis=1).astype(BF16)


def _softmax_t(s, bias, hg):
    sb = s + jnp.concatenate([bias] * hg, axis=1)
    m = jnp.maximum(jnp.max(sb, axis=0, keepdims=True), M_INIT)
    e = jnp.exp2(sb - m)
    d = jnp.sum(e, axis=0, keepdims=True)
    return e, 1.0 / jnp.where(d > 0, d, 1.0)


def _nsa_kernel(qp_ref, qr_ref, gate_ref, kc_ref, vct_ref, ks_ref, vst_ref, kw_ref, vwt_ref, et_ref,
                o_ref, sa_scr, sb_scr, oc_scr, psum_scr, *, n_sel, win_keys):
    qb = pl.program_id(1)
    t0 = qb * Q_BLOCK
    nq = Q_BLOCK
    hg = NSA_HPG
    tk = SEL_KEY_TILE
    n_cmp_pad = kc_ref.shape[1]
    t_row = t0 + lax.broadcasted_iota(jnp.int32, (1, nq), 1)
    blk = lax.broadcasted_iota(jnp.int32, (LANES, nq), 0)

    q_plain = qp_ref[0]
    q_rope = qr_ref[0].astype(F32)
    gates_t = jax.nn.sigmoid(gate_ref[0]).T

    cmp_chunk = min(LANES, n_cmp_pad)
    n_cmp_rows = (qb + 1) * (Q_BLOCK // CMP_STRIDE)
    n_cmp_chunks = jnp.minimum((n_cmp_rows + cmp_chunk - 1) // cmp_chunk, n_cmp_pad // cmp_chunk)
    cur = t_row >> 6
    forced = (blk == 0) | (blk == cur) | (blk == cur - 1)
    valid = blk <= cur

    win_start = pl.multiple_of(jnp.maximum(t0 - WIN, 0), Q_BLOCK)
    diff_w = t_row - (win_start + lax.broadcasted_iota(jnp.int32, (win_keys, nq), 0))
    bias_w = jnp.where((diff_w >= 0) & (diff_w < WIN), 0.0, MASK_BIAS)

    first_blk = qb * (Q_BLOCK // SLC_BLOCK)
    n_sweep = (t0 + tk - 1) // tk
    n_key_tiles = ks_ref.shape[1] // tk
    bias_own = jnp.where(lax.broadcasted_iota(jnp.int32, (nq, nq), 0) <= lax.broadcasted_iota(jnp.int32, (nq, nq), 1),
                         0.0, MASK_BIAS)

    pools = []
    for g in range(NSA_KV_GROUPS):
        qt_plain = _stack_heads_t(q_plain, g)

        for c in range(1, n_cmp_pad // cmp_chunk + 1):
            @pl.when(n_cmp_chunks == c)
            def _(c=c):
                rows = c * cmp_chunk
                cmp_end = lax.broadcasted_iota(jnp.int32, (rows, nq), 0) * CMP_STRIDE + (CMP_BLOCK - 1)
                bias_c = jnp.where(cmp_end <= t_row, 0.0, MASK_BIAS)
                e_c, r_c = _softmax_t(_dot(kc_ref[0, 0:rows, :], qt_plain), bias_c, hg)
                p_c = e_c * r_c
                oc_scr[g] = _dot(vct_ref[0, :, 0:rows], p_c.astype(BF16))
                p_sum = p_c[:, 0:nq]
                for h in range(1, hg):
                    p_sum = p_sum + p_c[:, h * nq:(h + 1) * nq]
                psum_scr[0:rows, :] = p_sum
                if rows < psum_scr.shape[0]:
                    psum_scr[rows:, :] = jnp.zeros((psum_scr.shape[0] - rows, nq), F32)
        per = SLC_BLOCK // CMP_STRIDE
        parts = [psum_scr[pl.ds(k, LANES, stride=per), :] for k in range(per)]
        tail_prev = jnp.where(blk == 0, 0.0, pltpu.roll(parts[per - 1], 1, axis=0))
        imp = parts[0] + parts[1] + parts[2] + 0.5 * parts[3] + 0.5 * tail_prev
        pools.append(jnp.where(valid & ~forced, imp, -FORCE))

    picks = [forced & valid] * NSA_KV_GROUPS
    for _ in range(n_sel - N_FORCED):
        for g in range(NSA_KV_GROUPS):
            best = jnp.max(pools[g], axis=0, keepdims=True)
            first = jnp.min(jnp.where(pools[g] == best, blk, LANES), axis=0, keepdims=True)
            hit = (blk == first) & (best > -0.5 * FORCE)
            picks[g] = picks[g] | hit
            pools[g] = jnp.where(hit, -FORCE, pools[g])

    for g in range(NSA_KV_GROUPS):
        qt_rope = _stack_heads_t(q_rope, g)
        feat = slice(g * HEAD_DIM, (g + 1) * HEAD_DIM)
        picked = picks[g]
        o_c = oc_scr[g]
        bias_blk = jnp.where(picked & (blk < first_blk), 0.0, MASK_BIAS).astype(BF16)
        q_aug = jnp.concatenate([qt_rope, jnp.concatenate([bias_blk] * hg, axis=1)], axis=0)

        def absorb(s, vt, carry):
            m, acc = carry
            m_new = jnp.maximum(m, jnp.max(s, axis=0, keepdims=True))
            e = jnp.exp2(s - m_new)
            return m_new, jnp.exp2(m - m_new) * acc + _dot(vt, e.astype(BF16))

        def key_tile(kt):
            return pl.multiple_of(jnp.minimum(kt, n_key_tiles - 1) * tk, tk)

        def scores_into(scr, kt):
            k0 = key_tile(kt)
            scr[...] = _dot(jnp.concatenate([ks_ref[0, pl.ds(k0, tk), :], et_ref[pl.ds(k0, tk), :]], axis=1), q_aug)

        def sweep_pair(i, carry):
            scores_into(sb_scr, 2 * i + 1)
            carry = absorb(sa_scr[...], vst_ref[g, :, pl.ds(key_tile(2 * i), tk)], carry)
            scores_into(sa_scr, 2 * i + 2)
            return absorb(sb_scr[...], vst_ref[g, :, pl.ds(key_tile(2 * i + 1), tk)], carry)

        init = (jnp.full((1, hg * nq), M_INIT, F32), jnp.zeros((VAL_ROWS, hg * nq), F32))
        scores_into(sa_scr, 0)
        carry = lax.fori_loop(0, n_sweep // 2, sweep_pair, init)
        last = n_sweep - 1
        carry = lax.cond(n_sweep % 2 == 1,
                         lambda c: absorb(sa_scr[...], vst_ref[g, :, pl.ds(key_tile(last), tk)], c),
                         lambda c: c, carry)
        own = pl.ds(pl.multiple_of(t0, nq), nq)
        s_own = _dot(ks_ref[0, own, :], qt_rope) + jnp.concatenate([bias_own] * hg, axis=1)
        _, acc_s = absorb(s_own, vst_ref[g, :, own], carry)
        o_s = acc_s[0:HEAD_DIM] * (1.0 / acc_s[HEAD_DIM:HEAD_DIM + 1])

        s_w = _dot(kw_ref[0, pl.ds(win_start, win_keys), :], qt_rope) + jnp.concatenate([bias_w] * hg, axis=1)
        e_w = jnp.exp2(s_w - jnp.max(s_w, axis=0, keepdims=True))
        acc_w = _dot(vwt_ref[g, :, pl.ds(win_start, win_keys)], e_w.astype(BF16))
        o_w = acc_w[0:HEAD_DIM] * (1.0 / acc_w[HEAD_DIM:HEAD_DIM + 1])

        mixes = []
        for h in range(hg):
            row = g * hg * 3 + h * 3
            cols = slice(h * nq, (h + 1) * nq)
            mixes.append(gates_t[row:row + 1] * o_c[feat, cols] + gates_t[row + 1:row + 2] * o_s[:, cols]
                         + gates_t[row + 2:row + 3] * o_w[:, cols])
        for hp in range(hg // 2):
            c = (g * hg) // 2 + hp
            pair = jnp.concatenate([mixes[2 * hp], mixes[2 * hp + 1]], axis=0)
            o_ref[0, :, c * LANES:(c + 1) * LANES] = pair.T.astype(o_ref.dtype)


def _nsa(pa3, r3, kc, vct, vst, vwt, expand, *, qp_blk, gate_blk, ks_blk, kw_blk):
    b, t, _ = pa3.shape
    n_cmp_pad = kc.shape[1]
    n_sel = min(SLC_TOPN, t // SLC_BLOCK)
    win_keys = min(WIN + Q_BLOCK, t)
    keys = lambda blk: pl.BlockSpec((1, t, LANES), lambda i, j, blk=blk: (i, 0, blk))
    vals_t = pl.BlockSpec((NSA_KV_GROUPS, VAL_ROWS, t), lambda i, j: (0, 0, i))
    return pl.pallas_call(
        functools.partial(_nsa_kernel, n_sel=n_sel, win_keys=win_keys),
        grid=(b, t // Q_BLOCK),
        in_specs=[
            pl.BlockSpec((1, Q_BLOCK, NSA_Q), lambda i, j: (i, j, qp_blk)),
            pl.BlockSpec((1, Q_BLOCK, NSA_Q), lambda i, j: (i, j, 0)),
            pl.BlockSpec((1, Q_BLOCK, LANES), lambda i, j: (i, j, gate_blk)),
            pl.BlockSpec((1, n_cmp_pad, LANES), lambda i, j: (i, 0, 0)),
            pl.BlockSpec((1, LANES, n_cmp_pad), lambda i, j: (i, 0, 0)),
            keys(ks_blk), vals_t, keys(kw_blk), vals_t,
            pl.BlockSpec((t, LANES), lambda i, j: (0, 0)),
        ],
        out_specs=pl.BlockSpec((1, Q_BLOCK, NSA_Q), lambda i, j: (i, j, 0)),
        out_shape=jax.ShapeDtypeStruct((b, t, NSA_Q), BF16),
        scratch_shapes=[
                        pltpu.VMEM((SEL_KEY_TILE, NSA_HPG * Q_BLOCK), F32), pltpu.VMEM((SEL_KEY_TILE, NSA_HPG * Q_BLOCK), F32),
                        pltpu.VMEM((NSA_KV_GROUPS, LANES, NSA_HPG * Q_BLOCK), F32),
                        pltpu.VMEM((max(n_cmp_pad, LANES * SLC_BLOCK // CMP_STRIDE), Q_BLOCK), F32)],
        compiler_params=_params(("parallel", "arbitrary")),
        name="nsa",
    )(pa3, r3, pa3, kc, vct, r3, vst, r3, vwt, expand)


def _dil_kernel(q_ref, kp_ref, kc_ref, vp_ref, vc_ref, o_ref, lse_ref, *, span):
    ub = pl.program_id(2)
    nq = DIL_BLOCK
    lane = lax.broadcasted_iota(jnp.int32, (nq, LANES), 1)
    row = lax.broadcasted_iota(jnp.int32, (2 * nq, 2 * nq), 0) & (nq - 1)
    col = lax.broadcasted_iota(jnp.int32, (2 * nq, 2 * nq), 1)
    diff = row + nq - col
    band = (diff >= 0) & (diff <= span)
    bias = jnp.where(band, 0.0, MASK_BIAS)
    bias_first = jnp.where(band & ((col >= nq) | (ub > 0)), 0.0, MASK_BIAS)
    for j in range(q_ref.shape[0] // nq):
        rows = slice(j * nq, (j + 1) * nq)
        before = slice((j - 1) * nq, j * nq)
        for hp in range(DIL_HEADS // 2):
            cols = slice(hp * LANES, (hp + 1) * LANES)
            qt = (q_ref[rows, cols].astype(F32) * (SCALE * LOG2E)).astype(BF16)
            zero = jnp.zeros_like(qt)
            qs = jnp.concatenate([jnp.where(lane < HEAD_DIM, qt, zero), jnp.where(lane >= HEAD_DIM, qt, zero)], axis=0)
            k = jnp.concatenate([kp_ref[:, cols] if j == 0 else kc_ref[before, cols], kc_ref[rows, cols]], axis=0)
            v = jnp.concatenate([vp_ref[:, cols] if j == 0 else vc_ref[before, cols], vc_ref[rows, cols]], axis=0)
            s = _dot_nt(qs, k) + (bias_first if j == 0 else bias)
            m = jnp.max(s, axis=-1, keepdims=True)
            e = jnp.exp2(s - m)
            d = jnp.sum(e, axis=-1, keepdims=True)
            o = _dot(e.astype(BF16), v) * (1.0 / d)
            lse = (m + jnp.log2(d)) * (1.0 / LOG2E)
            o_ref[rows, cols] = jnp.where(lane < HEAD_DIM, o[:nq], o[nq:]).astype(o_ref.dtype)
            lse_ref[rows, cols] = jnp.where(lane < HEAD_DIM, lse[:nq], lse[nq:])


def _dilated(qd, kd, vd, b, dil, span):
    rows = qd.shape[0]
    ln = rows // b
    ub = min(ln, 4 * DIL_BLOCK)
    nub = ln // ub
    per = ub // DIL_BLOCK
    cur = pl.BlockSpec((ub, DIL_W), lambda i, r, u: (i * nub + u, r))
    prev = pl.BlockSpec((DIL_BLOCK, DIL_W), lambda i, r, u: (jnp.maximum((i * nub + u) * per - 1, 0), r))
    o, lse = pl.pallas_call(
        functools.partial(_dil_kernel, span=span),
        grid=(b, dil, nub),
        in_specs=[cur, prev, cur, prev, cur],
        out_specs=[cur, cur],
        out_shape=[jax.ShapeDtypeStruct(qd.shape, F32), jax.ShapeDtypeStruct(qd.shape, F32)],
        compiler_params=_params(("parallel", "parallel", "arbitrary")),
        name=f"dilated_{dil}",
    )(qd, kd, kd, vd, vd)
    return o, lse


def _out_kernel(*refs, dils):
    ng = len(dils)
    x_ref, yn_ref = refs[:2]
    o_refs, l_refs = refs[2:2 + ng], refs[2 + ng:2 + 2 * ng]
    g0_ref, g1_ref, wn_ref, wd_ref, wo_ref, o_ref = refs[2 + 2 * ng:8 + 2 * ng]
    scrs = list(refs[8 + 2 * ng:])

    def natural(ref, dil):
        if dil == 1:
            return ref[...]
        scr = scrs.pop(0)
        n = scr.shape[1] // dil
        for res in range(dil):
            for k in range(DIL_W // LANES):
                scr[k, pl.ds(res, n, stride=dil), :] = ref[:, res * DIL_W + k * LANES:res * DIL_W + (k + 1) * LANES]
        return jnp.concatenate([scr[k] for k in range(DIL_W // LANES)], axis=1)

    outs = [natural(r, d) for r, d in zip(o_refs, dils)]
    lses = [natural(r, d) for r, d in zip(l_refs, dils)]
    m = functools.reduce(jnp.maximum, lses)
    ws = [jnp.exp(l - m) for l in lses]
    inv_den = 1.0 / functools.reduce(lambda a, c: a + c, ws)
    y_dil = functools.reduce(lambda a, c: a + c, [w * o for w, o in zip(ws, outs)]) * inv_den
    merged = (jax.nn.sigmoid(g0_ref[...]) * _dot(yn_ref[...], wn_ref[...])
              + jax.nn.sigmoid(g1_ref[...]) * _dot(y_dil.astype(BF16), wd_ref[...]))
    o_ref[...] = x_ref[...] + _dot(merged.astype(BF16), wo_ref[...])


def _out_proj(x2, y_nsa, outs, lses, dils, pa, w_bn, w_bd, w_o, tm=512):
    m, d = x2.shape
    row = lambda w: pl.BlockSpec((tm, w), lambda i: (i, 0))
    dil_row = lambda dil: pl.BlockSpec((tm // dil, dil * DIL_W), lambda i: (i, 0))
    whole = lambda a: pl.BlockSpec(a.shape, lambda i: (0, 0))
    n_staged = 2 * sum(dil != 1 for dil in dils)
    return pl.pallas_call(
        functools.partial(_out_kernel, dils=tuple(dils)),
        grid=(m // tm,),
        in_specs=[row(d), row(NSA_Q)] + [dil_row(dil) for dil in dils] * 2
        + [pl.BlockSpec((tm, d), lambda i: (i, 0)), pl.BlockSpec((tm, d), lambda i: (i, 1))]
        + [whole(w_bn), whole(w_bd), whole(w_o)],
        out_specs=row(d),
        out_shape=jax.ShapeDtypeStruct((m, d), F32),
        scratch_shapes=[pltpu.VMEM((DIL_W // LANES, tm, LANES), F32)] * n_staged,
        compiler_params=_params(("parallel",)),
        name="out_proj",
    )(x2, y_nsa, *outs, *lses, pa, pa, w_bn, w_bd, w_o)


def _mixer(x2, pos_col, inv_lane, b, t, norm_mix, w_in, cmp_pos, cmp_w1, cmp_w2, w_branch_nsa, w_branch_dil, w_out):
    d = x2.shape[1]
    dh = HEAD_DIM
    o_q, o_kv, o_gn = 0, NSA_Q, NSA_Q + 6 * NSA_KV
    o_d = o_gn + 3 * NSA_HEADS
    o_gm = o_d + 3 * DIL_GROUPS * DIL_W
    kv = lambda i: w_in[:, o_kv + i * NSA_KV:o_kv + (i + 1) * NSA_KV]
    dd = lambda i: w_in[:, o_d + i * DIL_GROUPS * DIL_W:o_d + (i + 1) * DIL_GROUPS * DIL_W]
    w_q = w_in[:, o_q:o_q + NSA_Q]
    gn_pad = jnp.zeros((d, 2 * LANES - 3 * NSA_HEADS), w_in.dtype)
    w_a = jnp.concatenate([w_in[:, o_gm:o_gm + 2 * d], w_q, kv(0), kv(1), w_in[:, o_gn:o_gn + 3 * NSA_HEADS], gn_pad],
                          axis=1).astype(BF16)
    w_b = jnp.concatenate([w_q, kv(2), kv(4), dd(0), dd(1)], axis=1).astype(BF16)
    w_c = jnp.concatenate([kv(3), kv(5), dd(2)], axis=1).astype(BF16)
    dils = [dil for _, dil in DIL_PAIRS]
    n_r = NSA_Q + 2 * NSA_KV
    seg_b = ((0, n_r, 1),) + tuple((n_r + i * DIL_W, DIL_W, dil) for i, dil in enumerate(dils)) \
        + tuple((n_r + (DIL_GROUPS + i) * DIL_W, DIL_W, dil) for i, dil in enumerate(dils))
    seg_c = ((0, NSA_KV, 0), (NSA_KV, NSA_KV, 0)) + tuple((2 * NSA_KV + i * DIL_W, DIL_W, dil) for i, dil in enumerate(dils))

    pa, = _proj(x2, norm_mix, w_a, F32, name="proj_f32")
    rb, *qk_d = _proj(x2, norm_mix, w_b, BF16, segments=seg_b, pos=pos_col, inv_lane=inv_lane, name="proj_rope")
    vst, vwt, *v_d = _proj(x2, norm_mix, w_c, BF16, segments=seg_c, name="proj_bf16")

    pos = jnp.concatenate([cmp_pos] * NSA_KV_GROUPS, axis=-1)
    w1 = _both_groups(cmp_w1.reshape(2, CMP_BLOCK, dh, -1)).astype(BF16)
    kc, vct = _compress(pa, b, (2 * d + NSA_Q) // LANES, pos, w1, _both_groups(cmp_w2).astype(BF16))

    pa3 = pa.reshape(b, t, -1)
    r3 = rb.reshape(b, t, -1)
    expand = jnp.asarray(np.arange(t)[:, None] // SLC_BLOCK == np.arange(LANES)[None, :], BF16)
    y_nsa = _nsa(pa3, r3, kc, vct, vst, vwt, expand,
                 qp_blk=2 * d // NSA_Q, gate_blk=(2 * d + NSA_Q + 2 * NSA_KV) // LANES,
                 ks_blk=NSA_Q // LANES, kw_blk=NSA_Q // LANES + 1)

    outs, lses = [], []
    for gidx, (window, dil) in enumerate(DIL_PAIRS):
        o, lse = _dilated(qk_d[gidx], qk_d[DIL_GROUPS + gidx], v_d[gidx], b, dil, window // dil)
        outs.append(o)
        lses.append(lse)

    return _out_proj(x2, y_nsa.reshape(b * t, NSA_Q), outs, lses, dils, pa,
                     w_branch_nsa.astype(BF16), w_branch_dil.astype(BF16), w_out.astype(BF16))


def kernel(x, positions, norm_ffn1, ffn1_w_in, ffn1_w_out, norm_mix, w_in, cmp_pos, cmp_w1, cmp_w2,
           w_branch_nsa, w_branch_dil, w_out, norm_ffn2, ffn2_w_in, ffn2_w_out, norm_final):
    b, t, d = x.shape
    depth = norm_ffn1.shape[0]
    x2 = x.reshape(b * t, d)
    pos_col = positions.reshape(b * t, 1).astype(jnp.int32)
    inv = jnp.power(ROPE_THETA, -jnp.arange(0, ROPE_DIM, 2, dtype=F32) / ROPE_DIM)
    inv_head = jnp.concatenate([inv, inv, jnp.zeros((HEAD_DIM - ROPE_DIM,), F32)])
    inv_lane = jnp.tile(inv_head, LANES // HEAD_DIM).reshape(1, LANES)
    for l in range(depth):
        last = l == depth - 1
        x2 = _ffn(x2, norm_ffn1[l], ffn1_w_in[l].astype(BF16), ffn1_w_out[l].astype(BF16))
        x2 = _mixer(x2, pos_col, inv_lane, b, t, norm_mix[l], w_in[l], cmp_pos[l], cmp_w1[l], cmp_w2[l],
                    w_branch_nsa[l], w_branch_dil[l], w_out[l])
        x2 = _ffn(x2, norm_ffn2[l], ffn2_w_in[l].astype(BF16), ffn2_w_out[l].astype(BF16),
                  final_gain=norm_final if last else None)
    if depth == 0:
        raise ValueError("depth must be positive")
    return x2.reshape(b, t, d)
```

```python
import functools

import numpy as np
import jax
import jax.numpy as jnp
from jax import lax
from jax.experimental import pallas as pl
from jax.experimental.pallas import tpu as pltpu

HEAD_DIM = 64
ROPE_DIM = HEAD_DIM // 4
ROPE_THETA = 500000.0
NORM_EPS = 1e-6

NSA_HEADS = 8
NSA_KV_GROUPS = 2
NSA_HPG = NSA_HEADS // NSA_KV_GROUPS
CMP_BLOCK = 32
CMP_STRIDE = 16
SLC_BLOCK = 64
SLC_TOPN = 16
WIN = 512
Q_BLOCK = 128
FORCE = 1e9
N_FORCED = 3
NEG = -1e30

DIL_PAIRS = ((128, 1), (512, 4), (2048, 16))
DIL_GROUPS = len(DIL_PAIRS)
DIL_HEADS = 4
DIL_BLOCK = 128

NSA_Q = NSA_HEADS * HEAD_DIM
NSA_KV = NSA_KV_GROUPS * HEAD_DIM
DIL_W = DIL_HEADS * HEAD_DIM

LANES = 128
SEL_KEY_TILE = 1024
VAL_ROWS = HEAD_DIM + 16
VMEM_LIMIT = 56 * 1024 * 1024

F32 = jnp.float32
BF16 = jnp.bfloat16
SCALE = HEAD_DIM ** -0.5
LOG2E = 1.4426950408889634


def _dot(a, b):
    return jnp.dot(a, b, preferred_element_type=F32)


def _dot_nt(a, b):
    return lax.dot_general(a, b, (((1,), (1,)), ((), ())), preferred_element_type=F32)


def _rmsnorm(x, g):
    return x * lax.rsqrt(jnp.mean(x * x, axis=-1, keepdims=True) + NORM_EPS) * g


def _params(sem):
    return pltpu.CompilerParams(dimension_semantics=sem, vmem_limit_bytes=VMEM_LIMIT)


def _ffn_kernel(*refs, final):
    x_ref, g_ref, wi_ref, wo_ref = refs[:4]
    o_ref = refs[-1]
    dff = wo_ref.shape[0]
    x = x_ref[...]
    h = _rmsnorm(x, g_ref[...]).astype(BF16)
    u = _dot(h, wi_ref[:, :dff])
    v = _dot(h, wi_ref[:, dff:])
    a = (u * jax.nn.sigmoid(u)) * v
    y = x + 0.5 * _dot(a.astype(BF16), wo_ref[...])
    if final:
        y = _rmsnorm(y, refs[4][...])
    o_ref[...] = y


def _ffn(x2, gain, w_in, w_out, final_gain=None, tm=512):
    m, d = x2.shape
    final = final_gain is not None
    whole = lambda a: pl.BlockSpec(a.shape, lambda i: (0, 0))
    in_specs = [pl.BlockSpec((tm, d), lambda i: (i, 0)), pl.BlockSpec((1, d), lambda i: (0, 0)), whole(w_in), whole(w_out)]
    args = [x2, gain.reshape(1, d), w_in, w_out]
    if final:
        in_specs.append(pl.BlockSpec((1, d), lambda i: (0, 0)))
        args.append(final_gain.reshape(1, d))
    return pl.pallas_call(
        functools.partial(_ffn_kernel, final=final),
        grid=(m // tm,),
        in_specs=in_specs,
        out_specs=pl.BlockSpec((tm, d), lambda i: (i, 0)),
        out_shape=jax.ShapeDtypeStruct((m, d), F32),
        compiler_params=_params(("parallel",)),
        name="ffn_final" if final else "ffn",
    )(*args)


def _rope_coeffs(pos_col, inv_lane):
    ang = pos_col.astype(F32) * inv_lane
    c, s = jnp.cos(ang), jnp.sin(ang)
    d = lax.broadcasted_iota(jnp.int32, ang.shape, 1) & (HEAD_DIM - 1)
    half = ROPE_DIM // 2
    coef_c = jnp.where(d < ROPE_DIM, c, 1.0)
    coef_hi = jnp.where(d < half, -s, 0.0)
    coef_lo = jnp.where((d >= half) & (d < ROPE_DIM), s, 0.0)
    return coef_c, coef_hi, coef_lo


def _proj_kernel(*refs, rope, segments):
    n_in = 5 if rope else 3
    x_ref, g_ref, w_ref = refs[:3]
    o_refs = refs[n_in:n_in + len(segments)]
    h = _rmsnorm(x_ref[...], g_ref[...]).astype(BF16)
    y = _dot(h, w_ref[...])
    if len(refs) == n_in + len(segments):
        o_refs[0][...] = y.astype(o_refs[0].dtype)
        return
    y_scr = refs[-1]
    if rope:
        coef_c, coef_hi, coef_lo = _rope_coeffs(refs[3][...], refs[4][...])
        half = ROPE_DIM // 2
        for t in range(y.shape[1] // LANES):
            yt = y[:, t * LANES:(t + 1) * LANES]
            hi = pltpu.roll(yt, LANES - half, axis=1)
            lo = pltpu.roll(yt, half, axis=1)
            y_scr[t] = yt * coef_c + hi * coef_hi + lo * coef_lo
    else:
        for t in range(y.shape[1] // LANES):
            y_scr[t] = y[:, t * LANES:(t + 1) * LANES]
    tm = y.shape[0]
    for o_ref, (c0, w, dil) in zip(o_refs, segments):
        if dil == 0:
            yt = y_scr[c0 // LANES].T
            pad = jnp.where(lax.broadcasted_iota(jnp.int32, (VAL_ROWS - HEAD_DIM, tm), 0) == 0, 1.0, 0.0)
            for g in range(NSA_KV_GROUPS):
                rows = jnp.concatenate([yt[g * HEAD_DIM:(g + 1) * HEAD_DIM], pad], axis=0)
                o_ref[g] = rows.astype(o_ref.dtype)
            continue
        for k in range(w // LANES):
            src = c0 // LANES + k
            for res in range(dil):
                rows = y_scr[src] if dil == 1 else y_scr[src, pl.ds(res, tm // dil, stride=dil), :]
                o_ref[:, res * w + k * LANES:res * w + (k + 1) * LANES] = rows.astype(o_ref.dtype)


def _proj(x2, gain, w, out_dtype, segments=None, pos=None, inv_lane=None, tm=512, name="proj"):
    m, d = x2.shape
    n = w.shape[1]
    rope = pos is not None
    segments = segments or ((0, n, 1),)
    staged = rope or len(segments) > 1 or segments[0][2] != 1
    in_specs = [
        pl.BlockSpec((tm, d), lambda i: (i, 0)),
        pl.BlockSpec((1, d), lambda i: (0, 0)),
        pl.BlockSpec((d, n), lambda i: (0, 0)),
    ]
    args = [x2, gain.reshape(1, d), w]
    if rope:
        in_specs += [pl.BlockSpec((tm, 1), lambda i: (i, 0)), pl.BlockSpec((1, LANES), lambda i: (0, 0))]
        args += [pos, inv_lane]
    outs = pl.pallas_call(
        functools.partial(_proj_kernel, rope=rope, segments=segments),
        grid=(m // tm,),
        in_specs=in_specs,
        out_specs=[pl.BlockSpec((tm // dil, dil * wd), lambda i: (i, 0)) if dil else
                   pl.BlockSpec((NSA_KV_GROUPS, VAL_ROWS, tm), lambda i: (0, 0, i)) for _, wd, dil in segments],
        out_shape=[jax.ShapeDtypeStruct((m // dil, dil * wd) if dil else (NSA_KV_GROUPS, VAL_ROWS, m), out_dtype)
                   for _, wd, dil in segments],
        scratch_shapes=[pltpu.VMEM((n // LANES, tm, LANES), F32)] if staged else [],
        compiler_params=_params(("parallel",)),
        name=name,
    )(*args)
    return outs


def _compress_kernel(ksrc_ref, vsrc_ref, pos_ref, w1_ref, w2_ref, kc_ref, vct_ref):
    n_seg = kc_ref.shape[1]
    outs = []
    for c, src in enumerate((ksrc_ref, vsrc_ref)):
        first = second = None
        for p in range(CMP_STRIDE):
            tok = src[pl.ds(p, n_seg, stride=CMP_STRIDE), :]
            a = _dot((tok + pos_ref[c, p:p + 1, :]).astype(BF16), w1_ref[c, p])
            b = _dot((tok + pos_ref[c, CMP_STRIDE + p:CMP_STRIDE + p + 1, :]).astype(BF16), w1_ref[c, CMP_STRIDE + p])
            first = a if first is None else first + a
            second = b if second is None else second + b
        pre = first + pltpu.roll(second, n_seg - 1, axis=0)
        hid = pre * jax.nn.sigmoid(pre)
        outs.append(_dot(hid.astype(BF16), w2_ref[c]))
    kc_ref[0] = outs[0].astype(kc_ref.dtype)
    vct_ref[0] = outs[1].T.astype(vct_ref.dtype)


def _compress(pa, b, ksrc_blk, pos, w1, w2):
    t = pa.shape[0] // b
    n_seg = t // CMP_STRIDE
    whole = lambda a: pl.BlockSpec(a.shape, lambda i: (0,) * a.ndim)
    return pl.pallas_call(
        _compress_kernel,
        grid=(b,),
        in_specs=[pl.BlockSpec((t, LANES), lambda i: (i, ksrc_blk)), pl.BlockSpec((t, LANES), lambda i: (i, ksrc_blk + 1)),
                  whole(pos), whole(w1), whole(w2)],
        out_specs=[pl.BlockSpec((1, n_seg, LANES), lambda i: (i, 0, 0)), pl.BlockSpec((1, LANES, n_seg), lambda i: (i, 0, 0))],
        out_shape=[jax.ShapeDtypeStruct((b, n_seg, LANES), BF16), jax.ShapeDtypeStruct((b, LANES, n_seg), BF16)],
        compiler_params=_params(("parallel",)),
        name="compress",
    )(pa, pa, pos, w1, w2)


def _both_groups(w):
    z = jnp.zeros_like(w)
    return jnp.concatenate([jnp.concatenate([w, z], axis=-1), jnp.concatenate([z, w], axis=-1)], axis=-2)


MASK_BIAS = NEG
M_INIT = -1e29


def _stack_heads_t(q, g):
    zeros = jnp.zeros((HEAD_DIM, q.shape[0]), F32)
    parts = []
    for h in range(NSA_HPG):
        hg = g * NSA_HPG + h
        tile_t = (q[:, (hg // 2) * LANES:(hg // 2 + 1) * LANES] * (SCALE * LOG2E)).T
        feat = tile_t[(hg % 2) * HEAD_DIM:(hg % 2 + 1) * HEAD_DIM]
        parts.append(jnp.concatenate([feat, zeros] if g == 0 else [zeros, feat], axis=0))
    return jnp.concatenate(parts, axis=1).astype(BF16)


def _softmax_t(s, bias, hg):
    sb = s + jnp.concatenate([bias] * hg, axis=1)
    m = jnp.maximum(jnp.max(sb, axis=0, keepdims=True), M_INIT)
    e = jnp.exp2(sb - m)
    d = jnp.sum(e, axis=0, keepdims=True)
    return e, 1.0 / jnp.where(d > 0, d, 1.0)


def _nsa_kernel(qp_ref, qr_ref, gate_ref, kc_ref, vct_ref, ks_ref, vst_ref, kw_ref, vwt_ref, et_ref,
                o_ref, sa_scr, sb_scr, oc_scr, psum_scr, *, n_sel, win_keys):
    qb = pl.program_id(1)
    t0 = qb * Q_BLOCK
    nq = Q_BLOCK
    hg = NSA_HPG
    tk = SEL_KEY_TILE
    n_cmp_pad = kc_ref.shape[1]
    t_row = t0 + lax.broadcasted_iota(jnp.int32, (1, nq), 1)
    blk = lax.broadcasted_iota(jnp.int32, (LANES, nq), 0)

    q_plain = qp_ref[0]
    q_rope = qr_ref[0].astype(F32)
    gates_t = jax.nn.sigmoid(gate_ref[0]).T

    cmp_chunk = min(LANES, n_cmp_pad)
    n_cmp_rows = (qb + 1) * (Q_BLOCK // CMP_STRIDE)
    n_cmp_chunks = jnp.minimum((n_cmp_rows + cmp_chunk - 1) // cmp_chunk, n_cmp_pad // cmp_chunk)
    cur = t_row >> 6
    forced = (blk == 0) | (blk == cur) | (blk == cur - 1)
    valid = blk <= cur

    win_start = pl.multiple_of(jnp.maximum(t0 - WIN, 0), Q_BLOCK)
    diff_w = t_row - (win_start + lax.broadcasted_iota(jnp.int32, (win_keys, nq), 0))
    bias_w = jnp.where((diff_w >= 0) & (diff_w < WIN), 0.0, MASK_BIAS)

    first_blk = qb * (Q_BLOCK // SLC_BLOCK)
    n_sweep = (t0 + tk - 1) // tk
    n_key_tiles = ks_ref.shape[1] // tk
    bias_own = jnp.where(lax.broadcasted_iota(jnp.int32, (nq, nq), 0) <= lax.broadcasted_iota(jnp.int32, (nq, nq), 1),
                         0.0, MASK_BIAS)

    groups = range(NSA_KV_GROUPS)
    qt_plains = [_stack_heads_t(q_plain, g) for g in groups]
    qt_ropes = [_stack_heads_t(q_rope, g) for g in groups]

    for c in range(1, n_cmp_pad // cmp_chunk + 1):
        @pl.when(n_cmp_chunks == c)
        def _(c=c):
            rows = c * cmp_chunk
            cmp_end = lax.broadcasted_iota(jnp.int32, (rows, nq), 0) * CMP_STRIDE + (CMP_BLOCK - 1)
            bias_c = jnp.where(cmp_end <= t_row, 0.0, MASK_BIAS)
            s_cs = [_dot(kc_ref[0, 0:rows, :], qt_plains[g]) for g in groups]
            for g in groups:
                e_c, r_c = _softmax_t(s_cs[g], bias_c, hg)
                p_c = e_c * r_c
                oc_scr[g] = _dot(vct_ref[0, :, 0:rows], p_c.astype(BF16))
                p_sum = p_c[:, 0:nq]
                for h in range(1, hg):
                    p_sum = p_sum + p_c[:, h * nq:(h + 1) * nq]
                psum_scr[g, 0:rows, :] = p_sum
                if rows < psum_scr.shape[1]:
                    psum_scr[g, rows:, :] = jnp.zeros((psum_scr.shape[1] - rows, nq), F32)

    per = SLC_BLOCK // CMP_STRIDE
    pools = []
    for g in groups:
        parts = [psum_scr[g, pl.ds(k, LANES, stride=per), :] for k in range(per)]
        tail_prev = jnp.where(blk == 0, 0.0, pltpu.roll(parts[per - 1], 1, axis=0))
        imp = parts[0] + parts[1] + parts[2] + 0.5 * parts[3] + 0.5 * tail_prev
        pools.append(jnp.where(valid & ~forced, imp, -FORCE))

    picks = [forced & valid] * NSA_KV_GROUPS
    for _ in range(n_sel - N_FORCED):
        for g in groups:
            best = jnp.max(pools[g], axis=0, keepdims=True)
            first = jnp.min(jnp.where(pools[g] == best, blk, LANES), axis=0, keepdims=True)
            hit = (blk == first) & (best > -0.5 * FORCE)
            picks[g] = picks[g] | hit
            pools[g] = jnp.where(hit, -FORCE, pools[g])

    s_ws = [_dot(kw_ref[0, pl.ds(win_start, win_keys), :], qt) + jnp.concatenate([bias_w] * hg, axis=1) for qt in qt_ropes]
    e_ws = [jnp.exp2(s_w - jnp.max(s_w, axis=0, keepdims=True)).astype(BF16) for s_w in s_ws]
    acc_ws = [_dot(vwt_ref[g, :, pl.ds(win_start, win_keys)], e_ws[g]) for g in groups]
    o_ws = [acc_w[0:HEAD_DIM] * (1.0 / acc_w[HEAD_DIM:HEAD_DIM + 1]) for acc_w in acc_ws]

    q_augs = []
    for g in groups:
        bias_blk = jnp.where(picks[g] & (blk < first_blk), 0.0, MASK_BIAS).astype(BF16)
        q_augs.append(jnp.concatenate([qt_ropes[g], jnp.concatenate([bias_blk] * hg, axis=1)], axis=0))

    def absorb(s, vt, carry):
        m, acc = carry
        m_new = jnp.maximum(m, jnp.max(s, axis=0, keepdims=True))
        e = jnp.exp2(s - m_new)
        return m_new, jnp.exp2(m - m_new) * acc + _dot(vt, e.astype(BF16))

    def key_tile(kt):
        return pl.multiple_of(jnp.minimum(kt, n_key_tiles - 1) * tk, tk)

    own = pl.ds(pl.multiple_of(t0, nq), nq)
    acc_ss = []
    for g in groups:
        def scores_into(scr, kt, g=g):
            k0 = key_tile(kt)
            scr[...] = _dot(jnp.concatenate([ks_ref[0, pl.ds(k0, tk), :], et_ref[pl.ds(k0, tk), :]], axis=1), q_augs[g])

        def absorb_tile(scr, kt, carry, g=g):
            return absorb(scr[...], vst_ref[g, :, pl.ds(key_tile(kt), tk)], carry)

        def sweep_pair(i, carry):
            scores_into(sb_scr, 2 * i + 1)
            carry = absorb_tile(sa_scr, 2 * i, carry)
            scores_into(sa_scr, 2 * i + 2)
            return absorb_tile(sb_scr, 2 * i + 1, carry)

        init = (jnp.full((1, hg * nq), M_INIT, F32), jnp.zeros((VAL_ROWS, hg * nq), F32))
        scores_into(sa_scr, 0)
        carry = lax.fori_loop(0, n_sweep // 2, sweep_pair, init)
        carry = lax.cond(n_sweep % 2 == 1, lambda c: absorb_tile(sa_scr, n_sweep - 1, c), lambda c: c, carry)
        s_own = _dot(ks_ref[0, own, :], qt_ropes[g]) + jnp.concatenate([bias_own] * hg, axis=1)
        acc_ss.append(absorb(s_own, vst_ref[g, :, own], carry)[1])

    for g in groups:
        feat = slice(g * HEAD_DIM, (g + 1) * HEAD_DIM)
        o_c = oc_scr[g]
        o_s = acc_ss[g][0:HEAD_DIM] * (1.0 / acc_ss[g][HEAD_DIM:HEAD_DIM + 1])
        mixes = []
        for h in range(hg):
            row = g * hg * 3 + h * 3
            cols = slice(h * nq, (h + 1) * nq)
            mixes.append(gates_t[row:row + 1] * o_c[feat, cols] + gates_t[row + 1:row + 2] * o_s[:, cols]
                         + gates_t[row + 2:row + 3] * o_ws[g][:, cols])
        for hp in range(hg // 2):
            c = (g * hg) // 2 + hp
            pair = jnp.concatenate([mixes[2 * hp], mixes[2 * hp + 1]], axis=0)
            o_ref[0, :, c * LANES:(c + 1) * LANES] = pair.T.astype(o_ref.dtype)


def _nsa(pa3, r3, kc, vct, vst, vwt, expand, *, qp_blk, gate_blk, ks_blk, kw_blk):
    b, t, _ = pa3.shape
    n_cmp_pad = kc.shape[1]
    n_sel = min(SLC_TOPN, t // SLC_BLOCK)
    win_keys = min(WIN + Q_BLOCK, t)
    keys = lambda blk: pl.BlockSpec((1, t, LANES), lambda i, j, blk=blk: (i, 0, blk))
    vals_t = pl.BlockSpec((NSA_KV_GROUPS, VAL_ROWS, t), lambda i, j: (0, 0, i))
    return pl.pallas_call(
        functools.partial(_nsa_kernel, n_sel=n_sel, win_keys=win_keys),
        grid=(b, t // Q_BLOCK),
        in_specs=[
            pl.BlockSpec((1, Q_BLOCK, NSA_Q), lambda i, j: (i, j, qp_blk)),
            pl.BlockSpec((1, Q_BLOCK, NSA_Q), lambda i, j: (i, j, 0)),
            pl.BlockSpec((1, Q_BLOCK, LANES), lambda i, j: (i, j, gate_blk)),
            pl.BlockSpec((1, n_cmp_pad, LANES), lambda i, j: (i, 0, 0)),
            pl.BlockSpec((1, LANES, n_cmp_pad), lambda i, j: (i, 0, 0)),
            keys(ks_blk), vals_t, keys(kw_blk), vals_t,
            pl.BlockSpec((t, LANES), lambda i, j: (0, 0)),
        ],
        out_specs=pl.BlockSpec((1, Q_BLOCK, NSA_Q), lambda i, j: (i, j, 0)),
        out_shape=jax.ShapeDtypeStruct((b, t, NSA_Q), BF16),
        scratch_shapes=[
                        pltpu.VMEM((SEL_KEY_TILE, NSA_HPG * Q_BLOCK), F32), pltpu.VMEM((SEL_KEY_TILE, NSA_HPG * Q_BLOCK), F32),
                        pltpu.VMEM((NSA_KV_GROUPS, LANES, NSA_HPG * Q_BLOCK), F32),
                        pltpu.VMEM((NSA_KV_GROUPS, max(n_cmp_pad, LANES * SLC_BLOCK // CMP_STRIDE), Q_BLOCK), F32)],
        compiler_params=_params(("parallel", "arbitrary")),
        name="nsa",
    )(pa3, r3, pa3, kc, vct, r3, vst, r3, vwt, expand)


def _dil_kernel(q_ref, kp_ref, kc_ref, vp_ref, vc_ref, o_ref, lse_ref, *, span):
    ub = pl.program_id(2)
    nq = DIL_BLOCK
    lane = lax.broadcasted_iota(jnp.int32, (nq, LANES), 1)
    row = lax.broadcasted_iota(jnp.int32, (2 * nq, 2 * nq), 0) & (nq - 1)
    col = lax.broadcasted_iota(jnp.int32, (2 * nq, 2 * nq), 1)
    diff = row + nq - col
    band = (diff >= 0) & (diff <= span)
    bias = jnp.where(band, 0.0, MASK_BIAS)
    bias_first = jnp.where(band & ((col >= nq) | (ub > 0)), 0.0, MASK_BIAS)
    for j in range(q_ref.shape[0] // nq):
        rows = slice(j * nq, (j + 1) * nq)
        before = slice((j - 1) * nq, j * nq)
        for hp in range(DIL_HEADS // 2):
            cols = slice(hp * LANES, (hp + 1) * LANES)
            qt = (q_ref[rows, cols].astype(F32) * (SCALE * LOG2E)).astype(BF16)
            zero = jnp.zeros_like(qt)
            qs = jnp.concatenate([jnp.where(lane < HEAD_DIM, qt, zero), jnp.where(lane >= HEAD_DIM, qt, zero)], axis=0)
            k = jnp.concatenate([kp_ref[:, cols] if j == 0 else kc_ref[before, cols], kc_ref[rows, cols]], axis=0)
            v = jnp.concatenate([vp_ref[:, cols] if j == 0 else vc_ref[before, cols], vc_ref[rows, cols]], axis=0)
            s = _dot_nt(qs, k) + (bias_first if j == 0 else bias)
            m = jnp.max(s, axis=-1, keepdims=True)
            e = jnp.exp2(s - m)
            d = jnp.sum(e, axis=-1, keepdims=True)
            o = _dot(e.astype(BF16), v) * (1.0 / d)
            lse = (m + jnp.log2(d)) * (1.0 / LOG2E)
            o_ref[rows, cols] = jnp.where(lane < HEAD_DIM, o[:nq], o[nq:]).astype(o_ref.dtype)
            lse_ref[rows, cols] = jnp.where(lane < HEAD_DIM, lse[:nq], lse[nq:])


def _dilated(qd, kd, vd, b, dil, span):
    rows = qd.shape[0]
    ln = rows // b
    ub = min(ln, 4 * DIL_BLOCK)
    nub = ln // ub
    per = ub // DIL_BLOCK
    cur = pl.BlockSpec((ub, DIL_W), lambda i, r, u: (i * nub + u, r))
    prev = pl.BlockSpec((DIL_BLOCK, DIL_W), lambda i, r, u: (jnp.maximum((i * nub + u) * per - 1, 0), r))
    o, lse = pl.pallas_call(
        functools.partial(_dil_kernel, span=span),
        grid=(b, dil, nub),
        in_specs=[cur, prev, cur, prev, cur],
        out_specs=[cur, cur],
        out_shape=[jax.ShapeDtypeStruct(qd.shape, F32), jax.ShapeDtypeStruct(qd.shape, F32)],
        compiler_params=_params(("parallel", "parallel", "arbitrary")),
        name=f"dilated_{dil}",
    )(qd, kd, kd, vd, vd)
    return o, lse


def _out_kernel(*refs, dils):
    ng = len(dils)
    x_ref, yn_ref = refs[:2]
    o_refs, l_refs = refs[2:2 + ng], refs[2 + ng:2 + 2 * ng]
    g0_ref, g1_ref, wn_ref, wd_ref, wo_ref, o_ref = refs[2 + 2 * ng:8 + 2 * ng]
    scrs = list(refs[8 + 2 * ng:])

    def natural(ref, dil):
        if dil == 1:
            return ref[...]
        scr = scrs.pop(0)
        n = scr.shape[1] // dil
        for res in range(dil):
            for k in range(DIL_W // LANES):
                scr[k, pl.ds(res, n, stride=dil), :] = ref[:, res * DIL_W + k * LANES:res * DIL_W + (k + 1) * LANES]
        return jnp.concatenate([scr[k] for k in range(DIL_W // LANES)], axis=1)

    outs = [natural(r, d) for r, d in zip(o_refs, dils)]
    lses = [natural(r, d) for r, d in zip(l_refs, dils)]
    m = functools.reduce(jnp.maximum, lses)
    ws = [jnp.exp(l - m) for l in lses]
    inv_den = 1.0 / functools.reduce(lambda a, c: a + c, ws)
    y_dil = functools.reduce(lambda a, c: a + c, [w * o for w, o in zip(ws, outs)]) * inv_den
    merged = (jax.nn.sigmoid(g0_ref[...]) * _dot(yn_ref[...], wn_ref[...])
              + jax.nn.sigmoid(g1_ref[...]) * _dot(y_dil.astype(BF16), wd_ref[...]))
    o_ref[...] = x_ref[...] + _dot(merged.astype(BF16), wo_ref[...])


def _out_proj(x2, y_nsa, outs, lses, dils, pa, w_bn, w_bd, w_o, tm=512):
    m, d = x2.shape
    row = lambda w: pl.BlockSpec((tm, w), lambda i: (i, 0))
    dil_row = lambda dil: pl.BlockSpec((tm // dil, dil * DIL_W), lambda i: (i, 0))
    whole = lambda a: pl.BlockSpec(a.shape, lambda i: (0, 0))
    n_staged = 2 * sum(dil != 1 for dil in dils)
    return pl.pallas_call(
        functools.partial(_out_kernel, dils=tuple(dils)),
        grid=(m // tm,),
        in_specs=[row(d), row(NSA_Q)] + [dil_row(dil) for dil in dils] * 2
        + [pl.BlockSpec((tm, d), lambda i: (i, 0)), pl.BlockSpec((tm, d), lambda i: (i, 1))]
        + [whole(w_bn), whole(w_bd), whole(w_o)],
        out_specs=row(d),
        out_shape=jax.ShapeDtypeStruct((m, d), F32),
        scratch_shapes=[pltpu.VMEM((DIL_W // LANES, tm, LANES), F32)] * n_staged,
        compiler_params=_params(("parallel",)),
        name="out_proj",
    )(x2, y_nsa, *outs, *lses, pa, pa, w_bn, w_bd, w_o)


def _mixer(x2, pos_col, inv_lane, b, t, norm_mix, w_in, cmp_pos, cmp_w1, cmp_w2, w_branch_nsa, w_branch_dil, w_out):
    d = x2.shape[1]
    dh = HEAD_DIM
    o_q, o_kv, o_gn = 0, NSA_Q, NSA_Q + 6 * NSA_KV
    o_d = o_gn + 3 * NSA_HEADS
    o_gm = o_d + 3 * DIL_GROUPS * DIL_W
    kv = lambda i: w_in[:, o_kv + i * NSA_KV:o_kv + (i + 1) * NSA_KV]
    dd = lambda i: w_in[:, o_d + i * DIL_GROUPS * DIL_W:o_d + (i + 1) * DIL_GROUPS * DIL_W]
    w_q = w_in[:, o_q:o_q + NSA_Q]
    gn_pad = jnp.zeros((d, 2 * LANES - 3 * NSA_HEADS), w_in.dtype)
    w_a = jnp.concatenate([w_in[:, o_gm:o_gm + 2 * d], w_q, kv(0), kv(1), w_in[:, o_gn:o_gn + 3 * NSA_HEADS], gn_pad],
                          axis=1).astype(BF16)
    w_b = jnp.concatenate([w_q, kv(2), kv(4), dd(0), dd(1)], axis=1).astype(BF16)
    w_c = jnp.concatenate([kv(3), kv(5), dd(2)], axis=1).astype(BF16)
    dils = [dil for _, dil in DIL_PAIRS]
    n_r = NSA_Q + 2 * NSA_KV
    seg_b = ((0, n_r, 1),) + tuple((n_r + i * DIL_W, DIL_W, dil) for i, dil in enumerate(dils)) \
        + tuple((n_r + (DIL_GROUPS + i) * DIL_W, DIL_W, dil) for i, dil in enumerate(dils))
    seg_c = ((0, NSA_KV, 0), (NSA_KV, NSA_KV, 0)) + tuple((2 * NSA_KV + i * DIL_W, DIL_W, dil) for i, dil in enumerate(dils))

    pa, = _proj(x2, norm_mix, w_a, F32, name="proj_f32")
    rb, *qk_d = _proj(x2, norm_mix, w_b, BF16, segments=seg_b, pos=pos_col, inv_lane=inv_lane, name="proj_rope")
    vst, vwt, *v_d = _proj(x2, norm_mix, w_c, BF16, segments=seg_c, name="proj_bf16")

    pos = jnp.concatenate([cmp_pos] * NSA_KV_GROUPS, axis=-1)
    w1 = _both_groups(cmp_w1.reshape(2, CMP_BLOCK, dh, -1)).astype(BF16)
    kc, vct = _compress(pa, b, (2 * d + NSA_Q) // LANES, pos, w1, _both_groups(cmp_w2).astype(BF16))

    pa3 = pa.reshape(b, t, -1)
    r3 = rb.reshape(b, t, -1)
    expand = jnp.asarray(np.arange(t)[:, None] // SLC_BLOCK == np.arange(LANES)[None, :], BF16)
    y_nsa = _nsa(pa3, r3, kc, vct, vst, vwt, expand,
                 qp_blk=2 * d // NSA_Q, gate_blk=(2 * d + NSA_Q + 2 * NSA_KV) // LANES,
                 ks_blk=NSA_Q // LANES, kw_blk=NSA_Q // LANES + 1)

    outs, lses = [], []
    for gidx, (window, dil) in enumerate(DIL_PAIRS):
        o, lse = _dilated(qk_d[gidx], qk_d[DIL_GROUPS + gidx], v_d[gidx], b, dil, window // dil)
        outs.append(o)
        lses.append(lse)

    return _out_proj(x2, y_nsa.reshape(b * t, NSA_Q), outs, lses, dils, pa,
                     w_branch_nsa.astype(BF16), w_branch_dil.astype(BF16), w_out.astype(BF16))


def kernel(x, positions, norm_ffn1, ffn1_w_in, ffn1_w_out, norm_mix, w_in, cmp_pos, cmp_w1, cmp_w2,
           w_branch_nsa, w_branch_dil, w_out, norm_ffn2, ffn2_w_in, ffn2_w_out, norm_final):
    b, t, d = x.shape
    depth = norm_ffn1.shape[0]
    x2 = x.reshape(b * t, d)
    pos_col = positions.reshape(b * t, 1).astype(jnp.int32)
    inv = jnp.power(ROPE_THETA, -jnp.arange(0, ROPE_DIM, 2, dtype=F32) / ROPE_DIM)
    inv_head = jnp.concatenate([inv, inv, jnp.zeros((HEAD_DIM - ROPE_DIM,), F32)])
    inv_lane = jnp.tile(inv_head, LANES // HEAD_DIM).reshape(1, LANES)
    for l in range(depth):
        last = l == depth - 1
        x2 = _ffn(x2, norm_ffn1[l], ffn1_w_in[l].astype(BF16), ffn1_w_out[l].astype(BF16))
        x2 = _mixer(x2, pos_col, inv_lane, b, t, norm_mix[l], w_in[l], cmp_pos[l], cmp_w1[l], cmp_w2[l],
                    w_branch_nsa[l], w_branch_dil[l], w_out[l])
        x2 = _ffn(x2, norm_ffn2[l], ffn2_w_in[l].astype(BF16), ffn2_w_out[l].astype(BF16),
                  final_gain=norm_final if last else None)
    if depth == 0:
        raise ValueError("depth must be positive")
    return x2.reshape(b, t, d)
```

```python
import functools

import numpy as np
import jax
import jax.numpy as jnp
from jax import lax
from jax.experimental import pallas as pl
from jax.experimental.pallas import tpu as pltpu

HEAD_DIM = 64
ROPE_DIM = HEAD_DIM // 4
ROPE_THETA = 500000.0
NORM_EPS = 1e-6

NSA_HEADS = 8
NSA_KV_GROUPS = 2
NSA_HPG = NSA_HEADS // NSA_KV_GROUPS
CMP_BLOCK = 32
CMP_STRIDE = 16
SLC_BLOCK = 64
SLC_TOPN = 16
WIN = 512
Q_BLOCK = 128
FORCE = 1e9
N_FORCED = 3
NEG = -1e30

DIL_PAIRS = ((128, 1), (512, 4), (2048, 16))
DIL_GROUPS = len(DIL_PAIRS)
DIL_HEADS = 4
DIL_BLOCK = 128
DIL_STEP_ROWS = 16 * DIL_BLOCK

NSA_Q = NSA_HEADS * HEAD_DIM
NSA_KV = NSA_KV_GROUPS * HEAD_DIM
DIL_W = DIL_HEADS * HEAD_DIM

LANES = 128
SEL_KEY_TILE = 1024
VAL_ROWS = HEAD_DIM + 16
VMEM_LIMIT = 56 * 1024 * 1024

F32 = jnp.float32
BF16 = jnp.bfloat16
SCALE = HEAD_DIM ** -0.5
LOG2E = 1.4426950408889634


def _dot(a, b):
    return jnp.dot(a, b, preferred_element_type=F32)


def _dot_nt(a, b):
    return lax.dot_general(a, b, (((1,), (1,)), ((), ())), preferred_element_type=F32)


def _rmsnorm(x, g):
    return x * lax.rsqrt(jnp.mean(x * x, axis=-1, keepdims=True) + NORM_EPS) * g


def _params(sem):
    return pltpu.CompilerParams(dimension_semantics=sem, vmem_limit_bytes=VMEM_LIMIT)


def _ffn_kernel(*refs, final):
    x_ref, g_ref, wi_ref, wo_ref = refs[:4]
    o_ref = refs[-1]
    dff = wo_ref.shape[0]
    x = x_ref[...]
    h = _rmsnorm(x, g_ref[...]).astype(BF16)
    u = _dot(h, wi_ref[:, :dff])
    v = _dot(h, wi_ref[:, dff:])
    a = (u * jax.nn.sigmoid(u)) * v
    y = x + 0.5 * _dot(a.astype(BF16), wo_ref[...])
    if final:
        y = _rmsnorm(y, refs[4][...])
    o_ref[...] = y


def _ffn(x2, gain, w_in, w_out, final_gain=None, tm=1024):
    m, d = x2.shape
    final = final_gain is not None
    whole = lambda a: pl.BlockSpec(a.shape, lambda i: (0, 0))
    in_specs = [pl.BlockSpec((tm, d), lambda i: (i, 0)), pl.BlockSpec((1, d), lambda i: (0, 0)), whole(w_in), whole(w_out)]
    args = [x2, gain.reshape(1, d), w_in, w_out]
    if final:
        in_specs.append(pl.BlockSpec((1, d), lambda i: (0, 0)))
        args.append(final_gain.reshape(1, d))
    return pl.pallas_call(
        functools.partial(_ffn_kernel, final=final),
        grid=(m // tm,),
        in_specs=in_specs,
        out_specs=pl.BlockSpec((tm, d), lambda i: (i, 0)),
        out_shape=jax.ShapeDtypeStruct((m, d), F32),
        compiler_params=_params(("parallel",)),
        name="ffn_final" if final else "ffn",
    )(*args)


def _rope_coeffs(pos_col, inv_lane):
    ang = pos_col.astype(F32) * inv_lane
    c, s = jnp.cos(ang), jnp.sin(ang)
    d = lax.broadcasted_iota(jnp.int32, ang.shape, 1) & (HEAD_DIM - 1)
    half = ROPE_DIM // 2
    coef_c = jnp.where(d < ROPE_DIM, c, 1.0)
    coef_hi = jnp.where(d < half, -s, 0.0)
    coef_lo = jnp.where((d >= half) & (d < ROPE_DIM), s, 0.0)
    return coef_c, coef_hi, coef_lo


def _proj_kernel(*refs, rope, segments):
    n_in = 5 if rope else 3
    x_ref, g_ref, w_ref = refs[:3]
    o_refs = refs[n_in:n_in + len(segments)]
    h = _rmsnorm(x_ref[...], g_ref[...]).astype(BF16)
    y = _dot(h, w_ref[...])
    if len(refs) == n_in + len(segments):
        o_refs[0][...] = y.astype(o_refs[0].dtype)
        return
    y_scr = refs[-1]
    if rope:
        coef_c, coef_hi, coef_lo = _rope_coeffs(refs[3][...], refs[4][...])
        half = ROPE_DIM // 2
        for t in range(y.shape[1] // LANES):
            yt = y[:, t * LANES:(t + 1) * LANES]
            hi = pltpu.roll(yt, LANES - half, axis=1)
            lo = pltpu.roll(yt, half, axis=1)
            y_scr[t] = yt * coef_c + hi * coef_hi + lo * coef_lo
    else:
        for t in range(y.shape[1] // LANES):
            y_scr[t] = y[:, t * LANES:(t + 1) * LANES]
    tm = y.shape[0]
    for o_ref, (c0, w, dil) in zip(o_refs, segments):
        if dil == 0:
            yt = y_scr[c0 // LANES].T
            pad = jnp.where(lax.broadcasted_iota(jnp.int32, (VAL_ROWS - HEAD_DIM, tm), 0) == 0, 1.0, 0.0)
            for g in range(NSA_KV_GROUPS):
                rows = jnp.concatenate([yt[g * HEAD_DIM:(g + 1) * HEAD_DIM], pad], axis=0)
                o_ref[g] = rows.astype(o_ref.dtype)
            continue
        for k in range(w // LANES):
            src = c0 // LANES + k
            for res in range(dil):
                rows = y_scr[src] if dil == 1 else y_scr[src, pl.ds(res, tm // dil, stride=dil), :]
                o_ref[:, res * w + k * LANES:res * w + (k + 1) * LANES] = rows.astype(o_ref.dtype)


def _proj(x2, gain, w, out_dtype, segments=None, pos=None, inv_lane=None, tm=512, name="proj"):
    m, d = x2.shape
    n = w.shape[1]
    rope = pos is not None
    segments = segments or ((0, n, 1),)
    staged = rope or len(segments) > 1 or segments[0][2] != 1
    in_specs = [
        pl.BlockSpec((tm, d), lambda i: (i, 0)),
        pl.BlockSpec((1, d), lambda i: (0, 0)),
        pl.BlockSpec((d, n), lambda i: (0, 0)),
    ]
    args = [x2, gain.reshape(1, d), w]
    if rope:
        in_specs += [pl.BlockSpec((tm, 1), lambda i: (i, 0)), pl.BlockSpec((1, LANES), lambda i: (0, 0))]
        args += [pos, inv_lane]
    outs = pl.pallas_call(
        functools.partial(_proj_kernel, rope=rope, segments=segments),
        grid=(m // tm,),
        in_specs=in_specs,
        out_specs=[pl.BlockSpec((tm // dil, dil * wd), lambda i: (i, 0)) if dil else
                   pl.BlockSpec((NSA_KV_GROUPS, VAL_ROWS, tm), lambda i: (0, 0, i)) for _, wd, dil in segments],
        out_shape=[jax.ShapeDtypeStruct((m // dil, dil * wd) if dil else (NSA_KV_GROUPS, VAL_ROWS, m), out_dtype)
                   for _, wd, dil in segments],
        scratch_shapes=[pltpu.VMEM((n // LANES, tm, LANES), F32)] if staged else [],
        compiler_params=_params(("parallel",)),
        name=name,
    )(*args)
    return outs


def _compress_kernel(ksrc_ref, vsrc_ref, pos_ref, w1_ref, w2_ref, kc_ref, vct_ref):
    n_seg = kc_ref.shape[1]
    outs = []
    for c, src in enumerate((ksrc_ref, vsrc_ref)):
        first = second = None
        for p in range(CMP_STRIDE):
            tok = src[pl.ds(p, n_seg, stride=CMP_STRIDE), :]
            a = _dot((tok + pos_ref[c, p:p + 1, :]).astype(BF16), w1_ref[c, p])
            b = _dot((tok + pos_ref[c, CMP_STRIDE + p:CMP_STRIDE + p + 1, :]).astype(BF16), w1_ref[c, CMP_STRIDE + p])
            first = a if first is None else first + a
            second = b if second is None else second + b
        pre = first + pltpu.roll(second, n_seg - 1, axis=0)
        hid = pre * jax.nn.sigmoid(pre)
        outs.append(_dot(hid.astype(BF16), w2_ref[c]))
    kc_ref[0] = outs[0].astype(kc_ref.dtype)
    vct_ref[0] = outs[1].T.astype(vct_ref.dtype)


def _compress(pa, b, ksrc_blk, pos, w1, w2):
    t = pa.shape[0] // b
    n_seg = t // CMP_STRIDE
    whole = lambda a: pl.BlockSpec(a.shape, lambda i: (0,) * a.ndim)
    return pl.pallas_call(
        _compress_kernel,
        grid=(b,),
        in_specs=[pl.BlockSpec((t, LANES), lambda i: (i, ksrc_blk)), pl.BlockSpec((t, LANES), lambda i: (i, ksrc_blk + 1)),
                  whole(pos), whole(w1), whole(w2)],
        out_specs=[pl.BlockSpec((1, n_seg, LANES), lambda i: (i, 0, 0)), pl.BlockSpec((1, LANES, n_seg), lambda i: (i, 0, 0))],
        out_shape=[jax.ShapeDtypeStruct((b, n_seg, LANES), BF16), jax.ShapeDtypeStruct((b, LANES, n_seg), BF16)],
        compiler_params=_params(("parallel",)),
        name="compress",
    )(pa, pa, pos, w1, w2)


def _both_groups(w):
    z = jnp.zeros_like(w)
    return jnp.concatenate([jnp.concatenate([w, z], axis=-1), jnp.concatenate([z, w], axis=-1)], axis=-2)


MASK_BIAS = NEG
M_INIT = -1e29


def _stack_heads_t(q, g):
    zeros = jnp.zeros((HEAD_DIM, q.shape[0]), F32)
    parts = []
    for h in range(NSA_HPG):
        hg = g * NSA_HPG + h
        tile_t = (q[:, (hg // 2) * LANES:(hg // 2 + 1) * LANES] * (SCALE * LOG2E)).T
        feat = tile_t[(hg % 2) * HEAD_DIM:(hg % 2 + 1) * HEAD_DIM]
        parts.append(jnp.concatenate([feat, zeros] if g == 0 else [zeros, feat], axis=0))
    return jnp.concatenate(parts, axis=1).astype(BF16)


def _softmax_t(s, bias, hg):
    sb = s + jnp.concatenate([bias] * hg, axis=1)
    m = jnp.maximum(jnp.max(sb, axis=0, keepdims=True), M_INIT)
    e = jnp.exp2(sb - m)
    d = jnp.sum(e, axis=0, keepdims=True)
    return e, 1.0 / jnp.where(d > 0, d, 1.0)


def _nsa_kernel(qp_ref, qr_ref, gate_ref, kc_ref, vct_ref, ks_ref, vst_ref, kw_ref, vwt_ref, et_ref,
                o_ref, sa_scr, sb_scr, oc_scr, psum_scr, *, n_sel, win_keys):
    qb = pl.program_id(1)
    t0 = qb * Q_BLOCK
    nq = Q_BLOCK
    hg = NSA_HPG
    tk = SEL_KEY_TILE
    n_cmp_pad = kc_ref.shape[1]
    t_row = t0 + lax.broadcasted_iota(jnp.int32, (1, nq), 1)
    blk = lax.broadcasted_iota(jnp.int32, (LANES, nq), 0)

    q_plain = qp_ref[0]
    q_rope = qr_ref[0].astype(F32)
    gates_t = jax.nn.sigmoid(gate_ref[0]).T

    cmp_chunk = min(LANES, n_cmp_pad)
    n_cmp_rows = (qb + 1) * (Q_BLOCK // CMP_STRIDE)
    n_cmp_chunks = jnp.minimum((n_cmp_rows + cmp_chunk - 1) // cmp_chunk, n_cmp_pad // cmp_chunk)
    cur = t_row >> 6
    forced = (blk == 0) | (blk == cur) | (blk == cur - 1)
    valid = blk <= cur

    win_start = pl.multiple_of(jnp.maximum(t0 - WIN, 0), Q_BLOCK)
    diff_w = t_row - (win_start + lax.broadcasted_iota(jnp.int32, (win_keys, nq), 0))
    bias_w = jnp.where((diff_w >= 0) & (diff_w < WIN), 0.0, MASK_BIAS)

    first_blk = qb * (Q_BLOCK // SLC_BLOCK)
    n_sweep = (t0 + tk - 1) // tk
    n_key_tiles = ks_ref.shape[1] // tk
    bias_own = jnp.where(lax.broadcasted_iota(jnp.int32, (nq, nq), 0) <= lax.broadcasted_iota(jnp.int32, (nq, nq), 1),
                         0.0, MASK_BIAS)

    groups = range(NSA_KV_GROUPS)
    qt_plains = [_stack_heads_t(q_plain, g) for g in groups]
    qt_ropes = [_stack_heads_t(q_rope, g) for g in groups]

    for c in range(1, n_cmp_pad // cmp_chunk + 1):
        @pl.when(n_cmp_chunks == c)
        def _(c=c):
            rows = c * cmp_chunk
            cmp_end = lax.broadcasted_iota(jnp.int32, (rows, nq), 0) * CMP_STRIDE + (CMP_BLOCK - 1)
            bias_c = jnp.where(cmp_end <= t_row, 0.0, MASK_BIAS)
            s_cs = [_dot(kc_ref[0, 0:rows, :], qt_plains[g]) for g in groups]
            for g in groups:
                e_c, r_c = _softmax_t(s_cs[g], bias_c, hg)
                p_c = e_c * r_c
                oc_scr[g] = _dot(vct_ref[0, :, 0:rows], p_c.astype(BF16))
                p_sum = p_c[:, 0:nq]
                for h in range(1, hg):
                    p_sum = p_sum + p_c[:, h * nq:(h + 1) * nq]
                psum_scr[g, 0:rows, :] = p_sum
                if rows < psum_scr.shape[1]:
                    psum_scr[g, rows:, :] = jnp.zeros((psum_scr.shape[1] - rows, nq), F32)

    per = SLC_BLOCK // CMP_STRIDE
    pools = []
    for g in groups:
        parts = [psum_scr[g, pl.ds(k, LANES, stride=per), :] for k in range(per)]
        tail_prev = jnp.where(blk == 0, 0.0, pltpu.roll(parts[per - 1], 1, axis=0))
        imp = parts[0] + parts[1] + parts[2] + 0.5 * parts[3] + 0.5 * tail_prev
        pools.append(jnp.where(valid & ~forced, imp, -FORCE))

    picks = [forced & valid] * NSA_KV_GROUPS
    for _ in range(n_sel - N_FORCED):
        for g in groups:
            best = jnp.max(pools[g], axis=0, keepdims=True)
            first = jnp.min(jnp.where(pools[g] == best, blk, LANES), axis=0, keepdims=True)
            hit = (blk == first) & (best > -0.5 * FORCE)
            picks[g] = picks[g] | hit
            pools[g] = jnp.where(hit, -FORCE, pools[g])

    s_ws = [_dot(kw_ref[0, pl.ds(win_start, win_keys), :], qt) + jnp.concatenate([bias_w] * hg, axis=1) for qt in qt_ropes]
    e_ws = [jnp.exp2(s_w - jnp.max(s_w, axis=0, keepdims=True)).astype(BF16) for s_w in s_ws]
    acc_ws = [_dot(vwt_ref[g, :, pl.ds(win_start, win_keys)], e_ws[g]) for g in groups]
    o_ws = [acc_w[0:HEAD_DIM] * (1.0 / acc_w[HEAD_DIM:HEAD_DIM + 1]) for acc_w in acc_ws]

    q_augs = []
    for g in groups:
        bias_blk = jnp.where(picks[g] & (blk < first_blk), 0.0, MASK_BIAS).astype(BF16)
        q_augs.append(jnp.concatenate([qt_ropes[g], jnp.concatenate([bias_blk] * hg, axis=1)], axis=0))

    def absorb(s, vt, carry):
        m, acc = carry
        m_new = jnp.maximum(m, jnp.max(s, axis=0, keepdims=True))
        e = jnp.exp2(s - m_new)
        return m_new, jnp.exp2(m - m_new) * acc + _dot(vt, e.astype(BF16))

    def key_tile(kt):
        return pl.multiple_of(jnp.minimum(kt, n_key_tiles - 1) * tk, tk)

    own = pl.ds(pl.multiple_of(t0, nq), nq)
    acc_ss = []
    for g in groups:
        def scores_into(scr, kt, g=g):
            k0 = key_tile(kt)
            scr[...] = _dot(jnp.concatenate([ks_ref[0, pl.ds(k0, tk), :], et_ref[pl.ds(k0, tk), :]], axis=1), q_augs[g])

        def absorb_tile(scr, kt, carry, g=g):
            return absorb(scr[...], vst_ref[g, :, pl.ds(key_tile(kt), tk)], carry)

        def sweep_pair(i, carry):
            scores_into(sb_scr, 2 * i + 1)
            carry = absorb_tile(sa_scr, 2 * i, carry)
            scores_into(sa_scr, 2 * i + 2)
            return absorb_tile(sb_scr, 2 * i + 1, carry)

        init = (jnp.full((1, hg * nq), M_INIT, F32), jnp.zeros((VAL_ROWS, hg * nq), F32))
        scores_into(sa_scr, 0)
        carry = lax.fori_loop(0, n_sweep // 2, sweep_pair, init)
        carry = lax.cond(n_sweep % 2 == 1, lambda c: absorb_tile(sa_scr, n_sweep - 1, c), lambda c: c, carry)
        s_own = _dot(ks_ref[0, own, :], qt_ropes[g]) + jnp.concatenate([bias_own] * hg, axis=1)
        acc_ss.append(absorb(s_own, vst_ref[g, :, own], carry)[1])

    for g in groups:
        feat = slice(g * HEAD_DIM, (g + 1) * HEAD_DIM)
        o_c = oc_scr[g]
        o_s = acc_ss[g][0:HEAD_DIM] * (1.0 / acc_ss[g][HEAD_DIM:HEAD_DIM + 1])
        mixes = []
        for h in range(hg):
            row = g * hg * 3 + h * 3
            cols = slice(h * nq, (h + 1) * nq)
            mixes.append(gates_t[row:row + 1] * o_c[feat, cols] + gates_t[row + 1:row + 2] * o_s[:, cols]
                         + gates_t[row + 2:row + 3] * o_ws[g][:, cols])
        for hp in range(hg // 2):
            c = (g * hg) // 2 + hp
            pair = jnp.concatenate([mixes[2 * hp], mixes[2 * hp + 1]], axis=0)
            o_ref[0, :, c * LANES:(c + 1) * LANES] = pair.T.astype(o_ref.dtype)


def _nsa(pa3, r3, kc, vct, vst, vwt, expand, *, qp_blk, gate_blk, ks_blk, kw_blk):
    b, t, _ = pa3.shape
    n_cmp_pad = kc.shape[1]
    n_sel = min(SLC_TOPN, t // SLC_BLOCK)
    win_keys = min(WIN + Q_BLOCK, t)
    keys = lambda blk: pl.BlockSpec((1, t, LANES), lambda i, j, blk=blk: (i, 0, blk))
    vals_t = pl.BlockSpec((NSA_KV_GROUPS, VAL_ROWS, t), lambda i, j: (0, 0, i))
    return pl.pallas_call(
        functools.partial(_nsa_kernel, n_sel=n_sel, win_keys=win_keys),
        grid=(b, t // Q_BLOCK),
        in_specs=[
            pl.BlockSpec((1, Q_BLOCK, NSA_Q), lambda i, j: (i, j, qp_blk)),
            pl.BlockSpec((1, Q_BLOCK, NSA_Q), lambda i, j: (i, j, 0)),
            pl.BlockSpec((1, Q_BLOCK, LANES), lambda i, j: (i, j, gate_blk)),
            pl.BlockSpec((1, n_cmp_pad, LANES), lambda i, j: (i, 0, 0)),
            pl.BlockSpec((1, LANES, n_cmp_pad), lambda i, j: (i, 0, 0)),
            keys(ks_blk), vals_t, keys(kw_blk), vals_t,
            pl.BlockSpec((t, LANES), lambda i, j: (0, 0)),
        ],
        out_specs=pl.BlockSpec((1, Q_BLOCK, NSA_Q), lambda i, j: (i, j, 0)),
        out_shape=jax.ShapeDtypeStruct((b, t, NSA_Q), BF16),
        scratch_shapes=[
                        pltpu.VMEM((SEL_KEY_TILE, NSA_HPG * Q_BLOCK), F32), pltpu.VMEM((SEL_KEY_TILE, NSA_HPG * Q_BLOCK), F32),
                        pltpu.VMEM((NSA_KV_GROUPS, LANES, NSA_HPG * Q_BLOCK), F32),
                        pltpu.VMEM((NSA_KV_GROUPS, max(n_cmp_pad, LANES * SLC_BLOCK // CMP_STRIDE), Q_BLOCK), F32)],
        compiler_params=_params(("parallel", "arbitrary")),
        name="nsa",
    )(pa3, r3, pa3, kc, vct, r3, vst, r3, vwt, expand)


def _dil_kernel(q_ref, kp_ref, kc_ref, vp_ref, vc_ref, o_ref, lse_ref, *, span):
    ub = pl.program_id(2)
    nq = DIL_BLOCK
    lane = lax.broadcasted_iota(jnp.int32, (nq, LANES), 1)
    row = lax.broadcasted_iota(jnp.int32, (2 * nq, 2 * nq), 0) & (nq - 1)
    col = lax.broadcasted_iota(jnp.int32, (2 * nq, 2 * nq), 1)
    diff = row + nq - col
    band = (diff >= 0) & (diff <= span)
    bias = jnp.where(band, 0.0, MASK_BIAS)
    bias_first = jnp.where(band & ((col >= nq) | (ub > 0)), 0.0, MASK_BIAS)
    for sub in range(q_ref.shape[1] // DIL_W):
        for j in range(q_ref.shape[0] // nq):
            rows = slice(j * nq, (j + 1) * nq)
            before = slice((j - 1) * nq, j * nq)
            for hp in range(DIL_HEADS // 2):
                cols = slice(sub * DIL_W + hp * LANES, sub * DIL_W + (hp + 1) * LANES)
                qt = (q_ref[rows, cols].astype(F32) * (SCALE * LOG2E)).astype(BF16)
                zero = jnp.zeros_like(qt)
                qs = jnp.concatenate([jnp.where(lane < HEAD_DIM, qt, zero), jnp.where(lane >= HEAD_DIM, qt, zero)], axis=0)
                k = jnp.concatenate([kp_ref[:, cols] if j == 0 else kc_ref[before, cols], kc_ref[rows, cols]], axis=0)
                v = jnp.concatenate([vp_ref[:, cols] if j == 0 else vc_ref[before, cols], vc_ref[rows, cols]], axis=0)
                s = _dot_nt(qs, k) + (bias_first if j == 0 else bias)
                m = jnp.max(s, axis=-1, keepdims=True)
                e = jnp.exp2(s - m)
                d = jnp.sum(e, axis=-1, keepdims=True)
                o = _dot(e.astype(BF16), v) * (1.0 / d)
                lse = (m + jnp.log2(d)) * (1.0 / LOG2E)
                o_ref[rows, cols] = jnp.where(lane < HEAD_DIM, o[:nq], o[nq:]).astype(o_ref.dtype)
                lse_ref[rows, cols] = jnp.where(lane < HEAD_DIM, lse[:nq], lse[nq:])


def _dilated(qd, kd, vd, b, dil, span):
    rows = qd.shape[0]
    ln = rows // b
    ub = min(ln, DIL_STEP_ROWS)
    nub = ln // ub
    per = ub // DIL_BLOCK
    nsub = min(dil, DIL_STEP_ROWS // ub)
    cur = pl.BlockSpec((ub, nsub * DIL_W), lambda i, r, u: (i * nub + u, r))
    prev = pl.BlockSpec((DIL_BLOCK, nsub * DIL_W), lambda i, r, u: (jnp.maximum((i * nub + u) * per - 1, 0), r))
    o, lse = pl.pallas_call(
        functools.partial(_dil_kernel, span=span),
        grid=(b, dil // nsub, nub),
        in_specs=[cur, prev, cur, prev, cur],
        out_specs=[cur, cur],
        out_shape=[jax.ShapeDtypeStruct(qd.shape, F32), jax.ShapeDtypeStruct(qd.shape, F32)],
        compiler_params=_params(("parallel", "parallel", "arbitrary")),
        name=f"dilated_{dil}",
    )(qd, kd, kd, vd, vd)
    return o, lse


def _out_kernel(*refs, dils):
    ng = len(dils)
    x_ref, yn_ref = refs[:2]
    o_refs, l_refs = refs[2:2 + ng], refs[2 + ng:2 + 2 * ng]
    g0_ref, g1_ref, wn_ref, wd_ref, wo_ref, o_ref = refs[2 + 2 * ng:8 + 2 * ng]
    scrs = list(refs[8 + 2 * ng:])

    def natural(ref, dil):
        if dil == 1:
            return ref[...]
        scr = scrs.pop(0)
        n = scr.shape[1] // dil
        for res in range(dil):
            for k in range(DIL_W // LANES):
                scr[k, pl.ds(res, n, stride=dil), :] = ref[:, res * DIL_W + k * LANES:res * DIL_W + (k + 1) * LANES]
        return jnp.concatenate([scr[k] for k in range(DIL_W // LANES)], axis=1)

    outs = [natural(r, d) for r, d in zip(o_refs, dils)]
    lses = [natural(r, d) for r, d in zip(l_refs, dils)]
    m = functools.reduce(jnp.maximum, lses)
    ws = [jnp.exp(l - m) for l in lses]
    inv_den = 1.0 / functools.reduce(lambda a, c: a + c, ws)
    y_dil = functools.reduce(lambda a, c: a + c, [w * o for w, o in zip(ws, outs)]) * inv_den
    merged = (jax.nn.sigmoid(g0_ref[...]) * _dot(yn_ref[...], wn_ref[...])
              + jax.nn.sigmoid(g1_ref[...]) * _dot(y_dil.astype(BF16), wd_ref[...]))
    o_ref[...] = x_ref[...] + _dot(merged.astype(BF16), wo_ref[...])


def _out_proj(x2, y_nsa, outs, lses, dils, pa, w_bn, w_bd, w_o, tm=512):
    m, d = x2.shape
    row = lambda w: pl.BlockSpec((tm, w), lambda i: (i, 0))
    dil_row = lambda dil: pl.BlockSpec((tm // dil, dil * DIL_W), lambda i: (i, 0))
    whole = lambda a: pl.BlockSpec(a.shape, lambda i: (0, 0))
    n_staged = 2 * sum(dil != 1 for dil in dils)
    return pl.pallas_call(
        functools.partial(_out_kernel, dils=tuple(dils)),
        grid=(m // tm,),
        in_specs=[row(d), row(NSA_Q)] + [dil_row(dil) for dil in dils] * 2
        + [pl.BlockSpec((tm, d), lambda i: (i, 0)), pl.BlockSpec((tm, d), lambda i: (i, 1))]
        + [whole(w_bn), whole(w_bd), whole(w_o)],
        out_specs=row(d),
        out_shape=jax.ShapeDtypeStruct((m, d), F32),
        scratch_shapes=[pltpu.VMEM((DIL_W // LANES, tm, LANES), F32)] * n_staged,
        compiler_params=_params(("parallel",)),
        name="out_proj",
    )(x2, y_nsa, *outs, *lses, pa, pa, w_bn, w_bd, w_o)


def _mixer(x2, pos_col, inv_lane, b, t, norm_mix, w_in, cmp_pos, cmp_w1, cmp_w2, w_branch_nsa, w_branch_dil, w_out):
    d = x2.shape[1]
    dh = HEAD_DIM
    o_q, o_kv, o_gn = 0, NSA_Q, NSA_Q + 6 * NSA_KV
    o_d = o_gn + 3 * NSA_HEADS
    o_gm = o_d + 3 * DIL_GROUPS * DIL_W
    w_in = w_in.astype(BF16)
    kv = lambda i: w_in[:, o_kv + i * NSA_KV:o_kv + (i + 1) * NSA_KV]
    dd = lambda i: w_in[:, o_d + i * DIL_GROUPS * DIL_W:o_d + (i + 1) * DIL_GROUPS * DIL_W]
    w_q = w_in[:, o_q:o_q + NSA_Q]
    gn_pad = jnp.zeros((d, 2 * LANES - 3 * NSA_HEADS), w_in.dtype)
    w_a = jnp.concatenate([w_in[:, o_gm:o_gm + 2 * d], w_q, kv(0), kv(1), w_in[:, o_gn:o_gn + 3 * NSA_HEADS], gn_pad],
                          axis=1)
    w_b = jnp.concatenate([w_q, kv(2), kv(4), dd(0), dd(1)], axis=1)
    w_c = jnp.concatenate([kv(3), kv(5), dd(2)], axis=1)
    dils = [dil for _, dil in DIL_PAIRS]
    n_r = NSA_Q + 2 * NSA_KV
    seg_b = ((0, n_r, 1),) + tuple((n_r + i * DIL_W, DIL_W, dil) for i, dil in enumerate(dils)) \
        + tuple((n_r + (DIL_GROUPS + i) * DIL_W, DIL_W, dil) for i, dil in enumerate(dils))
    seg_c = ((0, NSA_KV, 0), (NSA_KV, NSA_KV, 0)) + tuple((2 * NSA_KV + i * DIL_W, DIL_W, dil) for i, dil in enumerate(dils))

    pa, = _proj(x2, norm_mix, w_a, F32, name="proj_f32")
    rb, *qk_d = _proj(x2, norm_mix, w_b, BF16, segments=seg_b, pos=pos_col, inv_lane=inv_lane, tm=1024, name="proj_rope")
    vst, vwt, *v_d = _proj(x2, norm_mix, w_c, BF16, segments=seg_c, tm=1024, name="proj_bf16")

    pos = jnp.concatenate([cmp_pos] * NSA_KV_GROUPS, axis=-1)
    w1 = _both_groups(cmp_w1.reshape(2, CMP_BLOCK, dh, -1)).astype(BF16)
    kc, vct = _compress(pa, b, (2 * d + NSA_Q) // LANES, pos, w1, _both_groups(cmp_w2).astype(BF16))

    pa3 = pa.reshape(b, t, -1)
    r3 = rb.reshape(b, t, -1)
    expand = jnp.asarray(np.arange(t)[:, None] // SLC_BLOCK == np.arange(LANES)[None, :], BF16)
    y_nsa = _nsa(pa3, r3, kc, vct, vst, vwt, expand,
                 qp_blk=2 * d // NSA_Q, gate_blk=(2 * d + NSA_Q + 2 * NSA_KV) // LANES,
                 ks_blk=NSA_Q // LANES, kw_blk=NSA_Q // LANES + 1)

    outs, lses = [], []
    for gidx, (window, dil) in enumerate(DIL_PAIRS):
        o, lse = _dilated(qk_d[gidx], qk_d[DIL_GROUPS + gidx], v_d[gidx], b, dil, window // dil)
        outs.append(o)
        lses.append(lse)

    return _out_proj(x2, y_nsa.reshape(b * t, NSA_Q), outs, lses, dils, pa,
                     w_branch_nsa.astype(BF16), w_branch_dil.astype(BF16), w_out.astype(BF16))


def kernel(x, positions, norm_ffn1, ffn1_w_in, ffn1_w_out, norm_mix, w_in, cmp_pos, cmp_w1, cmp_w2,
           w_branch_nsa, w_branch_dil, w_out, norm_ffn2, ffn2_w_in, ffn2_w_out, norm_final):
    b, t, d = x.shape
    depth = norm_ffn1.shape[0]
    x2 = x.reshape(b * t, d)
    pos_col = positions.reshape(b * t, 1).astype(jnp.int32)
    inv = jnp.power(ROPE_THETA, -jnp.arange(0, ROPE_DIM, 2, dtype=F32) / ROPE_DIM)
    inv_head = jnp.concatenate([inv, inv, jnp.zeros((HEAD_DIM - ROPE_DIM,), F32)])
    inv_lane = jnp.tile(inv_head, LANES // HEAD_DIM).reshape(1, LANES)
    for l in range(depth):
        last = l == depth - 1
        x2 = _ffn(x2, norm_ffn1[l], ffn1_w_in[l].astype(BF16), ffn1_w_out[l].astype(BF16))
        x2 = _mixer(x2, pos_col, inv_lane, b, t, norm_mix[l], w_in[l], cmp_pos[l], cmp_w1[l], cmp_w2[l],
                    w_branch_nsa[l], w_branch_dil[l], w_out[l])
        x2 = _ffn(x2, norm_ffn2[l], ffn2_w_in[l].astype(BF16), ffn2_w_out[l].astype(BF16),
                  final_gain=norm_final if last else None)
    if depth == 0:
        raise ValueError("depth must be positive")
    return x2.reshape(b, t, d)
```

```python
import functools

import numpy as np
import jax
import jax.numpy as jnp
from jax import lax
from jax.experimental import pallas as pl
from jax.experimental.pallas import tpu as pltpu

HEAD_DIM = 64
ROPE_DIM = HEAD_DIM // 4
ROPE_THETA = 500000.0
NORM_EPS = 1e-6

NSA_HEADS = 8
NSA_KV_GROUPS = 2
NSA_HPG = NSA_HEADS // NSA_KV_GROUPS
CMP_BLOCK = 32
CMP_STRIDE = 16
SLC_BLOCK = 64
SLC_SHIFT = SLC_BLOCK.bit_length() - 1
SLC_TOPN = 16
WIN = 512
Q_BLOCK = 128
FORCE = 1e9
N_FORCED = 3
NEG = -1e30

DIL_PAIRS = ((128, 1), (512, 4), (2048, 16))
DIL_GROUPS = len(DIL_PAIRS)
DIL_HEADS = 4
DIL_BLOCK = 128
DIL_STEP_ROWS = 16 * DIL_BLOCK

NSA_Q = NSA_HEADS * HEAD_DIM
NSA_KV = NSA_KV_GROUPS * HEAD_DIM
DIL_W = DIL_HEADS * HEAD_DIM

LANES = 128
SEL_KEY_TILE = 1024
VAL_ROWS = HEAD_DIM + 16
VMEM_LIMIT = 56 * 1024 * 1024

F32 = jnp.float32
BF16 = jnp.bfloat16
SCALE = HEAD_DIM ** -0.5
LOG2E = 1.4426950408889634


def _dot(a, b):
    return jnp.dot(a, b, preferred_element_type=F32)


def _dot_nt(a, b):
    return lax.dot_general(a, b, (((1,), (1,)), ((), ())), preferred_element_type=F32)


def _rmsnorm(x, g):
    return x * lax.rsqrt(jnp.mean(x * x, axis=-1, keepdims=True) + NORM_EPS) * g


def _params(sem):
    return pltpu.CompilerParams(dimension_semantics=sem, vmem_limit_bytes=VMEM_LIMIT)


def _ffn_kernel(*refs, final):
    x_ref, g_ref, wi_ref, wo_ref = refs[:4]
    o_ref = refs[-1]
    dff = wo_ref.shape[0]
    x = x_ref[...]
    h = _rmsnorm(x, g_ref[...]).astype(BF16)
    u = _dot(h, wi_ref[:, :dff])
    v = _dot(h, wi_ref[:, dff:])
    a = (u * jax.nn.sigmoid(u)) * v
    y = x + 0.5 * _dot(a.astype(BF16), wo_ref[...])
    if final:
        y = _rmsnorm(y, refs[4][...])
    o_ref[...] = y


def _ffn(x2, gain, w_in, w_out, final_gain=None, tm=1024):
    m, d = x2.shape
    final = final_gain is not None
    whole = lambda a: pl.BlockSpec(a.shape, lambda i: (0, 0))
    in_specs = [pl.BlockSpec((tm, d), lambda i: (i, 0)), pl.BlockSpec((1, d), lambda i: (0, 0)), whole(w_in), whole(w_out)]
    args = [x2, gain.reshape(1, d), w_in, w_out]
    if final:
        in_specs.append(pl.BlockSpec((1, d), lambda i: (0, 0)))
        args.append(final_gain.reshape(1, d))
    return pl.pallas_call(
        functools.partial(_ffn_kernel, final=final),
        grid=(m // tm,),
        in_specs=in_specs,
        out_specs=pl.BlockSpec((tm, d), lambda i: (i, 0)),
        out_shape=jax.ShapeDtypeStruct((m, d), F32),
        compiler_params=_params(("parallel",)),
        name="ffn_final" if final else "ffn",
    )(*args)


def _rope_coeffs(pos_col, inv_lane):
    ang = pos_col.astype(F32) * inv_lane
    c, s = jnp.cos(ang), jnp.sin(ang)
    d = lax.broadcasted_iota(jnp.int32, ang.shape, 1) & (HEAD_DIM - 1)
    half = ROPE_DIM // 2
    coef_c = jnp.where(d < ROPE_DIM, c, 1.0)
    coef_hi = jnp.where(d < half, -s, 0.0)
    coef_lo = jnp.where((d >= half) & (d < ROPE_DIM), s, 0.0)
    return coef_c, coef_hi, coef_lo


def _proj_kernel(*refs, rope, segments):
    n_in = 5 if rope else 3
    x_ref, g_ref, w_ref = refs[:3]
    o_refs = refs[n_in:n_in + len(segments)]
    h = _rmsnorm(x_ref[...], g_ref[...]).astype(BF16)
    y = _dot(h, w_ref[...])
    if len(refs) == n_in + len(segments):
        o_refs[0][...] = y.astype(o_refs[0].dtype)
        return
    y_scr = refs[-1]
    if rope:
        coef_c, coef_hi, coef_lo = _rope_coeffs(refs[3][...], refs[4][...])
        half = ROPE_DIM // 2
        for t in range(y.shape[1] // LANES):
            yt = y[:, t * LANES:(t + 1) * LANES]
            hi = pltpu.roll(yt, LANES - half, axis=1)
            lo = pltpu.roll(yt, half, axis=1)
            y_scr[t] = yt * coef_c + hi * coef_hi + lo * coef_lo
    else:
        for t in range(y.shape[1] // LANES):
            y_scr[t] = y[:, t * LANES:(t + 1) * LANES]
    tm = y.shape[0]
    for o_ref, (c0, w, dil) in zip(o_refs, segments):
        if dil == 0:
            yt = y_scr[c0 // LANES].T
            pad = jnp.where(lax.broadcasted_iota(jnp.int32, (VAL_ROWS - HEAD_DIM, tm), 0) == 0, 1.0, 0.0)
            for g in range(NSA_KV_GROUPS):
                rows = jnp.concatenate([yt[g * HEAD_DIM:(g + 1) * HEAD_DIM], pad], axis=0)
                o_ref[g] = rows.astype(o_ref.dtype)
            continue
        for k in range(w // LANES):
            src = c0 // LANES + k
            for res in range(dil):
                rows = y_scr[src] if dil == 1 else y_scr[src, pl.ds(res, tm // dil, stride=dil), :]
                o_ref[:, res * w + k * LANES:res * w + (k + 1) * LANES] = rows.astype(o_ref.dtype)


def _proj(x2, gain, w, out_dtype, segments=None, pos=None, inv_lane=None, tm=512, name="proj"):
    m, d = x2.shape
    n = w.shape[1]
    rope = pos is not None
    segments = segments or ((0, n, 1),)
    staged = rope or len(segments) > 1 or segments[0][2] != 1
    in_specs = [
        pl.BlockSpec((tm, d), lambda i: (i, 0)),
        pl.BlockSpec((1, d), lambda i: (0, 0)),
        pl.BlockSpec((d, n), lambda i: (0, 0)),
    ]
    args = [x2, gain.reshape(1, d), w]
    if rope:
        in_specs += [pl.BlockSpec((tm, 1), lambda i: (i, 0)), pl.BlockSpec((1, LANES), lambda i: (0, 0))]
        args += [pos, inv_lane]
    outs = pl.pallas_call(
        functools.partial(_proj_kernel, rope=rope, segments=segments),
        grid=(m // tm,),
        in_specs=in_specs,
        out_specs=[pl.BlockSpec((tm // dil, dil * wd), lambda i: (i, 0)) if dil else
                   pl.BlockSpec((NSA_KV_GROUPS, VAL_ROWS, tm), lambda i: (0, 0, i)) for _, wd, dil in segments],
        out_shape=[jax.ShapeDtypeStruct((m // dil, dil * wd) if dil else (NSA_KV_GROUPS, VAL_ROWS, m), out_dtype)
                   for _, wd, dil in segments],
        scratch_shapes=[pltpu.VMEM((n // LANES, tm, LANES), F32)] if staged else [],
        compiler_params=_params(("parallel",)),
        name=name,
    )(*args)
    return outs


def _compress_kernel(ksrc_ref, vsrc_ref, pos_ref, w1_ref, w2_ref, kc_ref, vct_ref):
    n_seg = kc_ref.shape[1]
    outs = []
    for c, src in enumerate((ksrc_ref, vsrc_ref)):
        first = second = None
        for p in range(CMP_STRIDE):
            tok = src[pl.ds(p, n_seg, stride=CMP_STRIDE), :]
            a = _dot((tok + pos_ref[c, p:p + 1, :]).astype(BF16), w1_ref[c, p])
            b = _dot((tok + pos_ref[c, CMP_STRIDE + p:CMP_STRIDE + p + 1, :]).astype(BF16), w1_ref[c, CMP_STRIDE + p])
            first = a if first is None else first + a
            second = b if second is None else second + b
        pre = first + pltpu.roll(second, n_seg - 1, axis=0)
        hid = pre * jax.nn.sigmoid(pre)
        outs.append(_dot(hid.astype(BF16), w2_ref[c]))
    kc_ref[0] = outs[0].astype(kc_ref.dtype)
    vct_ref[0] = outs[1].T.astype(vct_ref.dtype)


def _compress(pa, b, ksrc_blk, pos, w1, w2):
    t = pa.shape[0] // b
    n_seg = t // CMP_STRIDE
    whole = lambda a: pl.BlockSpec(a.shape, lambda i: (0,) * a.ndim)
    return pl.pallas_call(
        _compress_kernel,
        grid=(b,),
        in_specs=[pl.BlockSpec((t, LANES), lambda i: (i, ksrc_blk)), pl.BlockSpec((t, LANES), lambda i: (i, ksrc_blk + 1)),
                  whole(pos), whole(w1), whole(w2)],
        out_specs=[pl.BlockSpec((1, n_seg, LANES), lambda i: (i, 0, 0)), pl.BlockSpec((1, LANES, n_seg), lambda i: (i, 0, 0))],
        out_shape=[jax.ShapeDtypeStruct((b, n_seg, LANES), BF16), jax.ShapeDtypeStruct((b, LANES, n_seg), BF16)],
        compiler_params=_params(("parallel",)),
        name="compress",
    )(pa, pa, pos, w1, w2)


def _both_groups(w):
    z = jnp.zeros_like(w)
    return jnp.concatenate([jnp.concatenate([w, z], axis=-1), jnp.concatenate([z, w], axis=-1)], axis=-2)


MASK_BIAS = NEG
M_INIT = -1e29


def _stack_heads_t(q, g):
    zeros = jnp.zeros((HEAD_DIM, q.shape[0]), F32)
    parts = []
    for h in range(NSA_HPG):
        hg = g * NSA_HPG + h
        tile_t = (q[:, (hg // 2) * LANES:(hg // 2 + 1) * LANES] * (SCALE * LOG2E)).T
        feat = tile_t[(hg % 2) * HEAD_DIM:(hg % 2 + 1) * HEAD_DIM]
        parts.append(jnp.concatenate([feat, zeros] if g == 0 else [zeros, feat], axis=0))
    return jnp.concatenate(parts, axis=1).astype(BF16)


def _softmax_t(s, bias, hg):
    sb = s + jnp.concatenate([bias] * hg, axis=1)
    m = jnp.maximum(jnp.max(sb, axis=0, keepdims=True), M_INIT)
    e = jnp.exp2(sb - m)
    d = jnp.sum(e, axis=0, keepdims=True)
    return e, 1.0 / jnp.where(d > 0, d, 1.0)


def _nsa_kernel(qp_ref, qr_ref, gate_ref, kc_ref, vct_ref, ks_ref, vst_ref, kw_ref, vwt_ref, et_ref,
                o_ref, sa_scr, sb_scr, oc_scr, psum_scr, *, n_sel, win_keys):
    qb = pl.program_id(1)
    t0 = qb * Q_BLOCK
    nq = Q_BLOCK
    hg = NSA_HPG
    tk = SEL_KEY_TILE
    n_cmp_pad = kc_ref.shape[1]
    t_row = t0 + lax.broadcasted_iota(jnp.int32, (1, nq), 1)
    blk = lax.broadcasted_iota(jnp.int32, (LANES, nq), 0)

    q_plain = qp_ref[0]
    q_rope = qr_ref[0].astype(F32)
    gates_t = jax.nn.sigmoid(gate_ref[0]).T

    cmp_chunk = min(LANES, n_cmp_pad)
    n_cmp_rows = (qb + 1) * (Q_BLOCK // CMP_STRIDE)
    n_cmp_chunks = jnp.minimum((n_cmp_rows + cmp_chunk - 1) // cmp_chunk, n_cmp_pad // cmp_chunk)
    cur = t_row >> SLC_SHIFT
    forced = (blk == 0) | (blk == cur) | (blk == cur - 1)
    valid = blk <= cur

    win_start = pl.multiple_of(jnp.maximum(t0 - WIN, 0), Q_BLOCK)
    diff_w = t_row - (win_start + lax.broadcasted_iota(jnp.int32, (win_keys, nq), 0))
    bias_w = jnp.where((diff_w >= 0) & (diff_w < WIN), 0.0, MASK_BIAS)

    first_blk = qb * (Q_BLOCK // SLC_BLOCK)
    n_sweep = (t0 + tk - 1) // tk
    n_key_tiles = ks_ref.shape[1] // tk
    bias_own = jnp.where(lax.broadcasted_iota(jnp.int32, (nq, nq), 0) <= lax.broadcasted_iota(jnp.int32, (nq, nq), 1),
                         0.0, MASK_BIAS)

    groups = range(NSA_KV_GROUPS)
    qt_plains = [_stack_heads_t(q_plain, g) for g in groups]
    qt_ropes = [_stack_heads_t(q_rope, g) for g in groups]

    for c in range(1, n_cmp_pad // cmp_chunk + 1):
        @pl.when(n_cmp_chunks == c)
        def _(c=c):
            rows = c * cmp_chunk
            cmp_end = lax.broadcasted_iota(jnp.int32, (rows, nq), 0) * CMP_STRIDE + (CMP_BLOCK - 1)
            bias_c = jnp.where(cmp_end <= t_row, 0.0, MASK_BIAS)
            s_cs = [_dot(kc_ref[0, 0:rows, :], qt_plains[g]) for g in groups]
            for g in groups:
                e_c, r_c = _softmax_t(s_cs[g], bias_c, hg)
                p_c = e_c * r_c
                oc_scr[g] = _dot(vct_ref[0, :, 0:rows], p_c.astype(BF16))
                p_sum = p_c[:, 0:nq]
                for h in range(1, hg):
                    p_sum = p_sum + p_c[:, h * nq:(h + 1) * nq]
                psum_scr[g, 0:rows, :] = p_sum
                if rows < psum_scr.shape[1]:
                    psum_scr[g, rows:, :] = jnp.zeros((psum_scr.shape[1] - rows, nq), F32)

    per = SLC_BLOCK // CMP_STRIDE
    pools = []
    for g in groups:
        parts = [psum_scr[g, pl.ds(k, LANES, stride=per), :] for k in range(per)]
        tail_prev = jnp.where(blk == 0, 0.0, pltpu.roll(parts[per - 1], 1, axis=0))
        imp = parts[0] + parts[1] + parts[2] + 0.5 * parts[3] + 0.5 * tail_prev
        pools.append(jnp.where(valid & ~forced, imp, -FORCE))

    picks = [forced & valid] * NSA_KV_GROUPS
    for _ in range(n_sel - N_FORCED):
        for g in groups:
            best = jnp.max(pools[g], axis=0, keepdims=True)
            first = jnp.min(jnp.where(pools[g] == best, blk, LANES), axis=0, keepdims=True)
            hit = (blk == first) & (best > -0.5 * FORCE)
            picks[g] = picks[g] | hit
            pools[g] = jnp.where(hit, -FORCE, pools[g])

    s_ws = [_dot(kw_ref[0, pl.ds(win_start, win_keys), :], qt) + jnp.concatenate([bias_w] * hg, axis=1) for qt in qt_ropes]
    e_ws = [jnp.exp2(s_w - jnp.max(s_w, axis=0, keepdims=True)).astype(BF16) for s_w in s_ws]
    acc_ws = [_dot(vwt_ref[g, :, pl.ds(win_start, win_keys)], e_ws[g]) for g in groups]
    o_ws = [acc_w[0:HEAD_DIM] * (1.0 / acc_w[HEAD_DIM:HEAD_DIM + 1]) for acc_w in acc_ws]

    q_augs = []
    for g in groups:
        bias_blk = jnp.where(picks[g] & (blk < first_blk), 0.0, MASK_BIAS).astype(BF16)
        q_augs.append(jnp.concatenate([qt_ropes[g], jnp.concatenate([bias_blk] * hg, axis=1)], axis=0))

    def absorb(s, vt, carry):
        m, acc = carry
        m_new = jnp.maximum(m, jnp.max(s, axis=0, keepdims=True))
        e = jnp.exp2(s - m_new)
        return m_new, jnp.exp2(m - m_new) * acc + _dot(vt, e.astype(BF16))

    def key_tile(kt):
        return pl.multiple_of(jnp.minimum(kt, n_key_tiles - 1) * tk, tk)

    own = pl.ds(pl.multiple_of(t0, nq), nq)
    acc_ss = []
    for g in groups:
        def scores_into(scr, kt, g=g):
            k0 = key_tile(kt)
            scr[...] = _dot(jnp.concatenate([ks_ref[0, pl.ds(k0, tk), :], et_ref[pl.ds(k0, tk), :]], axis=1), q_augs[g])

        def absorb_tile(scr, kt, carry, g=g):
            return absorb(scr[...], vst_ref[g, :, pl.ds(key_tile(kt), tk)], carry)

        def sweep_pair(i, carry):
            scores_into(sb_scr, 2 * i + 1)
            carry = absorb_tile(sa_scr, 2 * i, carry)
            scores_into(sa_scr, 2 * i + 2)
            return absorb_tile(sb_scr, 2 * i + 1, carry)

        init = (jnp.full((1, hg * nq), M_INIT, F32), jnp.zeros((VAL_ROWS, hg * nq), F32))
        scores_into(sa_scr, 0)
        carry = lax.fori_loop(0, n_sweep // 2, sweep_pair, init)
        carry = lax.cond(n_sweep % 2 == 1, lambda c: absorb_tile(sa_scr, n_sweep - 1, c), lambda c: c, carry)
        s_own = _dot(ks_ref[0, own, :], qt_ropes[g]) + jnp.concatenate([bias_own] * hg, axis=1)
        acc_ss.append(absorb(s_own, vst_ref[g, :, own], carry)[1])

    for g in groups:
        feat = slice(g * HEAD_DIM, (g + 1) * HEAD_DIM)
        o_c = oc_scr[g]
        o_s = acc_ss[g][0:HEAD_DIM] * (1.0 / acc_ss[g][HEAD_DIM:HEAD_DIM + 1])
        mixes = []
        for h in range(hg):
            row = g * hg * 3 + h * 3
            cols = slice(h * nq, (h + 1) * nq)
            mixes.append(gates_t[row:row + 1] * o_c[feat, cols] + gates_t[row + 1:row + 2] * o_s[:, cols]
                         + gates_t[row + 2:row + 3] * o_ws[g][:, cols])
        for hp in range(hg // 2):
            c = (g * hg) // 2 + hp
            pair = jnp.concatenate([mixes[2 * hp], mixes[2 * hp + 1]], axis=0)
            o_ref[0, :, c * LANES:(c + 1) * LANES] = pair.T.astype(o_ref.dtype)


def _nsa(pa3, r3, kc, vct, vst, vwt, expand, *, qp_blk, gate_blk, ks_blk, kw_blk):
    b, t, _ = pa3.shape
    n_cmp_pad = kc.shape[1]
    n_sel = min(SLC_TOPN, t // SLC_BLOCK)
    win_keys = min(WIN + Q_BLOCK, t)
    keys = lambda blk: pl.BlockSpec((1, t, LANES), lambda i, j, blk=blk: (i, 0, blk))
    vals_t = pl.BlockSpec((NSA_KV_GROUPS, VAL_ROWS, t), lambda i, j: (0, 0, i))
    return pl.pallas_call(
        functools.partial(_nsa_kernel, n_sel=n_sel, win_keys=win_keys),
        grid=(b, t // Q_BLOCK),
        in_specs=[
            pl.BlockSpec((1, Q_BLOCK, NSA_Q), lambda i, j: (i, j, qp_blk)),
            pl.BlockSpec((1, Q_BLOCK, NSA_Q), lambda i, j: (i, j, 0)),
            pl.BlockSpec((1, Q_BLOCK, LANES), lambda i, j: (i, j, gate_blk)),
            pl.BlockSpec((1, n_cmp_pad, LANES), lambda i, j: (i, 0, 0)),
            pl.BlockSpec((1, LANES, n_cmp_pad), lambda i, j: (i, 0, 0)),
            keys(ks_blk), vals_t, keys(kw_blk), vals_t,
            pl.BlockSpec((t, LANES), lambda i, j: (0, 0)),
        ],
        out_specs=pl.BlockSpec((1, Q_BLOCK, NSA_Q), lambda i, j: (i, j, 0)),
        out_shape=jax.ShapeDtypeStruct((b, t, NSA_Q), BF16),
        scratch_shapes=[
                        pltpu.VMEM((SEL_KEY_TILE, NSA_HPG * Q_BLOCK), F32), pltpu.VMEM((SEL_KEY_TILE, NSA_HPG * Q_BLOCK), F32),
                        pltpu.VMEM((NSA_KV_GROUPS, LANES, NSA_HPG * Q_BLOCK), F32),
                        pltpu.VMEM((NSA_KV_GROUPS, max(n_cmp_pad, LANES * SLC_BLOCK // CMP_STRIDE), Q_BLOCK), F32)],
        compiler_params=_params(("parallel", "arbitrary")),
        name="nsa",
    )(pa3, r3, pa3, kc, vct, r3, vst, r3, vwt, expand)


def _dil_kernel(q_ref, kp_ref, kc_ref, vp_ref, vc_ref, o_ref, lse_ref, *, span):
    ub = pl.program_id(2)
    nq = DIL_BLOCK
    lane = lax.broadcasted_iota(jnp.int32, (nq, LANES), 1)
    row = lax.broadcasted_iota(jnp.int32, (2 * nq, 2 * nq), 0) & (nq - 1)
    col = lax.broadcasted_iota(jnp.int32, (2 * nq, 2 * nq), 1)
    diff = row + nq - col
    band = (diff >= 0) & (diff <= span)
    bias = jnp.where(band, 0.0, MASK_BIAS)
    bias_first = jnp.where(band & ((col >= nq) | (ub > 0)), 0.0, MASK_BIAS)
    for sub in range(q_ref.shape[1] // DIL_W):
        for j in range(q_ref.shape[0] // nq):
            rows = slice(j * nq, (j + 1) * nq)
            before = slice((j - 1) * nq, j * nq)
            for hp in range(DIL_HEADS // 2):
                cols = slice(sub * DIL_W + hp * LANES, sub * DIL_W + (hp + 1) * LANES)
                qt = (q_ref[rows, cols].astype(F32) * (SCALE * LOG2E)).astype(BF16)
                zero = jnp.zeros_like(qt)
                qs = jnp.concatenate([jnp.where(lane < HEAD_DIM, qt, zero), jnp.where(lane >= HEAD_DIM, qt, zero)], axis=0)
                k = jnp.concatenate([kp_ref[:, cols] if j == 0 else kc_ref[before, cols], kc_ref[rows, cols]], axis=0)
                v = jnp.concatenate([vp_ref[:, cols] if j == 0 else vc_ref[before, cols], vc_ref[rows, cols]], axis=0)
                s = _dot_nt(qs, k) + (bias_first if j == 0 else bias)
                m = jnp.max(s, axis=-1, keepdims=True)
                e = jnp.exp2(s - m)
                d = jnp.sum(e, axis=-1, keepdims=True)
                o = _dot(e.astype(BF16), v) * (1.0 / d)
                lse = (m + jnp.log2(d)) * (1.0 / LOG2E)
                o_ref[rows, cols] = jnp.where(lane < HEAD_DIM, o[:nq], o[nq:]).astype(o_ref.dtype)
                lse_ref[rows, cols] = jnp.where(lane < HEAD_DIM, lse[:nq], lse[nq:])


def _dilated(qd, kd, vd, b, dil, span):
    rows = qd.shape[0]
    ln = rows // b
    ub = min(ln, DIL_STEP_ROWS)
    nub = ln // ub
    per = ub // DIL_BLOCK
    nsub = min(dil, DIL_STEP_ROWS // ub)
    cur = pl.BlockSpec((ub, nsub * DIL_W), lambda i, r, u: (i * nub + u, r))
    prev = pl.BlockSpec((DIL_BLOCK, nsub * DIL_W), lambda i, r, u: (jnp.maximum((i * nub + u) * per - 1, 0), r))
    o, lse = pl.pallas_call(
        functools.partial(_dil_kernel, span=span),
        grid=(b, dil // nsub, nub),
        in_specs=[cur, prev, cur, prev, cur],
        out_specs=[cur, cur],
        out_shape=[jax.ShapeDtypeStruct(qd.shape, F32), jax.ShapeDtypeStruct(qd.shape, F32)],
        compiler_params=_params(("parallel", "parallel", "arbitrary")),
        name=f"dilated_{dil}",
    )(qd, kd, kd, vd, vd)
    return o, lse


def _out_kernel(*refs, dils):
    ng = len(dils)
    x_ref, yn_ref = refs[:2]
    o_refs, l_refs = refs[2:2 + ng], refs[2 + ng:2 + 2 * ng]
    g0_ref, g1_ref, wn_ref, wd_ref, wo_ref, o_ref = refs[2 + 2 * ng:8 + 2 * ng]
    scrs = list(refs[8 + 2 * ng:])

    def natural(ref, dil):
        if dil == 1:
            return ref[...]
        scr = scrs.pop(0)
        n = scr.shape[1] // dil
        for res in range(dil):
            for k in range(DIL_W // LANES):
                scr[k, pl.ds(res, n, stride=dil), :] = ref[:, res * DIL_W + k * LANES:res * DIL_W + (k + 1) * LANES]
        return jnp.concatenate([scr[k] for k in range(DIL_W // LANES)], axis=1)

    outs = [natural(r, d) for r, d in zip(o_refs, dils)]
    lses = [natural(r, d) for r, d in zip(l_refs, dils)]
    m = functools.reduce(jnp.maximum, lses)
    ws = [jnp.exp(l - m) for l in lses]
    inv_den = 1.0 / functools.reduce(lambda a, c: a + c, ws)
    y_dil = functools.reduce(lambda a, c: a + c, [w * o for w, o in zip(ws, outs)]) * inv_den
    merged = (jax.nn.sigmoid(g0_ref[...]) * _dot(yn_ref[...], wn_ref[...])
              + jax.nn.sigmoid(g1_ref[...]) * _dot(y_dil.astype(BF16), wd_ref[...]))
    o_ref[...] = x_ref[...] + _dot(merged.astype(BF16), wo_ref[...])


def _out_proj(x2, y_nsa, outs, lses, dils, pa, w_bn, w_bd, w_o, tm=512):
    m, d = x2.shape
    row = lambda w: pl.BlockSpec((tm, w), lambda i: (i, 0))
    dil_row = lambda dil: pl.BlockSpec((tm // dil, dil * DIL_W), lambda i: (i, 0))
    whole = lambda a: pl.BlockSpec(a.shape, lambda i: (0, 0))
    n_staged = 2 * sum(dil != 1 for dil in dils)
    return pl.pallas_call(
        functools.partial(_out_kernel, dils=tuple(dils)),
        grid=(m // tm,),
        in_specs=[row(d), row(NSA_Q)] + [dil_row(dil) for dil in dils] * 2
        + [pl.BlockSpec((tm, d), lambda i: (i, 0)), pl.BlockSpec((tm, d), lambda i: (i, 1))]
        + [whole(w_bn), whole(w_bd), whole(w_o)],
        out_specs=row(d),
        out_shape=jax.ShapeDtypeStruct((m, d), F32),
        scratch_shapes=[pltpu.VMEM((DIL_W // LANES, tm, LANES), F32)] * n_staged,
        compiler_params=_params(("parallel",)),
        name="out_proj",
    )(x2, y_nsa, *outs, *lses, pa, pa, w_bn, w_bd, w_o)


def _mixer(x2, pos_col, inv_lane, b, t, norm_mix, w_in, cmp_pos, cmp_w1, cmp_w2, w_branch_nsa, w_branch_dil, w_out):
    d = x2.shape[1]
    dh = HEAD_DIM
    o_q, o_kv, o_gn = 0, NSA_Q, NSA_Q + 6 * NSA_KV
    o_d = o_gn + 3 * NSA_HEADS
    o_gm = o_d + 3 * DIL_GROUPS * DIL_W
    w_in = w_in.astype(BF16)
    kv = lambda i: w_in[:, o_kv + i * NSA_KV:o_kv + (i + 1) * NSA_KV]
    dd = lambda i: w_in[:, o_d + i * DIL_GROUPS * DIL_W:o_d + (i + 1) * DIL_GROUPS * DIL_W]
    w_q = w_in[:, o_q:o_q + NSA_Q]
    gn_pad = jnp.zeros((d, LANES - 3 * NSA_HEADS), w_in.dtype)
    w_a = jnp.concatenate([w_in[:, o_gm:o_gm + 2 * d], w_q, kv(0), kv(1), w_in[:, o_gn:o_gn + 3 * NSA_HEADS], gn_pad],
                          axis=1)
    w_b = jnp.concatenate([w_q, kv(2), kv(4), dd(0), dd(1)], axis=1)
    w_c = jnp.concatenate([kv(3), kv(5), dd(2)], axis=1)
    dils = [dil for _, dil in DIL_PAIRS]
    n_r = NSA_Q + 2 * NSA_KV
    seg_b = ((0, n_r, 1),) + tuple((n_r + i * DIL_W, DIL_W, dil) for i, dil in enumerate(dils)) \
        + tuple((n_r + (DIL_GROUPS + i) * DIL_W, DIL_W, dil) for i, dil in enumerate(dils))
    seg_c = ((0, NSA_KV, 0), (NSA_KV, NSA_KV, 0)) + tuple((2 * NSA_KV + i * DIL_W, DIL_W, dil) for i, dil in enumerate(dils))

    pa, = _proj(x2, norm_mix, w_a, F32, name="proj_f32")
    rb, *qk_d = _proj(x2, norm_mix, w_b, BF16, segments=seg_b, pos=pos_col, inv_lane=inv_lane, tm=1024, name="proj_rope")
    vst, vwt, *v_d = _proj(x2, norm_mix, w_c, BF16, segments=seg_c, tm=1024, name="proj_bf16")

    pos = jnp.concatenate([cmp_pos] * NSA_KV_GROUPS, axis=-1)
    w1 = _both_groups(cmp_w1.reshape(2, CMP_BLOCK, dh, -1)).astype(BF16)
    kc, vct = _compress(pa, b, (2 * d + NSA_Q) // LANES, pos, w1, _both_groups(cmp_w2).astype(BF16))

    pa3 = pa.reshape(b, t, -1)
    r3 = rb.reshape(b, t, -1)
    expand = jnp.asarray(np.arange(t)[:, None] // SLC_BLOCK == np.arange(LANES)[None, :], BF16)
    y_nsa = _nsa(pa3, r3, kc, vct, vst, vwt, expand,
                 qp_blk=2 * d // NSA_Q, gate_blk=(2 * d + NSA_Q + 2 * NSA_KV) // LANES,
                 ks_blk=NSA_Q // LANES, kw_blk=NSA_Q // LANES + 1)

    outs, lses = [], []
    for gidx, (window, dil) in enumerate(DIL_PAIRS):
        o, lse = _dilated(qk_d[gidx], qk_d[DIL_GROUPS + gidx], v_d[gidx], b, dil, window // dil)
        outs.append(o)
        lses.append(lse)

    return _out_proj(x2, y_nsa.reshape(b * t, NSA_Q), outs, lses, dils, pa,
                     w_branch_nsa.astype(BF16), w_branch_dil.astype(BF16), w_out.astype(BF16))


def kernel(x, positions, norm_ffn1, ffn1_w_in, ffn1_w_out, norm_mix, w_in, cmp_pos, cmp_w1, cmp_w2,
           w_branch_nsa, w_branch_dil, w_out, norm_ffn2, ffn2_w_in, ffn2_w_out, norm_final):
    b, t, d = x.shape
    depth = norm_ffn1.shape[0]
    if depth == 0:
        raise ValueError("depth must be positive")
    x2 = x.reshape(b * t, d)
    pos_col = positions.reshape(b * t, 1).astype(jnp.int32)
    inv = jnp.power(ROPE_THETA, -jnp.arange(0, ROPE_DIM, 2, dtype=F32) / ROPE_DIM)
    inv_head = jnp.concatenate([inv, inv, jnp.zeros((HEAD_DIM - ROPE_DIM,), F32)])
    inv_lane = jnp.tile(inv_head, LANES // HEAD_DIM).reshape(1, LANES)
    for l in range(depth):
        last = l == depth - 1
        x2 = _ffn(x2, norm_ffn1[l], ffn1_w_in[l].astype(BF16), ffn1_w_out[l].astype(BF16))
        x2 = _mixer(x2, pos_col, inv_lane, b, t, norm_mix[l], w_in[l], cmp_pos[l], cmp_w1[l], cmp_w2[l],
                    w_branch_nsa[l], w_branch_dil[l], w_out[l])
        x2 = _ffn(x2, norm_ffn2[l], ffn2_w_in[l].astype(BF16), ffn2_w_out[l].astype(BF16),
                  final_gain=norm_final if last else None)
    return x2.reshape(b, t, d)
```

```python
import functools

import numpy as np
import jax
import jax.numpy as jnp
from jax import lax
from jax.experimental import pallas as pl
from jax.experimental.pallas import tpu as pltpu

HEAD_DIM = 64
ROPE_DIM = HEAD_DIM // 4
ROPE_THETA = 500000.0
NORM_EPS = 1e-6

NSA_HEADS = 8
NSA_KV_GROUPS = 2
NSA_HPG = NSA_HEADS // NSA_KV_GROUPS
CMP_BLOCK = 32
CMP_STRIDE = 16
SLC_BLOCK = 64
SLC_SHIFT = SLC_BLOCK.bit_length() - 1
SLC_TOPN = 16
WIN = 512
Q_BLOCK = 128
FORCE = 1e9
N_FORCED = 3
NEG = -1e30

DIL_PAIRS = ((128, 1), (512, 4), (2048, 16))
DIL_GROUPS = len(DIL_PAIRS)
DIL_HEADS = 4
DIL_BLOCK = 128
DIL_STEP_ROWS = 16 * DIL_BLOCK

NSA_Q = NSA_HEADS * HEAD_DIM
NSA_KV = NSA_KV_GROUPS * HEAD_DIM
DIL_W = DIL_HEADS * HEAD_DIM

LANES = 128
SEL_KEY_TILE = 1024
VAL_ROWS = HEAD_DIM + 16
VMEM_LIMIT = 56 * 1024 * 1024

F32 = jnp.float32
BF16 = jnp.bfloat16
SCALE = HEAD_DIM ** -0.5
LOG2E = 1.4426950408889634


def _dot(a, b):
    return jnp.dot(a, b, preferred_element_type=F32)


def _dot_nt(a, b):
    return lax.dot_general(a, b, (((1,), (1,)), ((), ())), preferred_element_type=F32)


def _rmsnorm(x, g):
    return x * lax.rsqrt(jnp.mean(x * x, axis=-1, keepdims=True) + NORM_EPS) * g


def _params(sem):
    return pltpu.CompilerParams(dimension_semantics=sem, vmem_limit_bytes=VMEM_LIMIT)


def _ffn_kernel(*refs, final):
    x_ref, g_ref, wi_ref, wo_ref = refs[:4]
    o_ref = refs[-1]
    dff = wo_ref.shape[0]
    x = x_ref[...]
    h = _rmsnorm(x, g_ref[...]).astype(BF16)
    u = _dot(h, wi_ref[:, :dff])
    v = _dot(h, wi_ref[:, dff:])
    a = (u * jax.nn.sigmoid(u)) * v
    y = x + 0.5 * _dot(a.astype(BF16), wo_ref[...])
    if final:
        y = _rmsnorm(y, refs[4][...])
    o_ref[...] = y


def _ffn(x2, gain, w_in, w_out, final_gain=None, tm=1024):
    m, d = x2.shape
    final = final_gain is not None
    whole = lambda a: pl.BlockSpec(a.shape, lambda i: (0, 0))
    in_specs = [pl.BlockSpec((tm, d), lambda i: (i, 0)), pl.BlockSpec((1, d), lambda i: (0, 0)), whole(w_in), whole(w_out)]
    args = [x2, gain.reshape(1, d), w_in, w_out]
    if final:
        in_specs.append(pl.BlockSpec((1, d), lambda i: (0, 0)))
        args.append(final_gain.reshape(1, d))
    return pl.pallas_call(
        functools.partial(_ffn_kernel, final=final),
        grid=(m // tm,),
        in_specs=in_specs,
        out_specs=pl.BlockSpec((tm, d), lambda i: (i, 0)),
        out_shape=jax.ShapeDtypeStruct((m, d), F32),
        compiler_params=_params(("parallel",)),
        name="ffn_final" if final else "ffn",
    )(*args)


def _rope_coeffs(pos_col, inv_lane):
    ang = pos_col.astype(F32) * inv_lane
    c, s = jnp.cos(ang), jnp.sin(ang)
    d = lax.broadcasted_iota(jnp.int32, ang.shape, 1) & (HEAD_DIM - 1)
    half = ROPE_DIM // 2
    coef_c = jnp.where(d < ROPE_DIM, c, 1.0)
    coef_hi = jnp.where(d < half, -s, 0.0)
    coef_lo = jnp.where((d >= half) & (d < ROPE_DIM), s, 0.0)
    return coef_c, coef_hi, coef_lo


def _proj_kernel(*refs, rope, segments):
    n_in = 5 if rope else 3
    x_ref, g_ref, w_ref = refs[:3]
    o_refs = refs[n_in:n_in + len(segments)]
    h = _rmsnorm(x_ref[...], g_ref[...]).astype(BF16)
    y = _dot(h, w_ref[...])
    if len(refs) == n_in + len(segments):
        o_refs[0][...] = y.astype(o_refs[0].dtype)
        return
    y_scr = refs[-1]
    if rope:
        coef_c, coef_hi, coef_lo = _rope_coeffs(refs[3][...], refs[4][...])
        half = ROPE_DIM // 2
        for t in range(y.shape[1] // LANES):
            yt = y[:, t * LANES:(t + 1) * LANES]
            hi = pltpu.roll(yt, LANES - half, axis=1)
            lo = pltpu.roll(yt, half, axis=1)
            y_scr[t] = yt * coef_c + hi * coef_hi + lo * coef_lo
    else:
        for t in range(y.shape[1] // LANES):
            y_scr[t] = y[:, t * LANES:(t + 1) * LANES]
    tm = y.shape[0]
    for o_ref, (c0, w, dil) in zip(o_refs, segments):
        if dil == 0:
            yt = y_scr[c0 // LANES].T
            pad = jnp.where(lax.broadcasted_iota(jnp.int32, (VAL_ROWS - HEAD_DIM, tm), 0) == 0, 1.0, 0.0)
            for g in range(NSA_KV_GROUPS):
                rows = jnp.concatenate([yt[g * HEAD_DIM:(g + 1) * HEAD_DIM], pad], axis=0)
                o_ref[g] = rows.astype(o_ref.dtype)
            continue
        for k in range(w // LANES):
            src = c0 // LANES + k
            for res in range(dil):
                rows = y_scr[src] if dil == 1 else y_scr[src, pl.ds(res, tm // dil, stride=dil), :]
                o_ref[:, res * w + k * LANES:res * w + (k + 1) * LANES] = rows.astype(o_ref.dtype)


def _proj(x2, gain, w, out_dtype, segments=None, pos=None, inv_lane=None, tm=512, name="proj"):
    m, d = x2.shape
    n = w.shape[1]
    rope = pos is not None
    segments = segments or ((0, n, 1),)
    staged = rope or len(segments) > 1 or segments[0][2] != 1
    in_specs = [
        pl.BlockSpec((tm, d), lambda i: (i, 0)),
        pl.BlockSpec((1, d), lambda i: (0, 0)),
        pl.BlockSpec((d, n), lambda i: (0, 0)),
    ]
    args = [x2, gain.reshape(1, d), w]
    if rope:
        in_specs += [pl.BlockSpec((tm, 1), lambda i: (i, 0)), pl.BlockSpec((1, LANES), lambda i: (0, 0))]
        args += [pos, inv_lane]
    outs = pl.pallas_call(
        functools.partial(_proj_kernel, rope=rope, segments=segments),
        grid=(m // tm,),
        in_specs=in_specs,
        out_specs=[pl.BlockSpec((tm // dil, dil * wd), lambda i: (i, 0)) if dil else
                   pl.BlockSpec((NSA_KV_GROUPS, VAL_ROWS, tm), lambda i: (0, 0, i)) for _, wd, dil in segments],
        out_shape=[jax.ShapeDtypeStruct((m // dil, dil * wd) if dil else (NSA_KV_GROUPS, VAL_ROWS, m), out_dtype)
                   for _, wd, dil in segments],
        scratch_shapes=[pltpu.VMEM((n // LANES, tm, LANES), F32)] if staged else [],
        compiler_params=_params(("parallel",)),
        name=name,
    )(*args)
    return outs


def _compress_kernel(ksrc_ref, vsrc_ref, pos_ref, w1_ref, w2_ref, kc_ref, vct_ref):
    n_seg = kc_ref.shape[1]
    outs = []
    for c, src in enumerate((ksrc_ref, vsrc_ref)):
        first = second = None
        for p in range(CMP_STRIDE):
            tok = src[pl.ds(p, n_seg, stride=CMP_STRIDE), :]
            a = _dot((tok + pos_ref[c, p:p + 1, :]).astype(BF16), w1_ref[c, p])
            b = _dot((tok + pos_ref[c, CMP_STRIDE + p:CMP_STRIDE + p + 1, :]).astype(BF16), w1_ref[c, CMP_STRIDE + p])
            first = a if first is None else first + a
            second = b if second is None else second + b
        pre = first + pltpu.roll(second, n_seg - 1, axis=0)
        hid = pre * jax.nn.sigmoid(pre)
        outs.append(_dot(hid.astype(BF16), w2_ref[c]))
    kc_ref[0] = outs[0].astype(kc_ref.dtype)
    vct_ref[0] = outs[1].T.astype(vct_ref.dtype)


def _compress(pa, b, ksrc_blk, pos, w1, w2):
    t = pa.shape[0] // b
    n_seg = t // CMP_STRIDE
    whole = lambda a: pl.BlockSpec(a.shape, lambda i: (0,) * a.ndim)
    return pl.pallas_call(
        _compress_kernel,
        grid=(b,),
        in_specs=[pl.BlockSpec((t, LANES), lambda i: (i, ksrc_blk)), pl.BlockSpec((t, LANES), lambda i: (i, ksrc_blk + 1)),
                  whole(pos), whole(w1), whole(w2)],
        out_specs=[pl.BlockSpec((1, n_seg, LANES), lambda i: (i, 0, 0)), pl.BlockSpec((1, LANES, n_seg), lambda i: (i, 0, 0))],
        out_shape=[jax.ShapeDtypeStruct((b, n_seg, LANES), BF16), jax.ShapeDtypeStruct((b, LANES, n_seg), BF16)],
        compiler_params=_params(("parallel",)),
        name="compress",
    )(pa, pa, pos, w1, w2)


def _both_groups(w):
    z = jnp.zeros_like(w)
    return jnp.concatenate([jnp.concatenate([w, z], axis=-1), jnp.concatenate([z, w], axis=-1)], axis=-2)


MASK_BIAS = NEG
M_INIT = -1e29


def _stack_heads_t(q, g):
    zeros = jnp.zeros((HEAD_DIM, q.shape[0]), F32)
    parts = []
    for h in range(NSA_HPG):
        hg = g * NSA_HPG + h
        tile_t = (q[:, (hg // 2) * LANES:(hg // 2 + 1) * LANES] * (SCALE * LOG2E)).T
        feat = tile_t[(hg % 2) * HEAD_DIM:(hg % 2 + 1) * HEAD_DIM]
        parts.append(jnp.concatenate([feat, zeros] if g == 0 else [zeros, feat], axis=0))
    return jnp.concatenate(parts, axis=1).astype(BF16)


def _softmax_t(s, bias, hg):
    sb = s + jnp.concatenate([bias] * hg, axis=1)
    m = jnp.maximum(jnp.max(sb, axis=0, keepdims=True), M_INIT)
    e = jnp.exp2(sb - m)
    d = jnp.sum(e, axis=0, keepdims=True)
    return e, 1.0 / jnp.where(d > 0, d, 1.0)


def _nsa_kernel(qp_ref, qr_ref, gate_ref, kc_ref, vct_ref, ks_ref, vst_ref, kw_ref, vwt_ref, et_ref,
                o_ref, sa_scr, sb_scr, oc_scr, psum_scr, *, n_sel, win_keys):
    qb = pl.program_id(1)
    t0 = qb * Q_BLOCK
    nq = Q_BLOCK
    hg = NSA_HPG
    tk = SEL_KEY_TILE
    n_cmp_pad = kc_ref.shape[1]
    t_row = t0 + lax.broadcasted_iota(jnp.int32, (1, nq), 1)
    blk = lax.broadcasted_iota(jnp.int32, (LANES, nq), 0)

    q_plain = qp_ref[0]
    q_rope = qr_ref[0].astype(F32)
    gates_t = jax.nn.sigmoid(gate_ref[0]).T

    cmp_chunk = min(LANES, n_cmp_pad)
    n_cmp_rows = (qb + 1) * (Q_BLOCK // CMP_STRIDE)
    n_cmp_chunks = jnp.minimum((n_cmp_rows + cmp_chunk - 1) // cmp_chunk, n_cmp_pad // cmp_chunk)
    cur = t_row >> SLC_SHIFT
    forced = (blk == 0) | (blk == cur) | (blk == cur - 1)
    valid = blk <= cur

    win_start = pl.multiple_of(jnp.maximum(t0 - WIN, 0), Q_BLOCK)
    diff_w = t_row - (win_start + lax.broadcasted_iota(jnp.int32, (win_keys, nq), 0))
    bias_w = jnp.where((diff_w >= 0) & (diff_w < WIN), 0.0, MASK_BIAS)

    first_blk = qb * (Q_BLOCK // SLC_BLOCK)
    n_sweep = (t0 + tk - 1) // tk
    n_key_tiles = ks_ref.shape[1] // tk
    bias_own = jnp.where(lax.broadcasted_iota(jnp.int32, (nq, nq), 0) <= lax.broadcasted_iota(jnp.int32, (nq, nq), 1),
                         0.0, MASK_BIAS)

    groups = range(NSA_KV_GROUPS)
    qt_plains = [_stack_heads_t(q_plain, g) for g in groups]
    qt_ropes = [_stack_heads_t(q_rope, g) for g in groups]

    for c in range(1, n_cmp_pad // cmp_chunk + 1):
        @pl.when(n_cmp_chunks == c)
        def _(c=c):
            rows = c * cmp_chunk
            cmp_end = lax.broadcasted_iota(jnp.int32, (rows, nq), 0) * CMP_STRIDE + (CMP_BLOCK - 1)
            bias_c = jnp.where(cmp_end <= t_row, 0.0, MASK_BIAS)
            s_cs = [_dot(kc_ref[0, 0:rows, :], qt_plains[g]) for g in groups]
            for g in groups:
                e_c, r_c = _softmax_t(s_cs[g], bias_c, hg)
                p_c = e_c * r_c
                oc_scr[g] = _dot(vct_ref[0, :, 0:rows], p_c.astype(BF16))
                p_sum = p_c[:, 0:nq]
                for h in range(1, hg):
                    p_sum = p_sum + p_c[:, h * nq:(h + 1) * nq]
                psum_scr[g, 0:rows, :] = p_sum
                if rows < psum_scr.shape[1]:
                    psum_scr[g, rows:, :] = jnp.zeros((psum_scr.shape[1] - rows, nq), F32)

    per = SLC_BLOCK // CMP_STRIDE
    pools = []
    for g in groups:
        parts = [psum_scr[g, pl.ds(k, LANES, stride=per), :] for k in range(per)]
        tail_prev = jnp.where(blk == 0, 0.0, pltpu.roll(parts[per - 1], 1, axis=0))
        imp = parts[0] + parts[1] + parts[2] + 0.5 * parts[3] + 0.5 * tail_prev
        pools.append(jnp.where(valid & ~forced, imp, -FORCE))

    picks = [forced & valid] * NSA_KV_GROUPS
    for _ in range(n_sel - N_FORCED):
        for g in groups:
            best = jnp.max(pools[g], axis=0, keepdims=True)
            first = jnp.min(jnp.where(pools[g] == best, blk, LANES), axis=0, keepdims=True)
            hit = (blk == first) & (best > -0.5 * FORCE)
            picks[g] = picks[g] | hit
            pools[g] = jnp.where(hit, -FORCE, pools[g])

    s_ws = [_dot(kw_ref[0, pl.ds(win_start, win_keys), :], qt) + jnp.concatenate([bias_w] * hg, axis=1) for qt in qt_ropes]
    e_ws = [jnp.exp2(s_w - jnp.max(s_w, axis=0, keepdims=True)).astype(BF16) for s_w in s_ws]
    acc_ws = [_dot(vwt_ref[g, :, pl.ds(win_start, win_keys)], e_ws[g]) for g in groups]
    o_ws = [acc_w[0:HEAD_DIM] * (1.0 / acc_w[HEAD_DIM:HEAD_DIM + 1]) for acc_w in acc_ws]

    q_augs = []
    for g in groups:
        bias_blk = jnp.where(picks[g] & (blk < first_blk), 0.0, MASK_BIAS).astype(BF16)
        q_augs.append(jnp.concatenate([qt_ropes[g], jnp.concatenate([bias_blk] * hg, axis=1)], axis=0))

    def absorb(s, vt, carry):
        m, acc = carry
        m_new = jnp.maximum(m, jnp.max(s, axis=0, keepdims=True))
        e = jnp.exp2(s - m_new)
        return m_new, jnp.exp2(m - m_new) * acc + _dot(vt, e.astype(BF16))

    def key_tile(kt):
        return pl.multiple_of(jnp.minimum(kt, n_key_tiles - 1) * tk, tk)

    own = pl.ds(pl.multiple_of(t0, nq), nq)
    acc_ss = []
    for g in groups:
        def scores_into(scr, kt, g=g):
            k0 = key_tile(kt)
            scr[...] = _dot(jnp.concatenate([ks_ref[0, pl.ds(k0, tk), :], et_ref[pl.ds(k0, tk), :]], axis=1), q_augs[g])

        def absorb_tile(scr, kt, carry, g=g):
            return absorb(scr[...], vst_ref[g, :, pl.ds(key_tile(kt), tk)], carry)

        def sweep_pair(i, carry):
            scores_into(sb_scr, 2 * i + 1)
            carry = absorb_tile(sa_scr, 2 * i, carry)
            scores_into(sa_scr, 2 * i + 2)
            return absorb_tile(sb_scr, 2 * i + 1, carry)

        init = (jnp.full((1, hg * nq), M_INIT, F32), jnp.zeros((VAL_ROWS, hg * nq), F32))
        scores_into(sa_scr, 0)
        carry = lax.fori_loop(0, n_sweep // 2, sweep_pair, init)
        carry = lax.cond(n_sweep % 2 == 1, lambda c: absorb_tile(sa_scr, n_sweep - 1, c), lambda c: c, carry)
        s_own = _dot(ks_ref[0, own, :], qt_ropes[g]) + jnp.concatenate([bias_own] * hg, axis=1)
        acc_ss.append(absorb(s_own, vst_ref[g, :, own], carry)[1])

    for g in groups:
        feat = slice(g * HEAD_DIM, (g + 1) * HEAD_DIM)
        o_c = oc_scr[g]
        o_s = acc_ss[g][0:HEAD_DIM] * (1.0 / acc_ss[g][HEAD_DIM:HEAD_DIM + 1])
        mixes = []
        for h in range(hg):
            row = g * hg * 3 + h * 3
            cols = slice(h * nq, (h + 1) * nq)
            mixes.append(gates_t[row:row + 1] * o_c[feat, cols] + gates_t[row + 1:row + 2] * o_s[:, cols]
                         + gates_t[row + 2:row + 3] * o_ws[g][:, cols])
        for hp in range(hg // 2):
            c = (g * hg) // 2 + hp
            pair = jnp.concatenate([mixes[2 * hp], mixes[2 * hp + 1]], axis=0)
            o_ref[0, :, c * LANES:(c + 1) * LANES] = pair.T.astype(o_ref.dtype)


def _nsa(pa3, r3, kc, vct, vst, vwt, expand, *, qp_blk, gate_blk, ks_blk, kw_blk):
    b, t, _ = pa3.shape
    n_cmp_pad = kc.shape[1]
    n_sel = min(SLC_TOPN, t // SLC_BLOCK)
    win_keys = min(WIN + Q_BLOCK, t)
    keys = lambda blk: pl.BlockSpec((1, t, LANES), lambda i, j, blk=blk: (i, 0, blk))
    vals_t = pl.BlockSpec((NSA_KV_GROUPS, VAL_ROWS, t), lambda i, j: (0, 0, i))
    return pl.pallas_call(
        functools.partial(_nsa_kernel, n_sel=n_sel, win_keys=win_keys),
        grid=(b, t // Q_BLOCK),
        in_specs=[
            pl.BlockSpec((1, Q_BLOCK, NSA_Q), lambda i, j: (i, j, qp_blk)),
            pl.BlockSpec((1, Q_BLOCK, NSA_Q), lambda i, j: (i, j, 0)),
            pl.BlockSpec((1, Q_BLOCK, LANES), lambda i, j: (i, j, gate_blk)),
            pl.BlockSpec((1, n_cmp_pad, LANES), lambda i, j: (i, 0, 0)),
            pl.BlockSpec((1, LANES, n_cmp_pad), lambda i, j: (i, 0, 0)),
            keys(ks_blk), vals_t, keys(kw_blk), vals_t,
            pl.BlockSpec((t, LANES), lambda i, j: (0, 0)),
        ],
        out_specs=pl.BlockSpec((1, Q_BLOCK, NSA_Q), lambda i, j: (i, j, 0)),
        out_shape=jax.ShapeDtypeStruct((b, t, NSA_Q), BF16),
        scratch_shapes=[
                        pltpu.VMEM((SEL_KEY_TILE, NSA_HPG * Q_BLOCK), F32), pltpu.VMEM((SEL_KEY_TILE, NSA_HPG * Q_BLOCK), F32),
                        pltpu.VMEM((NSA_KV_GROUPS, LANES, NSA_HPG * Q_BLOCK), F32),
                        pltpu.VMEM((NSA_KV_GROUPS, max(n_cmp_pad, LANES * SLC_BLOCK // CMP_STRIDE), Q_BLOCK), F32)],
        compiler_params=_params(("parallel", "arbitrary")),
        name="nsa",
    )(pa3, r3, pa3, kc, vct, r3, vst, r3, vwt, expand)


def _dil_kernel(q_ref, kp_ref, kc_ref, vp_ref, vc_ref, o_ref, lse_ref, *, span):
    ub = pl.program_id(2)
    nq = DIL_BLOCK
    lane = lax.broadcasted_iota(jnp.int32, (nq, LANES), 1)
    row = lax.broadcasted_iota(jnp.int32, (2 * nq, 2 * nq), 0) & (nq - 1)
    col = lax.broadcasted_iota(jnp.int32, (2 * nq, 2 * nq), 1)
    diff = row + nq - col
    band = (diff >= 0) & (diff <= span)
    bias = jnp.where(band, 0.0, MASK_BIAS)
    bias_first = jnp.where(band & ((col >= nq) | (ub > 0)), 0.0, MASK_BIAS)
    for sub in range(q_ref.shape[1] // DIL_W):
        for j in range(q_ref.shape[0] // nq):
            rows = slice(j * nq, (j + 1) * nq)
            before = slice((j - 1) * nq, j * nq)
            for hp in range(DIL_HEADS // 2):
                cols = slice(sub * DIL_W + hp * LANES, sub * DIL_W + (hp + 1) * LANES)
                qt = (q_ref[rows, cols].astype(F32) * (SCALE * LOG2E)).astype(BF16)
                zero = jnp.zeros_like(qt)
                qs = jnp.concatenate([jnp.where(lane < HEAD_DIM, qt, zero), jnp.where(lane >= HEAD_DIM, qt, zero)], axis=0)
                k = jnp.concatenate([kp_ref[:, cols] if j == 0 else kc_ref[before, cols], kc_ref[rows, cols]], axis=0)
                v = jnp.concatenate([vp_ref[:, cols] if j == 0 else vc_ref[before, cols], vc_ref[rows, cols]], axis=0)
                s = _dot_nt(qs, k) + (bias_first if j == 0 else bias)
                m = jnp.max(s, axis=-1, keepdims=True)
                e = jnp.exp2(s - m)
                d = jnp.sum(e, axis=-1, keepdims=True)
                o = _dot(e.astype(BF16), v) * (1.0 / d)
                lse = (m + jnp.log2(d)) * (1.0 / LOG2E)
                o_ref[rows, cols] = jnp.where(lane < HEAD_DIM, o[:nq], o[nq:]).astype(o_ref.dtype)
                lse_ref[rows, cols] = jnp.where(lane < HEAD_DIM, lse[:nq], lse[nq:])


def _dilated(qd, kd, vd, b, dil, span):
    rows = qd.shape[0]
    ln = rows // b
    ub = min(ln, DIL_STEP_ROWS)
    nub = ln // ub
    per = ub // DIL_BLOCK
    nsub = min(dil, DIL_STEP_ROWS // ub)
    cur = pl.BlockSpec((ub, nsub * DIL_W), lambda i, r, u: (i * nub + u, r))
    prev = pl.BlockSpec((DIL_BLOCK, nsub * DIL_W), lambda i, r, u: (jnp.maximum((i * nub + u) * per - 1, 0), r))
    o, lse = pl.pallas_call(
        functools.partial(_dil_kernel, span=span),
        grid=(b, dil // nsub, nub),
        in_specs=[cur, prev, cur, prev, cur],
        out_specs=[cur, cur],
        out_shape=[jax.ShapeDtypeStruct(qd.shape, F32), jax.ShapeDtypeStruct(qd.shape, F32)],
        compiler_params=_params(("parallel", "parallel", "arbitrary")),
        name=f"dilated_{dil}",
    )(qd, kd, kd, vd, vd)
    return o, lse


def _out_kernel(*refs, dils):
    ng = len(dils)
    x_ref, yn_ref = refs[:2]
    o_refs, l_refs = refs[2:2 + ng], refs[2 + ng:2 + 2 * ng]
    g0_ref, g1_ref, wn_ref, wd_ref, wo_ref, o_ref = refs[2 + 2 * ng:8 + 2 * ng]
    scrs = list(refs[8 + 2 * ng:])

    def natural(ref, dil):
        if dil == 1:
            return ref[...]
        scr = scrs.pop(0)
        n = scr.shape[1] // dil
        for res in range(dil):
            for k in range(DIL_W // LANES):
                scr[k, pl.ds(res, n, stride=dil), :] = ref[:, res * DIL_W + k * LANES:res * DIL_W + (k + 1) * LANES]
        return jnp.concatenate([scr[k] for k in range(DIL_W // LANES)], axis=1)

    outs = [natural(r, d) for r, d in zip(o_refs, dils)]
    lses = [natural(r, d) for r, d in zip(l_refs, dils)]
    m = functools.reduce(jnp.maximum, lses)
    ws = [jnp.exp(l - m) for l in lses]
    inv_den = 1.0 / functools.reduce(lambda a, c: a + c, ws)
    y_dil = functools.reduce(lambda a, c: a + c, [w * o for w, o in zip(ws, outs)]) * inv_den
    merged = (jax.nn.sigmoid(g0_ref[...]) * _dot(yn_ref[...], wn_ref[...])
              + jax.nn.sigmoid(g1_ref[...]) * _dot(y_dil.astype(BF16), wd_ref[...]))
    o_ref[...] = x_ref[...] + _dot(merged.astype(BF16), wo_ref[...])


def _out_proj(x2, y_nsa, outs, lses, dils, pa, w_bn, w_bd, w_o, tm=512):
    m, d = x2.shape
    row = lambda w: pl.BlockSpec((tm, w), lambda i: (i, 0))
    dil_row = lambda dil: pl.BlockSpec((tm // dil, dil * DIL_W), lambda i: (i, 0))
    whole = lambda a: pl.BlockSpec(a.shape, lambda i: (0, 0))
    n_staged = 2 * sum(dil != 1 for dil in dils)
    return pl.pallas_call(
        functools.partial(_out_kernel, dils=tuple(dils)),
        grid=(m // tm,),
        in_specs=[row(d), row(NSA_Q)] + [dil_row(dil) for dil in dils] * 2
        + [pl.BlockSpec((tm, d), lambda i: (i, 0)), pl.BlockSpec((tm, d), lambda i: (i, 1))]
        + [whole(w_bn), whole(w_bd), whole(w_o)],
        out_specs=row(d),
        out_shape=jax.ShapeDtypeStruct((m, d), F32),
        scratch_shapes=[pltpu.VMEM((DIL_W // LANES, tm, LANES), F32)] * n_staged,
        compiler_params=_params(("parallel",)),
        name="out_proj",
    )(x2, y_nsa, *outs, *lses, pa, pa, w_bn, w_bd, w_o)


def _mixer(x2, pos_col, inv_lane, b, t, norm_mix, w_in, cmp_pos, cmp_w1, cmp_w2, w_branch_nsa, w_branch_dil, w_out):
    d = x2.shape[1]
    dh = HEAD_DIM
    o_q, o_kv, o_gn = 0, NSA_Q, NSA_Q + 6 * NSA_KV
    o_d = o_gn + 3 * NSA_HEADS
    o_gm = o_d + 3 * DIL_GROUPS * DIL_W
    w_in = w_in.astype(BF16)
    kv = lambda i: w_in[:, o_kv + i * NSA_KV:o_kv + (i + 1) * NSA_KV]
    dd = lambda i: w_in[:, o_d + i * DIL_GROUPS * DIL_W:o_d + (i + 1) * DIL_GROUPS * DIL_W]
    w_q = w_in[:, o_q:o_q + NSA_Q]
    gn_pad = jnp.zeros((d, 2 * LANES - 3 * NSA_HEADS), w_in.dtype)
    w_a = jnp.concatenate([w_in[:, o_gm:o_gm + 2 * d], w_q, kv(0), kv(1), w_in[:, o_gn:o_gn + 3 * NSA_HEADS], gn_pad],
                          axis=1)
    w_b = jnp.concatenate([w_q, kv(2), kv(4), dd(0), dd(1)], axis=1)
    w_c = jnp.concatenate([kv(3), kv(5), dd(2)], axis=1)
    dils = [dil for _, dil in DIL_PAIRS]
    n_r = NSA_Q + 2 * NSA_KV
    seg_b = ((0, n_r, 1),) + tuple((n_r + i * DIL_W, DIL_W, dil) for i, dil in enumerate(dils)) \
        + tuple((n_r + (DIL_GROUPS + i) * DIL_W, DIL_W, dil) for i, dil in enumerate(dils))
    seg_c = ((0, NSA_KV, 0), (NSA_KV, NSA_KV, 0)) + tuple((2 * NSA_KV + i * DIL_W, DIL_W, dil) for i, dil in enumerate(dils))

    pa, = _proj(x2, norm_mix, w_a, F32, name="proj_f32")
    rb, *qk_d = _proj(x2, norm_mix, w_b, BF16, segments=seg_b, pos=pos_col, inv_lane=inv_lane, tm=1024, name="proj_rope")
    vst, vwt, *v_d = _proj(x2, norm_mix, w_c, BF16, segments=seg_c, tm=1024, name="proj_bf16")

    pos = jnp.concatenate([cmp_pos] * NSA_KV_GROUPS, axis=-1)
    w1 = _both_groups(cmp_w1.reshape(2, CMP_BLOCK, dh, -1)).astype(BF16)
    kc, vct = _compress(pa, b, (2 * d + NSA_Q) // LANES, pos, w1, _both_groups(cmp_w2).astype(BF16))

    pa3 = pa.reshape(b, t, -1)
    r3 = rb.reshape(b, t, -1)
    expand = jnp.asarray(np.arange(t)[:, None] // SLC_BLOCK == np.arange(LANES)[None, :], BF16)
    y_nsa = _nsa(pa3, r3, kc, vct, vst, vwt, expand,
                 qp_blk=2 * d // NSA_Q, gate_blk=(2 * d + NSA_Q + 2 * NSA_KV) // LANES,
                 ks_blk=NSA_Q // LANES, kw_blk=NSA_Q // LANES + 1)

    outs, lses = [], []
    for gidx, (window, dil) in enumerate(DIL_PAIRS):
        o, lse = _dilated(qk_d[gidx], qk_d[DIL_GROUPS + gidx], v_d[gidx], b, dil, window // dil)
        outs.append(o)
        lses.append(lse)

    return _out_proj(x2, y_nsa.reshape(b * t, NSA_Q), outs, lses, dils, pa,
                     w_branch_nsa.astype(BF16), w_branch_dil.astype(BF16), w_out.astype(BF16))


def kernel(x, positions, norm_ffn1, ffn1_w_in, ffn1_w_out, norm_mix, w_in, cmp_pos, cmp_w1, cmp_w2,
           w_branch_nsa, w_branch_dil, w_out, norm_ffn2, ffn2_w_in, ffn2_w_out, norm_final):
    b, t, d = x.shape
    depth = norm_ffn1.shape[0]
    if depth == 0:
        raise ValueError("depth must be positive")
    x2 = x.reshape(b * t, d)
    pos_col = positions.reshape(b * t, 1).astype(jnp.int32)
    inv = jnp.power(ROPE_THETA, -jnp.arange(0, ROPE_DIM, 2, dtype=F32) / ROPE_DIM)
    inv_head = jnp.concatenate([inv, inv, jnp.zeros((HEAD_DIM - ROPE_DIM,), F32)])
    inv_lane = jnp.tile(inv_head, LANES // HEAD_DIM).reshape(1, LANES)
    for l in range(depth):
        last = l == depth - 1
        x2 = _ffn(x2, norm_ffn1[l], ffn1_w_in[l].astype(BF16), ffn1_w_out[l].astype(BF16))
        x2 = _mixer(x2, pos_col, inv_lane, b, t, norm_mix[l], w_in[l], cmp_pos[l], cmp_w1[l], cmp_w2[l],
                    w_branch_nsa[l], w_branch_dil[l], w_out[l])
        x2 = _ffn(x2, norm_ffn2[l], ffn2_w_in[l].astype(BF16), ffn2_w_out[l].astype(BF16),
                  final_gain=norm_final if last else None)
    return x2.reshape(b, t, d)
```
